```python
import math
import jax, jax.numpy as jnp
from jax import lax
import numpy as np

D_MODEL = 1024
BATCH = 4
SEQ = 4096
DEPTH = 1
DEC_BATCH = 128
DEC_SEQ = 8
PAST_LEN = 8192
PAGE_SIZE = 128

ATTN_GROUPS = ((128, 1), (512, 4), (2048, 16))
N_GROUPS = 3
HEADS_PER_GROUP = 4
ATTN_HEAD_DIM = 128
N_ATTN_HEADS = N_GROUPS * HEADS_PER_GROUP
ATTN_QKV_W = N_ATTN_HEADS * ATTN_HEAD_DIM
ATTN_OUT_W = HEADS_PER_GROUP * ATTN_HEAD_DIM
BAND_BLOCK = 128
HG_EXPAND = 128
HG_HEADS = D_MODEL // HG_EXPAND
HG_DK = HG_EXPAND
HG_DV = D_MODEL // HG_HEADS
HG_W = HG_HEADS * HG_DK
HG_CHUNK = 64
D_FF = 4 * D_MODEL
SPLITS = [ATTN_QKV_W, 2 * ATTN_QKV_W, 3 * ATTN_QKV_W,
          3 * ATTN_QKV_W + HG_W, 3 * ATTN_QKV_W + 2 * HG_W, 3 * ATTN_QKV_W + 3 * HG_W,
          3 * ATTN_QKV_W + 4 * HG_W, 3 * ATTN_QKV_W + 4 * HG_W + D_MODEL]
N_IN = 3 * ATTN_QKV_W + 4 * HG_W + 2 * D_MODEL
DN_ALPHA = (2 * DEPTH) ** 0.25
DN_BETA = (8 * DEPTH) ** -0.25
LN_EPS = 1e-5
RMS_EPS = 1e-6

kernel_name = "hybrid_dilated_attn_hgrn2_decoder_step"

F32 = jnp.float32


def _alibi_slopes():
    h = jnp.arange(1, N_ATTN_HEADS + 1, dtype=F32)
    return (2.0 ** (-8.0 * h / N_ATTN_HEADS)).reshape(N_GROUPS, HEADS_PER_GROUP)


def _layer_norm(h, g, b, dtype):
    h = h.astype(F32)
    mu = jnp.mean(h, axis=-1, keepdims=True)
    var = jnp.mean(jnp.square(h - mu), axis=-1, keepdims=True)
    return ((h - mu) * lax.rsqrt(var + LN_EPS) * g.astype(F32) + b.astype(F32)).astype(dtype)


def _dilated_prompt(q, k, v, window, dil, slopes):
    b, s, h, dh = q.shape
    n_steps = window // dil
    m = s // dil
    nb = -(-m // BAND_BLOCK)
    mp = nb * BAND_BLOCK

    def to_res(a):
        a = a.astype(F32).reshape(b, m, dil, h, dh).transpose(0, 2, 1, 3, 4).reshape(b * dil, m, h, dh)
        a = jnp.pad(a, ((0, 0), (0, mp - m), (0, 0), (0, 0)))
        return a.reshape(b * dil, nb, BAND_BLOCK, h, dh)

    def band(a):
        prev = jnp.pad(a[:, :-1], ((0, 0), (1, 0), (0, 0), (0, 0), (0, 0)))
        return jnp.concatenate([prev, a], axis=2)

    qb = to_res(q)
    kk = band(to_res(k))
    vv = band(to_res(v))
    scores = jnp.einsum('bnqhd,bnkhd->bnhqk', qb, kk) * (dh ** -0.5)
    qi = jnp.arange(BAND_BLOCK)[:, None]
    kj = jnp.arange(2 * BAND_BLOCK)[None, :]
    steps = BAND_BLOCK + qi - kj
    key_m = (jnp.arange(nb)[:, None] - 1) * BAND_BLOCK + kj
    valid = ((steps >= 0) & (steps <= n_steps))[None] & (key_m >= 0)[:, None, :]
    alibi = -slopes[:, None, None] * (dil * steps).astype(F32)[None]
    scores = jnp.where(valid[None, :, None], scores + alibi[None, None], -jnp.inf)
    lse = jax.nn.logsumexp(scores, axis=-1)
    p = jnp.exp(scores - lse[..., None])
    o = jnp.einsum('bnhqk,bnkhd->bnqhd', p, vv)
    o = o.reshape(b, dil, mp, h, dh)[:, :, :m].transpose(0, 2, 1, 3, 4).reshape(b, s, h, dh)
    lse = lse.transpose(0, 1, 3, 2).reshape(b, dil, mp, h)[:, :, :m].transpose(0, 2, 1, 3).reshape(b, s, h)
    return o, lse


def _dilated_sample(q, k, v, buf, window, dil, slopes):
    bd, t, h, dh = q.shape
    L = buf.shape[1]
    n_steps = window // dil
    ext = jnp.concatenate([buf, jnp.stack([k, v], axis=2).astype(buf.dtype)], axis=1)
    st = jnp.arange(n_steps + 1)[None, :]
    idx = L + jnp.arange(t)[:, None] - dil * st
    valid = idx >= 0
    g = jnp.take(ext, jnp.clip(idx, 0), axis=1).astype(F32)
    scores = jnp.einsum('bthd,btkhd->bhtk', q.astype(F32), g[:, :, :, 0]) * (dh ** -0.5)
    scores = scores - slopes[:, None, None] * (dil * st).astype(F32)
    scores = jnp.where(valid[None, None], scores, -jnp.inf)
    lse = jax.nn.logsumexp(scores, axis=-1)
    p = jnp.exp(scores - lse[..., None])
    o = jnp.einsum('bhtk,btkhd->bthd', p, g[:, :, :, 1])
    new_len = min(window, L + t)
    return o, lse.transpose(0, 2, 1), ext[:, L + t - new_len:]


def _hgrn2_chunked(q, k, v, logf, s0):
    b, l, h, dk = q.shape
    dv = v.shape[-1]
    c = min(HG_CHUNK, l)
    nc = -(-l // c)
    lp = nc * c

    def chunks(a):
        a = jnp.pad(a, ((0, 0), (0, lp - l), (0, 0), (0, 0)))
        return a.reshape(b, nc, c, h, a.shape[-1]).transpose(1, 0, 3, 2, 4)

    causal = jnp.tril(jnp.ones((c, c), dtype=bool))

    def step(state, inp):
        qc, kc, vc, gc = inp
        gcum = jnp.cumsum(gc, axis=2)
        diff = gcum[:, :, :, None, :] - gcum[:, :, None, :, :]
        decay = jnp.exp(jnp.where(causal[:, :, None], diff, -jnp.inf))
        attn = jnp.einsum('bhtk,bhsk,bhtsk->bhts', qc, kc, decay)
        o = (jnp.einsum('bhts,bhsv->bhtv', attn, vc)
             + jnp.einsum('bhtk,bhkv->bhtv', qc * jnp.exp(gcum), state))
        g_last = gcum[:, :, -1:, :]
        new_state = (jnp.exp(g_last[:, :, 0, :, None]) * state
                     + jnp.einsum('bhsk,bhsv->bhkv', kc * jnp.exp(g_last - gcum), vc))
        return new_state, o

    s_fin, o = lax.scan(step, s0, (chunks(q), chunks(k), chunks(v), chunks(logf)))
    o = o.transpose(1, 0, 3, 2, 4).reshape(b, lp, h, dv)[:, :l]
    return o, s_fin


def _layer(x, c, kv_bufs, hg_state, lb, w_ada, b_ada, w_in, hg_norm_w, w_branch_a, w_branch_b,
           w_out, ln1_g, ln1_b, w_up, b_up, w_down, b_down, ln2_g, ln2_b):
    b, s, _ = x.shape
    dtype = x.dtype
    mod = jnp.einsum('bd,dn->bn', jax.nn.silu(c), w_ada) + b_ada
    sh1, sc1, g1, sh2, sc2, g2 = jnp.split(mod[:, None, :], 6, axis=-1)
    u = x * (1 + sc1) + sh1
    proj = jnp.einsum('bsd,dn->bsn', u, w_in)
    qa, ka, va, qh, fh, ih, ogh, gate_a, gate_b = jnp.split(proj, SPLITS, axis=-1)

    def heads(a):
        return a.reshape(b, s, N_GROUPS, HEADS_PER_GROUP, ATTN_HEAD_DIM)
    qa, ka, va = heads(qa), heads(ka), heads(va)
    slopes = _alibi_slopes()
    outs, lses, new_bufs = [], [], []
    for gi, (win, dil) in enumerate(ATTN_GROUPS):
        q_g, k_g, v_g = qa[:, :, gi], ka[:, :, gi], va[:, :, gi]
        if kv_bufs is None:
            o_g, lse_g = _dilated_prompt(q_g, k_g, v_g, win, dil, slopes[gi])
            buf = jnp.stack([k_g, v_g], axis=2)[:, s - min(win, s):]
        else:
            o_g, lse_g, buf = _dilated_sample(q_g, k_g, v_g, kv_bufs[gi], win, dil, slopes[gi])
        outs.append(o_g)
        lses.append(lse_g)
        new_bufs.append(buf)
    w_mix = jax.nn.softmax(jnp.stack(lses), axis=0)
    attn = jnp.sum(w_mix[..., None] * jnp.stack(outs), axis=0).reshape(b, s, ATTN_OUT_W)

    hq = jax.nn.silu(qh.astype(F32)).reshape(b, s, HG_HEADS, HG_DK)
    f = lb + (1.0 - lb) * jax.nn.sigmoid(fh.astype(F32))
    logf = jnp.log(f).reshape(b, s, HG_HEADS, HG_DK)
    hk = (1.0 - f).reshape(b, s, HG_HEADS, HG_DK)
    hv = ih.astype(F32).reshape(b, s, HG_HEADS, HG_DV)
    s0 = (jnp.zeros((b, HG_HEADS, HG_DK, HG_DV), F32) if hg_state is None else hg_state.astype(F32))
    o_h, s_fin = _hgrn2_chunked(hq, hk, hv, logf, s0)
    o_h = o_h * lax.rsqrt(jnp.mean(jnp.square(o_h), axis=-1, keepdims=True) + RMS_EPS) * hg_norm_w.astype(F32)
    hg = o_h.reshape(b, s, HG_W) * jax.nn.silu(ogh.astype(F32))

    branch_a = jnp.einsum('bsn,nd->bsd', attn, w_branch_a)
    branch_b = jnp.einsum('bsn,nd->bsd', hg, w_branch_b)
    merged = jax.nn.sigmoid(gate_a.astype(F32)) * branch_a + jax.nn.sigmoid(gate_b.astype(F32)) * branch_b
    mix = jnp.einsum('bsd,de->bse', merged, w_out)
    x1 = _layer_norm(DN_ALPHA * x + g1 * mix, ln1_g, ln1_b, dtype)

    u2 = x1 * (1 + sc2) + sh2
    hid = jnp.square(jax.nn.relu(jnp.einsum('bsd,df->bsf', u2, w_up) + b_up))
    ff = jnp.einsum('bsf,fd->bsd', hid, w_down) + b_down
    x2 = _layer_norm(DN_ALPHA * x1 + g2 * ff, ln2_g, ln2_b, dtype)
    st_dtype = dtype if hg_state is None else hg_state.dtype
    return x2, new_bufs, s_fin.astype(st_dtype)


def setup_inputs(seed: int = 0) -> dict:
    key = jax.random.key(seed)
    ks = jax.random.split(key, 32)
    nrm = lambda k, shape, sc: jax.random.normal(k, shape, F32) * sc
    L = DEPTH
    return {
        "x_prompt": nrm(ks[0], (BATCH, SEQ, D_MODEL), 1.0),
        "x_sample": nrm(ks[1], (DEC_BATCH, DEC_SEQ, D_MODEL), 1.0),
        "c_prompt": nrm(ks[2], (BATCH, D_MODEL), 1.0),
        "c_sample": nrm(ks[3], (DEC_BATCH, D_MODEL), 1.0),
        "cache_kv_w128": nrm(ks[4], (L, DEC_BATCH, min(ATTN_GROUPS[0][0], PAST_LEN), 2, HEADS_PER_GROUP, ATTN_HEAD_DIM), 1.0),
        "cache_kv_w512": nrm(ks[5], (L, DEC_BATCH, min(ATTN_GROUPS[1][0], PAST_LEN), 2, HEADS_PER_GROUP, ATTN_HEAD_DIM), 1.0),
        "cache_kv_w2048": nrm(ks[6], (L, DEC_BATCH, min(ATTN_GROUPS[2][0], PAST_LEN), 2, HEADS_PER_GROUP, ATTN_HEAD_DIM), 1.0),
        "state_hgrn": nrm(ks[7], (L, DEC_BATCH, HG_HEADS, HG_DK, HG_DV), 0.5),
        "w_ada": nrm(ks[8], (L, D_MODEL, 6 * D_MODEL), 0.5 * D_MODEL ** -0.5),
        "b_ada": nrm(ks[9], (L, 6 * D_MODEL), 0.02),
        "w_in": nrm(ks[10], (L, D_MODEL, N_IN), D_MODEL ** -0.5),
        "lb_param": nrm(ks[11], (L + 1, HG_W), 0.5),
        "hg_norm_w": 1.0 + nrm(ks[12], (L, HG_DV), 0.02),
        "w_branch_a": nrm(ks[13], (L, ATTN_OUT_W, D_MODEL), ATTN_OUT_W ** -0.5),
        "w_branch_b": nrm(ks[14], (L, HG_W, D_MODEL), HG_W ** -0.5),
        "w_out": nrm(ks[15], (L, D_MODEL, D_MODEL), DN_BETA * D_MODEL ** -0.5),
        "ln1_g": 1.0 + nrm(ks[16], (L, D_MODEL), 0.02),
        "ln1_b": nrm(ks[17], (L, D_MODEL), 0.02),
        "w_up": nrm(ks[18], (L, D_MODEL, D_FF), D_MODEL ** -0.5),
        "b_up": nrm(ks[19], (L, D_FF), 0.02),
        "w_down": nrm(ks[20], (L, D_FF, D_MODEL), DN_BETA * D_FF ** -0.5),
        "b_down": nrm(ks[21], (L, D_MODEL), 0.02),
        "ln2_g": 1.0 + nrm(ks[22], (L, D_MODEL), 0.02),
        "ln2_b": nrm(ks[23], (L, D_MODEL), 0.02),
    }


def reference(x_prompt, x_sample, c_prompt, c_sample, cache_kv_w128, cache_kv_w512, cache_kv_w2048,
              state_hgrn, w_ada, b_ada, w_in, lb_param, hg_norm_w, w_branch_a, w_branch_b, w_out,
              ln1_g, ln1_b, w_up, b_up, w_down, b_down, ln2_g, ln2_b):
    lower_bounds = jnp.cumsum(jax.nn.softmax(lb_param.astype(F32), axis=0), axis=0)
    yp, ys = x_prompt, x_sample
    p128, p512, p2048, php = [], [], [], []
    s128, s512, s2048, shs = [], [], [], []
    for l in range(DEPTH):
        yp, bufs_p, st_p = _layer(yp, c_prompt, None, None, lower_bounds[l], w_ada[l], b_ada[l], w_in[l],
                                  hg_norm_w[l], w_branch_a[l], w_branch_b[l], w_out[l], ln1_g[l], ln1_b[l],
                                  w_up[l], b_up[l], w_down[l], b_down[l], ln2_g[l], ln2_b[l])
        ys, bufs_s, st_s = _layer(ys, c_sample, (cache_kv_w128[l], cache_kv_w512[l], cache_kv_w2048[l]),
                                  state_hgrn[l], lower_bounds[l], w_ada[l], b_ada[l], w_in[l],
                                  hg_norm_w[l], w_branch_a[l], w_branch_b[l], w_out[l], ln1_g[l], ln1_b[l],
                                  w_up[l], b_up[l], w_down[l], b_down[l], ln2_g[l], ln2_b[l])
        p128.append(bufs_p[0]); p512.append(bufs_p[1]); p2048.append(bufs_p[2]); php.append(st_p)
        s128.append(bufs_s[0]); s512.append(bufs_s[1]); s2048.append(bufs_s[2]); shs.append(st_s)
    return (yp, ys, jnp.stack(p128), jnp.stack(p512), jnp.stack(p2048), jnp.stack(php),
            jnp.stack(s128), jnp.stack(s512), jnp.stack(s2048), jnp.stack(shs))
```

```python
import functools
import math

import jax
import jax.numpy as jnp
from jax import lax
from jax.experimental import pallas as pl
from jax.experimental.pallas import tpu as pltpu

F32 = jnp.float32
BF16 = jnp.bfloat16

ATTN_GROUPS = ((128, 1), (512, 4), (2048, 16))
N_GROUPS = len(ATTN_GROUPS)
HEADS_PER_GROUP = 4
HEAD_DIM = 128
N_ATTN_HEADS = N_GROUPS * HEADS_PER_GROUP
ATTN_W = N_ATTN_HEADS * HEAD_DIM
GROUP_W = HEADS_PER_GROUP * HEAD_DIM
BAND = 128
HG_DK = 128
HG_DV = 128
LN_EPS = 1e-5
RMS_EPS = 1e-6
NEG = -1e30

SUBLANES = 8
LANES = 128
VMEM_LIMIT_BYTES = 56 * 1024 * 1024
COL_CHUNK = 512

NT_DIMS = (((1,), (1,)), ((), ()))
TN_DIMS = (((0,), (0,)), ((), ()))


def _alibi_slope(group, head):
    return 2.0 ** (-8.0 * (group * HEADS_PER_GROUP + head + 1) / N_ATTN_HEADS)


def _cparams(*sem):
    return pltpu.CompilerParams(dimension_semantics=sem, vmem_limit_bytes=VMEM_LIMIT_BYTES)


def _resident(shape):
    nd = len(shape)
    return pl.BlockSpec(shape, lambda *_: (0,) * nd, pipeline_mode=pl.Buffered(1))


def _sigmoid(x):
    return 1.0 / (1.0 + jnp.exp(-x))


def _layer_norm(h, g, b):
    mu = jnp.mean(h, axis=-1, keepdims=True)
    hc = h - mu
    var = jnp.mean(hc * hc, axis=-1, keepdims=True)
    return hc * lax.rsqrt(var + LN_EPS) * g + b


def _mod_kernel(c_ref, w_ref, b_ref, o_ref):
    c = c_ref[...]
    a = (c * _sigmoid(c)).astype(BF16)
    o_ref[...] = jnp.dot(a, w_ref[...], preferred_element_type=F32) + b_ref[...]


def _modulation(c, w_ada, b_ada):
    n, d = c.shape
    nout = w_ada.shape[1]
    tn = 1024
    return pl.pallas_call(
        _mod_kernel,
        out_shape=jax.ShapeDtypeStruct((n, nout), F32),
        grid=(nout // tn,),
        in_specs=[pl.BlockSpec((n, d), lambda j: (0, 0)),
                  pl.BlockSpec((d, tn), lambda j: (0, j)),
                  pl.BlockSpec((1, tn), lambda j: (0, j))],
        out_specs=pl.BlockSpec((n, tn), lambda j: (0, j)),
        compiler_params=_cparams("arbitrary"),
        name="modulation",
    )(c, w_ada, b_ada.reshape(1, nout))


def _inproj_kernel(x_ref, mod_ref, w_ref, lbp_ref, attn_ref, hq_ref, hk_ref, hv_ref, logf_ref,
                   gates_ref, *, layer):
    gb, rb, d = x_ref.shape
    tm = gb * rb
    hg_w = hq_ref.shape[1]
    act = attn_ref.dtype
    sh = mod_ref[:, :, 0:d]
    sc = mod_ref[:, :, d:2 * d]
    u = (x_ref[...] * (1.0 + sc) + sh).reshape(tm, d).astype(BF16)

    def proj(c0):
        return jnp.dot(u, w_ref[:, c0:c0 + COL_CHUNK], preferred_element_type=F32)

    for c0 in range(0, 3 * ATTN_W, COL_CHUNK):
        attn_ref[:, c0:c0 + COL_CHUNK] = proj(c0).astype(act)
    base = 3 * ATTN_W
    for c0 in range(0, hg_w, COL_CHUNK):
        y = proj(base + c0)
        hq_ref[:, c0:c0 + COL_CHUNK] = (y * _sigmoid(y)).astype(act)
    lbp = lbp_ref[...]
    e = jnp.exp(lbp - jnp.max(lbp, axis=0, keepdims=True))
    lb = jnp.sum(e[0:layer + 1], axis=0, keepdims=True) / jnp.sum(e, axis=0, keepdims=True)
    base += hg_w
    for c0 in range(0, hg_w, COL_CHUNK):
        lbc = lb[:, c0:c0 + COL_CHUNK]
        f = lbc + (1.0 - lbc) * _sigmoid(proj(base + c0))
        logf_ref[:, c0:c0 + COL_CHUNK] = jnp.log(f)
        hk_ref[:, c0:c0 + COL_CHUNK] = (1.0 - f).astype(act)
    base += hg_w
    for c0 in range(0, hg_w, COL_CHUNK):
        hv_ref[:, c0:c0 + COL_CHUNK] = proj(base + c0).astype(act)
    base += hg_w
    for c0 in range(0, hg_w, COL_CHUNK):
        y = proj(base + c0)
        gates_ref[:, c0:c0 + COL_CHUNK] = (y * _sigmoid(y)).astype(act)
    base += hg_w
    for c0 in range(0, 2 * d, COL_CHUNK):
        gates_ref[:, hg_w + c0:hg_w + c0 + COL_CHUNK] = _sigmoid(proj(base + c0)).astype(act)


def _in_projection(x3, mod3, w_in, lb_param, layer, gb, rb, act):
    g, r, d = x3.shape
    n = g * r
    tm = gb * rb
    hg_w = lb_param.shape[1]
    rt = r // rb
    row = lambda gi, ri: (gi * rt + ri, 0)
    out_shapes = (
        jax.ShapeDtypeStruct((n, 3 * ATTN_W), act),
        jax.ShapeDtypeStruct((n, hg_w), act),
        jax.ShapeDtypeStruct((n, hg_w), act),
        jax.ShapeDtypeStruct((n, hg_w), act),
        jax.ShapeDtypeStruct((n, hg_w), F32),
        jax.ShapeDtypeStruct((n, hg_w + 2 * d), act),
    )
    return pl.pallas_call(
        functools.partial(_inproj_kernel, layer=layer),
        out_shape=out_shapes,
        grid=(g // gb, rt),
        in_specs=[pl.BlockSpec((gb, rb, d), lambda gi, ri: (gi, ri, 0)),
                  pl.BlockSpec((gb, 1, mod3.shape[2]), lambda gi, ri: (gi, 0, 0)),
                  _resident(w_in.shape),
                  _resident(lb_param.shape)],
        out_specs=tuple(pl.BlockSpec((tm, s.shape[1]), row) for s in out_shapes),
        compiler_params=_cparams("arbitrary", "arbitrary"),
        name="in_projection",
    )(x3, mod3, w_in, lb_param)


def _attn_prompt_kernel(q_ref, kp_ref, ko_ref, vp_ref, vo_ref, o_ref, lse_ref, *, group, dil, n_steps):
    blk = pl.program_id(2)
    qi = lax.broadcasted_iota(jnp.int32, (BAND, BAND), 0)
    kj = lax.broadcasted_iota(jnp.int32, (BAND, BAND), 1)
    steps_own = qi - kj
    steps_prev = steps_own + BAND
    ok_own = (steps_own >= 0) & (steps_own <= n_steps)
    ok_prev = (steps_prev <= n_steps) & (blk > 0)
    scale = HEAD_DIM ** -0.5
    for h in range(HEADS_PER_GROUP):
        sl = slice(h * HEAD_DIM, (h + 1) * HEAD_DIM)
        slope = _alibi_slope(group, h) * dil
        q = q_ref[0, :, sl]
        s_own = lax.dot_general(q, ko_ref[0, :, sl], NT_DIMS, preferred_element_type=F32)
        s_prev = lax.dot_general(q, kp_ref[0, :, sl], NT_DIMS, preferred_element_type=F32)
        s_own = jnp.where(ok_own, s_own * scale - slope * steps_own.astype(F32), NEG)
        s_prev = jnp.where(ok_prev, s_prev * scale - slope * steps_prev.astype(F32), NEG)
        m = jnp.maximum(jnp.max(s_own, axis=-1, keepdims=True), jnp.max(s_prev, axis=-1, keepdims=True))
        p_own = jnp.exp(s_own - m)
        p_prev = jnp.exp(s_prev - m)
        l = jnp.sum(p_own, axis=-1, keepdims=True) + jnp.sum(p_prev, axis=-1, keepdims=True)
        o = (jnp.dot(p_own.astype(BF16), vo_ref[0, :, sl], preferred_element_type=F32)
             + jnp.dot(p_prev.astype(BF16), vp_ref[0, :, sl], preferred_element_type=F32))
        o_ref[0, :, sl] = (o / l).astype(o_ref.dtype)
        lse_ref[0, :, sl] = jnp.broadcast_to(m + jnp.log(l), (BAND, HEAD_DIM))


def _attention_prompt(qkv, b, s, group):
    window, dil = ATTN_GROUPS[group]
    m = s // dil
    nb = m // BAND
    ncb = 3 * ATTN_W // GROUP_W
    qkv_r = qkv.reshape(b, m, dil * 3 * ATTN_W)
    qcol, kcol, vcol = group, N_GROUPS + group, 2 * N_GROUPS + group
    blk = (1, BAND, GROUP_W)
    own = lambda col: (lambda bi, r, i: (bi, i, r * ncb + col))
    prev = lambda col: (lambda bi, r, i: (bi, jnp.maximum(i - 1, 0), r * ncb + col))
    o, lse = pl.pallas_call(
        functools.partial(_attn_prompt_kernel, group=group, dil=dil, n_steps=window // dil),
        out_shape=(jax.ShapeDtypeStruct((b, m, dil * GROUP_W), BF16),
                   jax.ShapeDtypeStruct((b, m, dil * GROUP_W), F32)),
        grid=(b, dil, nb),
        in_specs=[pl.BlockSpec(blk, own(qcol)),
                  pl.BlockSpec(blk, prev(kcol)), pl.BlockSpec(blk, own(kcol)),
                  pl.BlockSpec(blk, prev(vcol)), pl.BlockSpec(blk, own(vcol))],
        out_specs=(pl.BlockSpec(blk, lambda bi, r, i: (bi, i, r)),
                   pl.BlockSpec(blk, lambda bi, r, i: (bi, i, r))),
        compiler_params=_cparams("arbitrary", "arbitrary", "arbitrary"),
        name=f"attn_prompt_g{group}",
    )(qkv_r, qkv_r, qkv_r, qkv_r, qkv_r)
    return o.reshape(b * s, GROUP_W), lse.reshape(b * s, GROUP_W)


def _attn_sample_kernel(qkv_ref, cache_ref, newc_ref, o_ref, lse_ref, *, group, dil, n_steps):
    t = qkv_ref.shape[0]
    length = cache_ref.shape[1]
    qoff = group * GROUP_W
    koff = ATTN_W + group * GROUP_W
    voff = 2 * ATTN_W + group * GROUP_W
    newc_ref[0, 0:length - t, :] = cache_ref[0, t:length, :]
    newc_ref[0, length - t:length, 0:GROUP_W] = qkv_ref[:, koff:koff + GROUP_W]
    newc_ref[0, length - t:length, GROUP_W:2 * GROUP_W] = qkv_ref[:, voff:voff + GROUP_W]
    j_c = lax.broadcasted_iota(jnp.int32, (t, length), 0)
    r_c = lax.broadcasted_iota(jnp.int32, (t, length), 1)
    dist_c = length + j_c - r_c
    ok_c = ((dist_c & (dil - 1)) == 0) & (dist_c <= n_steps * dil)
    j_n = lax.broadcasted_iota(jnp.int32, (t, t), 0)
    r_n = lax.broadcasted_iota(jnp.int32, (t, t), 1)
    dist_n = j_n - r_n
    ok_n = (dist_n >= 0) & ((dist_n & (dil - 1)) == 0) & (dist_n <= n_steps * dil)
    scale = HEAD_DIM ** -0.5
    for h in range(HEADS_PER_GROUP):
        sl = slice(h * HEAD_DIM, (h + 1) * HEAD_DIM)
        slv = slice(GROUP_W + h * HEAD_DIM, GROUP_W + (h + 1) * HEAD_DIM)
        slope = _alibi_slope(group, h)
        q = qkv_ref[:, qoff + h * HEAD_DIM:qoff + (h + 1) * HEAD_DIM].astype(BF16)
        k_new = qkv_ref[:, koff + h * HEAD_DIM:koff + (h + 1) * HEAD_DIM].astype(BF16)
        v_new = qkv_ref[:, voff + h * HEAD_DIM:voff + (h + 1) * HEAD_DIM].astype(BF16)
        s_c = lax.dot_general(q, cache_ref[0, :, sl].astype(BF16), NT_DIMS, preferred_element_type=F32)
        s_n = lax.dot_general(q, k_new, NT_DIMS, preferred_element_type=F32)
        s_c = jnp.where(ok_c, s_c * scale - slope * dist_c.astype(F32), NEG)
        s_n = jnp.where(ok_n, s_n * scale - slope * dist_n.astype(F32), NEG)
        m = jnp.maximum(jnp.max(s_c, axis=-1, keepdims=True), jnp.max(s_n, axis=-1, keepdims=True))
        p_c = jnp.exp(s_c - m)
        p_n = jnp.exp(s_n - m)
        l = jnp.sum(p_c, axis=-1, keepdims=True) + jnp.sum(p_n, axis=-1, keepdims=True)
        o = (jnp.dot(p_c.astype(BF16), cache_ref[0, :, slv].astype(BF16), preferred_element_type=F32)
             + jnp.dot(p_n.astype(BF16), v_new, preferred_element_type=F32))
        o_ref[:, sl] = o / l
        lse_ref[:, sl] = jnp.broadcast_to(m + jnp.log(l), (t, HEAD_DIM))


def _attention_sample(qkv, cache, t, group):
    window, dil = ATTN_GROUPS[group]
    bd, length, cw = cache.shape
    return pl.pallas_call(
        functools.partial(_attn_sample_kernel, group=group, dil=dil, n_steps=window // dil),
        out_shape=(jax.ShapeDtypeStruct(cache.shape, cache.dtype),
                   jax.ShapeDtypeStruct((bd * t, GROUP_W), F32),
                   jax.ShapeDtypeStruct((bd * t, GROUP_W), F32)),
        grid=(bd,),
        in_specs=[pl.BlockSpec((t, 3 * ATTN_W), lambda bi: (bi, 0)),
                  pl.BlockSpec((1, length, cw), lambda bi: (bi, 0, 0))],
        out_specs=(pl.BlockSpec((1, length, cw), lambda bi: (bi, 0, 0)),
                   pl.BlockSpec((t, GROUP_W), lambda bi: (bi, 0)),
                   pl.BlockSpec((t, GROUP_W), lambda bi: (bi, 0))),
        compiler_params=_cparams("arbitrary"),
        name=f"attn_sample_g{group}",
    )(qkv, cache)


def _cumsum_rows(x, row):
    d = 1
    while d < x.shape[0]:
        x = x + jnp.where(row >= d, pltpu.roll(x, d, 0), 0.0)
        d *= 2
    return x


def _hgrn_chunk(q, k, v, lf, st):
    c = q.shape[0]
    row = lax.broadcasted_iota(jnp.int32, (c, HG_DK), 0)
    g = _cumsum_rows(lf, row)
    o = jnp.sum(q * k, axis=-1, keepdims=True) * v
    sub = row & (SUBLANES - 1)
    for d in range(1, SUBLANES):
        ok = sub >= d
        e = jnp.exp(jnp.where(ok, g - pltpu.roll(g, d, 0), 0.0))
        a = jnp.sum(jnp.where(ok, q * pltpu.roll(k, d, 0) * e, 0.0), axis=-1, keepdims=True)
        o = o + a * pltpu.roll(v, d, 0)
    if c > SUBLANES:
        ri = lax.broadcasted_iota(jnp.int32, (c, c), 0)
        ci = lax.broadcasted_iota(jnp.int32, (c, c), 1)
        amat = jnp.zeros((c, c), F32)
        n = 2 * SUBLANES
        while n <= c:
            half = n // 2
            gmid = jnp.broadcast_to(g.reshape(c // n, n, HG_DK)[:, half - 1:half, :],
                                    (c // n, n, HG_DK)).reshape(c, HG_DK)
            second = (row & (n - 1)) >= half
            e = jnp.exp(jnp.where(second, g - gmid, gmid - g))
            qt = jnp.where(second, q * e, 0.0).astype(BF16)
            kt = jnp.where(second, 0.0, k * e).astype(BF16)
            a_n = lax.dot_general(qt, kt, NT_DIMS, preferred_element_type=F32)
            shift = n.bit_length() - 1
            amat = amat + jnp.where((ri >> shift) == (ci >> shift), a_n, 0.0)
            n *= 2
        o = o + jnp.dot(amat.astype(BF16), v.astype(BF16), preferred_element_type=F32)
    qg = (q * jnp.exp(g)).astype(BF16)
    o = o + lax.dot_general(qg, st.astype(BF16), NT_DIMS, preferred_element_type=F32)
    g_last = g[c - 1:c, :]
    kd = (k * jnp.exp(g_last - g)).astype(BF16)
    st_new = st * jnp.exp(g_last) + lax.dot_general(v.astype(BF16), kd, TN_DIMS, preferred_element_type=F32)
    return o, st_new


def _hgrn_output(o, og, nw):
    return o * lax.rsqrt(jnp.mean(o * o, axis=-1, keepdims=True) + RMS_EPS) * nw * og


def _hgrn_prompt_kernel(q_ref, k_ref, v_ref, lf_ref, og_ref, nw_ref, hg_ref, sfin_ref, st_ref):
    ci = pl.program_id(2)

    @pl.when(ci == 0)
    def _():
        st_ref[...] = jnp.zeros_like(st_ref)

    o, st_new = _hgrn_chunk(q_ref[...].astype(F32), k_ref[...].astype(F32), v_ref[...].astype(F32),
                            lf_ref[...], st_ref[...])
    st_ref[...] = st_new
    hg_ref[...] = _hgrn_output(o, og_ref[...].astype(F32), nw_ref[...]).astype(hg_ref.dtype)

    @pl.when(ci == pl.num_programs(2) - 1)
    def _():
        sfin_ref[0, 0] = st_new.T


def _hgrn_prompt(hq, hk, hv, logf, gates, norm_w, b, s, chunk):
    hg_w = hq.shape[1]
    heads = hg_w // HG_DK
    nc = s // chunk
    blk = pl.BlockSpec((chunk, HG_DK), lambda bi, h, ci: (bi * nc + ci, h))
    return pl.pallas_call(
        _hgrn_prompt_kernel,
        out_shape=(jax.ShapeDtypeStruct((b * s, hg_w), BF16),
                   jax.ShapeDtypeStruct((b, heads, HG_DK, HG_DV), F32)),
        grid=(b, heads, nc),
        in_specs=[blk, blk, blk, blk, blk, pl.BlockSpec((1, HG_DV), lambda bi, h, ci: (0, 0))],
        out_specs=(blk, pl.BlockSpec((1, 1, HG_DK, HG_DV), lambda bi, h, ci: (bi, h, 0, 0))),
        scratch_shapes=[pltpu.VMEM((HG_DV, HG_DK), F32)],
        compiler_params=_cparams("arbitrary", "arbitrary", "arbitrary"),
        name="hgrn_prompt",
    )(hq, hk, hv, logf, gates, norm_w.reshape(1, HG_DV))


def _hgrn_sample_kernel(q_ref, k_ref, v_ref, lf_ref, og_ref, nw_ref, s_ref, hg_ref, snew_ref):
    heads = s_ref.shape[1]
    for h in range(heads):
        sl = slice(h * HG_DK, (h + 1) * HG_DK)
        o, st_new = _hgrn_chunk(q_ref[:, sl], k_ref[:, sl], v_ref[:, sl], lf_ref[:, sl], s_ref[0, h].T)
        snew_ref[0, h] = st_new.T
        hg_ref[:, sl] = _hgrn_output(o, og_ref[:, sl], nw_ref[...])


def _hgrn_sample(hq, hk, hv, logf, gates, norm_w, state, t):
    bd, heads = state.shape[0], state.shape[1]
    hg_w = hq.shape[1]
    blk = pl.BlockSpec((t, hg_w), lambda bi: (bi, 0))
    sblk = pl.BlockSpec((1, heads, HG_DK, HG_DV), lambda bi: (bi, 0, 0, 0))
    return pl.pallas_call(
        _hgrn_sample_kernel,
        out_shape=(jax.ShapeDtypeStruct((bd * t, hg_w), F32),
                   jax.ShapeDtypeStruct(state.shape, state.dtype)),
        grid=(bd,),
        in_specs=[blk, blk, blk, blk, blk, pl.BlockSpec((1, HG_DV), lambda bi: (0, 0)), sblk],
        out_specs=(blk, sblk),
        compiler_params=_cparams("arbitrary"),
        name="hgrn_sample",
    )(hq, hk, hv, logf, gates, norm_w.reshape(1, HG_DV), state)


def _merge_kernel(x_ref, mod_ref, o0_ref, o1_ref, o2_ref, l0_ref, l1_ref, l2_ref, hg_ref, gates_ref,
                  wa_ref, wb_ref, wo_ref, lng_ref, lnb_ref, x1_ref, *, alpha):
    gb, rb, d = x_ref.shape
    tm = gb * rb
    hg_w = hg_ref.shape[1]
    l0, l1, l2 = l0_ref[...], l1_ref[...], l2_ref[...]
    m = jnp.maximum(jnp.maximum(l0, l1), l2)
    e0, e1, e2 = jnp.exp(l0 - m), jnp.exp(l1 - m), jnp.exp(l2 - m)
    attn = (e0 * o0_ref[...].astype(F32) + e1 * o1_ref[...].astype(F32)
            + e2 * o2_ref[...].astype(F32)) / (e0 + e1 + e2)
    branch_a = jnp.dot(attn.astype(BF16), wa_ref[...], preferred_element_type=F32)
    branch_b = jnp.dot(hg_ref[...].astype(BF16), wb_ref[...], preferred_element_type=F32)
    gate_a = gates_ref[:, hg_w:hg_w + d].astype(F32)
    gate_b = gates_ref[:, hg_w + d:hg_w + 2 * d].astype(F32)
    merged = gate_a * branch_a + gate_b * branch_b
    mix = jnp.dot(merged.astype(BF16), wo_ref[...], preferred_element_type=F32)
    g1 = mod_ref[:, :, 2 * d:3 * d]
    h = alpha * x_ref[...] + g1 * mix.reshape(gb, rb, d)
    x1_ref[...] = _layer_norm(h, lng_ref[...], lnb_ref[...])


def _merge(x3, mod3, outs, lses, hg, gates, wa, wb, wo, ln_g, ln_b, gb, rb, alpha):
    g, r, d = x3.shape
    tm = gb * rb
    rt = r // rb
    row = lambda gi, ri: (gi * rt + ri, 0)
    rows = lambda a: pl.BlockSpec((tm, a.shape[1]), row)
    tile3 = pl.BlockSpec((gb, rb, d), lambda gi, ri: (gi, ri, 0))
    return pl.pallas_call(
        functools.partial(_merge_kernel, alpha=alpha),
        out_shape=jax.ShapeDtypeStruct(x3.shape, F32),
        grid=(g // gb, rt),
        in_specs=[tile3, pl.BlockSpec((gb, 1, mod3.shape[2]), lambda gi, ri: (gi, 0, 0)),
                  rows(outs[0]), rows(outs[1]), rows(outs[2]), rows(lses[0]), rows(lses[1]), rows(lses[2]),
                  rows(hg), rows(gates),
                  _resident(wa.shape), _resident(wb.shape), _resident(wo.shape),
                  _resident((1, d)), _resident((1, d))],
        out_specs=tile3,
        compiler_params=_cparams("arbitrary", "arbitrary"),
        name="merge",
    )(x3, mod3, *outs, *lses, hg, gates, wa, wb, wo, ln_g.reshape(1, d), ln_b.reshape(1, d))


def _mlp_kernel(x1_ref, mod_ref, wu_ref, bu_ref, wd_ref, bd_ref, lng_ref, lnb_ref, x2_ref, *, alpha):
    gb, rb, d = x1_ref.shape
    tm = gb * rb
    dff = wu_ref.shape[1]
    x1 = x1_ref[...]
    sh = mod_ref[:, :, 3 * d:4 * d]
    sc = mod_ref[:, :, 4 * d:5 * d]
    g2 = mod_ref[:, :, 5 * d:6 * d]
    u = (x1 * (1.0 + sc) + sh).reshape(tm, d).astype(BF16)
    acc = jnp.zeros((tm, d), F32)
    for c0 in range(0, dff, COL_CHUNK):
        hid = jnp.dot(u, wu_ref[:, c0:c0 + COL_CHUNK], preferred_element_type=F32) + bu_ref[:, c0:c0 + COL_CHUNK]
        hid = jnp.square(jnp.maximum(hid, 0.0))
        acc = acc + jnp.dot(hid.astype(BF16), wd_ref[c0:c0 + COL_CHUNK, :], preferred_element_type=F32)
    ff = acc + bd_ref[...]
    h = alpha * x1 + g2 * ff.reshape(gb, rb, d)
    x2_ref[...] = _layer_norm(h, lng_ref[...], lnb_ref[...])


def _mlp(x3, mod3, wu, bu, wd, bd, ln_g, ln_b, gb, rb, alpha):
    g, r, d = x3.shape
    dff = wu.shape[1]
    tile3 = pl.BlockSpec((gb, rb, d), lambda gi, ri: (gi, ri, 0))
    return pl.pallas_call(
        functools.partial(_mlp_kernel, alpha=alpha),
        out_shape=jax.ShapeDtypeStruct(x3.shape, F32),
        grid=(g // gb, r // rb),
        in_specs=[tile3, pl.BlockSpec((gb, 1, mod3.shape[2]), lambda gi, ri: (gi, 0, 0)),
                  _resident(wu.shape), _resident((1, dff)), _resident(wd.shape), _resident((1, d)),
                  _resident((1, d)), _resident((1, d))],
        out_specs=tile3,
        compiler_params=_cparams("arbitrary", "arbitrary"),
        name="mlp",
    )(x3, mod3, wu, bu.reshape(1, dff), wd, bd.reshape(1, d), ln_g.reshape(1, d), ln_b.reshape(1, d))


PROMPT_ROWS = 256
SAMPLE_SEQS = 32
HGRN_CHUNK = 128


def _kv_tail(qkv, b, s, group, rows):
    q3 = qkv.reshape(b, s, 3 * ATTN_W)
    k = q3[:, s - rows:, ATTN_W + group * GROUP_W:ATTN_W + (group + 1) * GROUP_W]
    v = q3[:, s - rows:, 2 * ATTN_W + group * GROUP_W:2 * ATTN_W + (group + 1) * GROUP_W]
    return jnp.stack([k, v], axis=2).astype(F32).reshape(b, rows, 2, HEADS_PER_GROUP, HEAD_DIM)


def _layer(x, mod, kv_bufs, hg_state, layer, w, alpha):
    b, s, d = x.shape
    mod3 = mod.reshape(b, 1, mod.shape[1])
    prompt = kv_bufs is None
    if prompt:
        gb, rb, act = 1, min(PROMPT_ROWS, s), BF16
    else:
        gb, rb, act = min(SAMPLE_SEQS, b), s, F32
    qkv, hq, hk, hv, logf, gates = _in_projection(x, mod3, w["w_in"], w["lb_param"], layer, gb, rb, act)

    outs, lses, new_bufs = [], [], []
    for gi, (window, dil) in enumerate(ATTN_GROUPS):
        if prompt:
            o_g, lse_g = _attention_prompt(qkv, b, s, gi)
            buf = _kv_tail(qkv, b, s, gi, min(window, s))
        else:
            cache = kv_bufs[gi]
            length = cache.shape[1]
            buf, o_g, lse_g = _attention_sample(qkv, cache.reshape(b, length, 2 * GROUP_W), s, gi)
            buf = buf.reshape(cache.shape)
        outs.append(o_g)
        lses.append(lse_g)
        new_bufs.append(buf)

    if prompt:
        hg, s_fin = _hgrn_prompt(hq, hk, hv, logf, gates, w["hg_norm_w"], b, s, min(HGRN_CHUNK, s))
    else:
        hg, s_fin = _hgrn_sample(hq, hk, hv, logf, gates, w["hg_norm_w"], hg_state, s)

    x1 = _merge(x, mod3, outs, lses, hg, gates, w["w_branch_a"], w["w_branch_b"], w["w_out"],
                w["ln1_g"], w["ln1_b"], gb, rb, alpha)
    x2 = _mlp(x1, mod3, w["w_up"], w["b_up"], w["w_down"], w["b_down"], w["ln2_g"], w["ln2_b"], gb, rb, alpha)
    return x2, new_bufs, s_fin


def kernel(x_prompt, x_sample, c_prompt, c_sample, cache_kv_w128, cache_kv_w512, cache_kv_w2048, state_hgrn,
           w_ada, b_ada, w_in, lb_param, hg_norm_w, w_branch_a, w_branch_b, w_out, ln1_g, ln1_b, w_up, b_up,
           w_down, b_down, ln2_g, ln2_b):
    depth = w_ada.shape[0]
    alpha = (2 * depth) ** 0.25
    caches = (cache_kv_w128, cache_kv_w512, cache_kv_w2048)
    for (window, dil), cache in zip(ATTN_GROUPS, caches):
        assert window // dil == BAND and cache.shape[2] == window
    assert x_prompt.shape[1] % (BAND * ATTN_GROUPS[-1][1]) == 0 and x_sample.shape[1] == SUBLANES
    nb = c_prompt.shape[0]
    yp, ys = x_prompt, x_sample
    p_bufs, p_states, s_bufs, s_states = [], [], [], []
    for l in range(depth):
        w = dict(w_in=w_in[l].astype(BF16), lb_param=lb_param, hg_norm_w=hg_norm_w[l],
                 w_branch_a=w_branch_a[l].astype(BF16), w_branch_b=w_branch_b[l].astype(BF16),
                 w_out=w_out[l].astype(BF16), ln1_g=ln1_g[l], ln1_b=ln1_b[l],
                 w_up=w_up[l].astype(BF16), b_up=b_up[l], w_down=w_down[l].astype(BF16), b_down=b_down[l],
                 ln2_g=ln2_g[l], ln2_b=ln2_b[l])
        mod = _modulation(jnp.concatenate([c_prompt, c_sample], axis=0), w_ada[l].astype(BF16), b_ada[l])
        yp, bufs_p, st_p = _layer(yp, mod[:nb], None, None, l, w, alpha)
        ys, bufs_s, st_s = _layer(ys, mod[nb:], tuple(c[l] for c in caches), state_hgrn[l], l, w, alpha)
        p_bufs.append(bufs_p)
        p_states.append(st_p)
        s_bufs.append(bufs_s)
        s_states.append(st_s)
    stack = lambda bufs, gi: jnp.stack([bl[gi] for bl in bufs])
    return (yp, ys, stack(p_bufs, 0), stack(p_bufs, 1), stack(p_bufs, 2), jnp.stack(p_states),
            stack(s_bufs, 0), stack(s_bufs, 1), stack(s_bufs, 2), jnp.stack(s_states))
```

```python
import functools
import math

import jax
import jax.numpy as jnp
from jax import lax
from jax.experimental import pallas as pl
from jax.experimental.pallas import tpu as pltpu

F32 = jnp.float32
BF16 = jnp.bfloat16

ATTN_GROUPS = ((128, 1), (512, 4), (2048, 16))
N_GROUPS = len(ATTN_GROUPS)
HEADS_PER_GROUP = 4
HEAD_DIM = 128
N_ATTN_HEADS = N_GROUPS * HEADS_PER_GROUP
ATTN_W = N_ATTN_HEADS * HEAD_DIM
GROUP_W = HEADS_PER_GROUP * HEAD_DIM
BAND = 128
HG_DK = 128
HG_DV = 128
LN_EPS = 1e-5
RMS_EPS = 1e-6
NEG = -1e30

SUBLANES = 8
LANES = 128
VMEM_LIMIT_BYTES = 56 * 1024 * 1024
COL_CHUNK = 512

NT_DIMS = (((1,), (1,)), ((), ()))
TN_DIMS = (((0,), (0,)), ((), ()))


def _alibi_slope(group, head):
    return 2.0 ** (-8.0 * (group * HEADS_PER_GROUP + head + 1) / N_ATTN_HEADS)


def _cparams(*sem):
    return pltpu.CompilerParams(dimension_semantics=sem, vmem_limit_bytes=VMEM_LIMIT_BYTES)


def _resident(shape):
    nd = len(shape)
    return pl.BlockSpec(shape, lambda *_: (0,) * nd, pipeline_mode=pl.Buffered(1))


def _sigmoid(x):
    return 1.0 / (1.0 + jnp.exp(-x))


def _layer_norm(h, g, b):
    mu = jnp.mean(h, axis=-1, keepdims=True)
    hc = h - mu
    var = jnp.mean(hc * hc, axis=-1, keepdims=True)
    return hc * lax.rsqrt(var + LN_EPS) * g + b


def _mod_kernel(c_ref, w_ref, b_ref, o_ref):
    c = c_ref[...]
    a = (c * _sigmoid(c)).astype(BF16)
    o_ref[...] = jnp.dot(a, w_ref[...], preferred_element_type=F32) + b_ref[...]


def _modulation(c, w_ada, b_ada):
    n, d = c.shape
    nout = w_ada.shape[1]
    tn = 1024
    return pl.pallas_call(
        _mod_kernel,
        out_shape=jax.ShapeDtypeStruct((n, nout), F32),
        grid=(nout // tn,),
        in_specs=[pl.BlockSpec((n, d), lambda j: (0, 0)),
                  pl.BlockSpec((d, tn), lambda j: (0, j)),
                  pl.BlockSpec((1, tn), lambda j: (0, j))],
        out_specs=pl.BlockSpec((n, tn), lambda j: (0, j)),
        compiler_params=_cparams("arbitrary"),
        name="modulation",
    )(c, w_ada, b_ada.reshape(1, nout))


def _store_heads(ref, c0, y):
    for j in range(y.shape[1] // HG_DK):
        ref[c0 // HG_DK + j] = y[:, j * HG_DK:(j + 1) * HG_DK].astype(ref.dtype)


def _inproj_kernel(x_ref, mod_ref, w_ref, lbp_ref, attn_ref, hq_ref, hk_ref, hv_ref, logf_ref, og_ref,
                   gates_ref, *, layer):
    gb, rb, d = x_ref.shape
    tm = gb * rb
    hg_w = hq_ref.shape[0] * HG_DK
    act = attn_ref.dtype
    sh = mod_ref[:, :, 0:d]
    sc = mod_ref[:, :, d:2 * d]
    u = (x_ref[...] * (1.0 + sc) + sh).reshape(tm, d).astype(BF16)

    def proj(c0):
        return jnp.dot(u, w_ref[:, c0:c0 + COL_CHUNK], preferred_element_type=F32)

    for c0 in range(0, 3 * ATTN_W, COL_CHUNK):
        attn_ref[:, c0:c0 + COL_CHUNK] = proj(c0).astype(act)
    base = 3 * ATTN_W
    for c0 in range(0, hg_w, COL_CHUNK):
        y = proj(base + c0)
        _store_heads(hq_ref, c0, y * _sigmoid(y))
    lbp = lbp_ref[...]
    e = jnp.exp(lbp - jnp.max(lbp, axis=0, keepdims=True))
    lb = jnp.sum(e[0:layer + 1], axis=0, keepdims=True) / jnp.sum(e, axis=0, keepdims=True)
    base += hg_w
    for c0 in range(0, hg_w, COL_CHUNK):
        lbc = lb[:, c0:c0 + COL_CHUNK]
        f = lbc + (1.0 - lbc) * _sigmoid(proj(base + c0))
        _store_heads(logf_ref, c0, jnp.log(f))
        _store_heads(hk_ref, c0, 1.0 - f)
    base += hg_w
    for c0 in range(0, hg_w, COL_CHUNK):
        _store_heads(hv_ref, c0, proj(base + c0))
    base += hg_w
    for c0 in range(0, hg_w, COL_CHUNK):
        y = proj(base + c0)
        _store_heads(og_ref, c0, y * _sigmoid(y))
    base += hg_w
    for c0 in range(0, 2 * d, COL_CHUNK):
        gates_ref[:, c0:c0 + COL_CHUNK] = _sigmoid(proj(base + c0)).astype(act)


def _in_projection(x3, mod3, w_in, lb_param, layer, gb, rb, act):
    g, r, d = x3.shape
    n = g * r
    tm = gb * rb
    hg_w = lb_param.shape[1]
    rt = r // rb
    heads = hg_w // HG_DK
    rows = lambda width: pl.BlockSpec((tm, width), lambda gi, ri: (gi * rt + ri, 0))
    by_head = pl.BlockSpec((heads, tm, HG_DK), lambda gi, ri: (0, gi * rt + ri, 0))
    head_major = lambda dtype: jax.ShapeDtypeStruct((heads, n, HG_DK), dtype)
    out_shapes = (
        jax.ShapeDtypeStruct((n, 3 * ATTN_W), act),
        head_major(act),
        head_major(act),
        head_major(act),
        head_major(F32),
        head_major(act),
        jax.ShapeDtypeStruct((n, 2 * d), act),
    )
    return pl.pallas_call(
        functools.partial(_inproj_kernel, layer=layer),
        out_shape=out_shapes,
        grid=(g // gb, rt),
        in_specs=[pl.BlockSpec((gb, rb, d), lambda gi, ri: (gi, ri, 0)),
                  pl.BlockSpec((gb, 1, mod3.shape[2]), lambda gi, ri: (gi, 0, 0)),
                  _resident(w_in.shape),
                  _resident(lb_param.shape)],
        out_specs=(rows(3 * ATTN_W), by_head, by_head, by_head, by_head, by_head, rows(2 * d)),
        compiler_params=_cparams("arbitrary", "arbitrary"),
        name="in_projection",
    )(x3, mod3, w_in, lb_param)


def _band_scores(q, k_own, k_prev, v_own, v_prev, has_prev, slope):
    qi = lax.broadcasted_iota(jnp.int32, (BAND, BAND), 0)
    kj = lax.broadcasted_iota(jnp.int32, (BAND, BAND), 1)
    steps_own = qi - kj
    steps_prev = steps_own + BAND
    scale = HEAD_DIM ** -0.5
    s_own = lax.dot_general(q, k_own, NT_DIMS, preferred_element_type=F32)
    s_prev = lax.dot_general(q, k_prev, NT_DIMS, preferred_element_type=F32)
    s_own = jnp.where(steps_own >= 0, s_own * scale - slope * steps_own.astype(F32), NEG)
    s_prev = jnp.where((steps_prev <= BAND) & has_prev, s_prev * scale - slope * steps_prev.astype(F32), NEG)
    m = jnp.max(jnp.maximum(s_own, s_prev), axis=-1, keepdims=True)
    p_own = jnp.exp(s_own - m).astype(BF16)
    p_prev = jnp.exp(s_prev - m).astype(BF16)
    ones = jnp.ones((BAND, HEAD_DIM), BF16)
    both = (jnp.dot(p_own, jnp.concatenate([v_own, ones], axis=1), preferred_element_type=F32)
            + jnp.dot(p_prev, jnp.concatenate([v_prev, ones], axis=1), preferred_element_type=F32))
    return m, both[:, HEAD_DIM:], both[:, :HEAD_DIM]


ATTN_BLOCKS_PER_ITER = 4


def _attn_prompt_kernel(*refs, head_axis):
    in_refs = refs[:3 * N_GROUPS]
    attn_ref = refs[3 * N_GROUPS]
    qf, kf, vf, mf, lf, af = refs[3 * N_GROUPS + 1:]
    s = attn_ref.shape[0]
    nblk = s // BAND
    head = pl.program_id(head_axis)
    for g, (window, dil) in enumerate(ATTN_GROUPS):
        q_ref, k_ref, v_ref = in_refs[3 * g:3 * g + 3]
        qf[...] = q_ref[...].astype(F32)
        kf[...] = k_ref[...].astype(F32)
        vf[...] = v_ref[...].astype(F32)
        nb = nblk // dil
        slope = jnp.float32(_alibi_slope(g, HEADS_PER_GROUP - 1) * dil)
        for h in range(HEADS_PER_GROUP - 1):
            slope = jnp.where(head == h, jnp.float32(_alibi_slope(g, h) * dil), slope)

        def body(it, carry, g=g, dil=dil, nb=nb, slope=slope):
            results = []
            for u in range(ATTN_BLOCKS_PER_ITER):
                idx = it * ATTN_BLOCKS_PER_ITER + u
                r = idx // nb
                i = idx - r * nb
                start = r + i * (BAND * dil)
                start_prev = r + jnp.maximum(i - 1, 0) * (BAND * dil)
                rows = pl.ds(start, BAND, stride=dil) if dil > 1 else pl.ds(start, BAND)
                rows_prev = pl.ds(start_prev, BAND, stride=dil) if dil > 1 else pl.ds(start_prev, BAND)
                m, l, acc = _band_scores(qf[rows, :].astype(BF16), kf[rows, :].astype(BF16),
                                         kf[rows_prev, :].astype(BF16), vf[rows, :].astype(BF16),
                                         vf[rows_prev, :].astype(BF16), i > 0, slope)
                if g > 0:
                    m_old = mf[rows, :]
                    m_new = jnp.maximum(m_old, m)
                    a_old = jnp.exp(m_old - m_new)
                    a_blk = jnp.exp(m - m_new)
                    l = a_old * lf[rows, :] + a_blk * l
                    acc = a_old * af[rows, :] + a_blk * acc
                    m = m_new
                results.append((rows, m, l, acc))
            for rows, m, l, acc in results:
                mf[rows, :] = jnp.broadcast_to(m, (BAND, HEAD_DIM))
                lf[rows, :] = jnp.broadcast_to(l, (BAND, HEAD_DIM))
                af[rows, :] = acc
            return carry

        lax.fori_loop(0, nblk // ATTN_BLOCKS_PER_ITER, body, 0)
    attn_ref[...] = (af[...] / lf[...]).astype(attn_ref.dtype)


def _attention_prompt(qkv, b, s):
    hpq = ATTN_W // HEAD_DIM
    in_specs = []
    for g in range(N_GROUPS):
        for section in range(3):
            base = section * hpq + g * HEADS_PER_GROUP
            in_specs.append(pl.BlockSpec((s, HEAD_DIM), lambda bi, h, base=base: (bi, base + h)))
    return pl.pallas_call(
        functools.partial(_attn_prompt_kernel, head_axis=1),
        out_shape=jax.ShapeDtypeStruct((b * s, GROUP_W), BF16),
        grid=(b, HEADS_PER_GROUP),
        in_specs=in_specs,
        out_specs=pl.BlockSpec((s, HEAD_DIM), lambda bi, h: (bi, h)),
        scratch_shapes=[pltpu.VMEM((s, HEAD_DIM), F32) for _ in range(6)],
        compiler_params=_cparams("arbitrary", "arbitrary"),
        name="attn_prompt",
    )(*([qkv] * (3 * N_GROUPS)))


def _attn_sample_kernel(*refs, group, dil, n_steps):
    first = group == 0
    last = group == N_GROUPS - 1
    q_ref, newkv_ref, cache_ref = refs[:3]
    pos = 3
    if not first:
        m_in, l_in, a_in = refs[pos:pos + 3]
        pos += 3
    newc_ref = refs[pos]
    outs = refs[pos + 1:]
    t = q_ref.shape[0]
    rp = 2 * HEADS_PER_GROUP
    length = cache_ref.shape[0] // rp
    newc_ref[0:(length - t) * rp, :] = cache_ref[t * rp:length * rp, :]
    newc_ref[(length - t) * rp:length * rp, :] = newkv_ref[...]
    head_rows = lambda ref, n, first_row: ref[pl.ds(first_row, n, stride=rp), :].astype(BF16)
    j_c = lax.broadcasted_iota(jnp.int32, (t, length), 0)
    r_c = lax.broadcasted_iota(jnp.int32, (t, length), 1)
    dist_c = length + j_c - r_c
    ok_c = ((dist_c & (dil - 1)) == 0) & (dist_c <= n_steps * dil)
    j_n = lax.broadcasted_iota(jnp.int32, (t, t), 0)
    r_n = lax.broadcasted_iota(jnp.int32, (t, t), 1)
    dist_n = j_n - r_n
    ok_n = (dist_n >= 0) & ((dist_n & (dil - 1)) == 0) & (dist_n <= n_steps * dil)
    scale = HEAD_DIM ** -0.5
    qoff = group * GROUP_W
    for h in range(HEADS_PER_GROUP):
        sl = slice(h * HEAD_DIM, (h + 1) * HEAD_DIM)
        slope = _alibi_slope(group, h)
        q = q_ref[:, qoff + h * HEAD_DIM:qoff + (h + 1) * HEAD_DIM].astype(BF16)
        s_c = lax.dot_general(q, head_rows(cache_ref, length, h), NT_DIMS, preferred_element_type=F32)
        s_n = lax.dot_general(q, head_rows(newkv_ref, t, h), NT_DIMS, preferred_element_type=F32)
        s_c = jnp.where(ok_c, s_c * scale - slope * dist_c.astype(F32), NEG)
        s_n = jnp.where(ok_n, s_n * scale - slope * dist_n.astype(F32), NEG)
        m = jnp.maximum(jnp.max(s_c, axis=-1, keepdims=True), jnp.max(s_n, axis=-1, keepdims=True))
        if not first:
            m_old = m_in[:, sl]
            m = jnp.maximum(m, m_old[:, 0:1])
        p_c = jnp.exp(s_c - m)
        p_n = jnp.exp(s_n - m)
        l = jnp.sum(p_c, axis=-1, keepdims=True) + jnp.sum(p_n, axis=-1, keepdims=True)
        acc = (jnp.dot(p_c.astype(BF16), head_rows(cache_ref, length, HEADS_PER_GROUP + h),
                       preferred_element_type=F32)
               + jnp.dot(p_n.astype(BF16), head_rows(newkv_ref, t, HEADS_PER_GROUP + h),
                         preferred_element_type=F32))
        if not first:
            a_old = jnp.exp(m_old - m)
            l = a_old * l_in[:, sl] + l
            acc = a_old * a_in[:, sl] + acc
        if last:
            outs[0][:, sl] = acc / l
        else:
            outs[0][:, sl] = jnp.broadcast_to(m, (t, HEAD_DIM))
            outs[1][:, sl] = jnp.broadcast_to(l, (t, HEAD_DIM))
            outs[2][:, sl] = acc


def _attention_sample(qkv, new_kv, cache, layer, running, t, group):
    window, dil = ATTN_GROUPS[group]
    depth, bd, length, _, heads, dh = cache.shape
    rp = 2 * heads
    first = group == 0
    last = group == N_GROUPS - 1
    rows = pl.BlockSpec((t, GROUP_W), lambda bi: (bi, 0))
    stat = jax.ShapeDtypeStruct((bd * t, GROUP_W), F32)
    in_specs = [pl.BlockSpec((t, 3 * ATTN_W), lambda bi: (bi, 0)),
                pl.BlockSpec((None, t * rp, dh), lambda bi: (bi, 0, 0)),
                pl.BlockSpec((None, None, length * rp, dh), lambda bi: (layer, bi, 0, 0))]
    args = [qkv, new_kv.reshape(bd, t * rp, dh), cache.reshape(depth, bd, length * rp, dh)]
    if not first:
        in_specs += [rows, rows, rows]
        args += list(running)
    n_out = 1 if last else 3
    res = pl.pallas_call(
        functools.partial(_attn_sample_kernel, group=group, dil=dil, n_steps=window // dil),
        out_shape=(jax.ShapeDtypeStruct((bd, length * rp, dh), cache.dtype),) + (stat,) * n_out,
        grid=(bd,),
        in_specs=in_specs,
        out_specs=(pl.BlockSpec((None, length * rp, dh), lambda bi: (bi, 0, 0)),) + (rows,) * n_out,
        compiler_params=_cparams("arbitrary"),
        name=f"attn_sample_g{group}",
    )(*args)
    return res[0].reshape(cache.shape[1:]), (res[1] if last else tuple(res[1:]))


def _cumsum_rows(x, row):
    d = 1
    while d < x.shape[0]:
        x = x + jnp.where(row >= d, pltpu.roll(x, d, 0), 0.0)
        d *= 2
    return x


def _hgrn_carry(q, k, v, g, st):
    c = q.shape[0]
    qg = (q * jnp.exp(g)).astype(BF16)
    o = lax.dot_general(qg, st.astype(BF16), NT_DIMS, preferred_element_type=F32)
    g_last = g[c - 1:c, :]
    kd = (k * jnp.exp(g_last - g)).astype(BF16)
    st_new = st * jnp.exp(g_last) + lax.dot_general(v.astype(BF16), kd, TN_DIMS, preferred_element_type=F32)
    return o, st_new


def _hgrn_chunk_pairwise(q, k, v, lf, st):
    row = lax.broadcasted_iota(jnp.int32, q.shape, 0)
    g = _cumsum_rows(lf, row)
    o = jnp.sum(q * k, axis=-1, keepdims=True) * v
    for d in range(1, q.shape[0]):
        ok = row >= d
        e = jnp.exp(jnp.where(ok, g - pltpu.roll(g, d, 0), 0.0))
        a = jnp.sum(jnp.where(ok, q * pltpu.roll(k, d, 0) * e, 0.0), axis=-1, keepdims=True)
        o = o + a * pltpu.roll(v, d, 0)
    o_carry, st_new = _hgrn_carry(q, k, v, g, st)
    return o + o_carry, st_new


def _pair_level(c):
    ri = lax.broadcasted_iota(jnp.int32, (c, c), 0)
    ci = lax.broadcasted_iota(jnp.int32, (c, c), 1)
    x = ri ^ ci
    bits = jnp.zeros((c, c), jnp.int32)
    p = 1
    while p < c:
        bits = bits + (x >= p).astype(jnp.int32)
        p *= 2
    return jnp.where(ri > ci, bits + 1, jnp.where(ri == ci, 1, 0))


def _hgrn_chunk_blocked(q, k, v, lf, st, g_ref, level):
    c = q.shape[0]
    row = lax.broadcasted_iota(jnp.int32, (c, HG_DK), 0)
    x = lf
    sub = row & (SUBLANES - 1)
    for d in (1, 2, 4):
        x = x + jnp.where(sub >= d, pltpu.roll(x, d, 0), 0.0)
    g_ref[...] = x
    pieces = [x[0:SUBLANES]]
    off = None
    for j in range(1, c // SUBLANES):
        tot = jnp.broadcast_to(g_ref[j * SUBLANES - 1:j * SUBLANES, :], (SUBLANES, HG_DK))
        off = tot if off is None else off + tot
        pieces.append(x[j * SUBLANES:(j + 1) * SUBLANES] + off)
    g = jnp.concatenate(pieces, axis=0)
    g_ref[...] = g

    def mid_rows(n):
        half = n // 2
        sub8 = lax.broadcasted_iota(jnp.int32, (SUBLANES, HG_DK), 0)
        bcast = lambda r: jnp.broadcast_to(g_ref[r:r + 1, :], (SUBLANES, HG_DK))
        out = []
        for j in range(c // SUBLANES):
            base = j * SUBLANES
            if n >= SUBLANES:
                out.append(bcast(base // n * n + half - 1))
            else:
                assert n == 4
                out.append(jnp.where(sub8 < 4, bcast(base + 1), bcast(base + 5)))
        return jnp.concatenate(out, axis=0)

    amat = jnp.where(level == 1, lax.dot_general(q.astype(BF16), k.astype(BF16), NT_DIMS,
                                                 preferred_element_type=F32), 0.0)
    n = 2
    while n <= c:
        second = (row & (n // 2)) != 0
        if n == 2:
            arg = jnp.where(second, lf, 0.0)
        else:
            gmid = mid_rows(n)
            arg = jnp.where(second, g - gmid, gmid - g)
        y = (jnp.where(second, q, k) * jnp.exp(arg)).astype(BF16)
        amat = jnp.where(level == n.bit_length(), lax.dot_general(y, y, NT_DIMS, preferred_element_type=F32),
                         amat)
        n *= 2
    o = jnp.dot(amat.astype(BF16), v.astype(BF16), preferred_element_type=F32)
    o_carry, st_new = _hgrn_carry(q, k, v, g, st)
    return o + o_carry, st_new


def _hgrn_output(o, og, nw):
    return o * lax.rsqrt(jnp.mean(o * o, axis=-1, keepdims=True) + RMS_EPS) * nw * og


HGRN_HEADS_PER_ITER = 2


def _hgrn_prompt_kernel(q_ref, k_ref, v_ref, lf_ref, og_ref, nw_ref, hg_ref, sfin_ref, st_ref, g_ref):
    ci = pl.program_id(1)
    heads, c, _ = q_ref.shape

    @pl.when(ci == 0)
    def _():
        st_ref[...] = jnp.zeros_like(st_ref)

    level = _pair_level(c)

    def head_group(i, carry):
        for u in range(HGRN_HEADS_PER_ITER):
            h = i * HGRN_HEADS_PER_ITER + u
            o, st_new = _hgrn_chunk_blocked(q_ref[h].astype(F32), k_ref[h].astype(F32), v_ref[h].astype(F32),
                                            lf_ref[h], st_ref[h], g_ref.at[u], level)
            st_ref[h] = st_new
            hg_ref[h] = _hgrn_output(o, og_ref[h].astype(F32), nw_ref[...]).astype(hg_ref.dtype)
        return carry

    lax.fori_loop(0, heads // HGRN_HEADS_PER_ITER, head_group, 0)

    @pl.when(ci == pl.num_programs(1) - 1)
    def _():
        for h in range(heads):
            sfin_ref[0, h] = st_ref[h].T


def _hgrn_prompt(hq, hk, hv, logf, og, norm_w, b, s, chunk):
    heads, n, _ = hq.shape
    nc = s // chunk
    blk = pl.BlockSpec((heads, chunk, HG_DK), lambda bi, ci: (0, bi * nc + ci, 0))
    return pl.pallas_call(
        _hgrn_prompt_kernel,
        out_shape=(jax.ShapeDtypeStruct((heads, n, HG_DV), BF16),
                   jax.ShapeDtypeStruct((b, heads, HG_DK, HG_DV), F32)),
        grid=(b, nc),
        in_specs=[blk, blk, blk, blk, blk, pl.BlockSpec((1, HG_DV), lambda bi, ci: (0, 0))],
        out_specs=(blk, pl.BlockSpec((1, heads, HG_DK, HG_DV), lambda bi, ci: (bi, 0, 0, 0))),
        scratch_shapes=[pltpu.VMEM((heads, HG_DV, HG_DK), F32),
                        pltpu.VMEM((HGRN_HEADS_PER_ITER, chunk, HG_DK), F32)],
        compiler_params=_cparams("arbitrary", "arbitrary"),
        name="hgrn_prompt",
    )(hq, hk, hv, logf, og, norm_w.reshape(1, HG_DV))


def _hgrn_sample_kernel(q_ref, k_ref, v_ref, lf_ref, og_ref, nw_ref, s_ref, hg_ref, snew_ref):
    heads = s_ref.shape[1]
    for h in range(heads):
        o, st_new = _hgrn_chunk_pairwise(q_ref[h], k_ref[h], v_ref[h], lf_ref[h], s_ref[0, h].T)
        snew_ref[0, h] = st_new.T
        hg_ref[h] = _hgrn_output(o, og_ref[h], nw_ref[...])


def _hgrn_sample(hq, hk, hv, logf, og, norm_w, state, t):
    bd, heads = state.shape[0], state.shape[1]
    blk = pl.BlockSpec((heads, t, HG_DK), lambda bi: (0, bi, 0))
    sblk = pl.BlockSpec((1, heads, HG_DK, HG_DV), lambda bi: (bi, 0, 0, 0))
    return pl.pallas_call(
        _hgrn_sample_kernel,
        out_shape=(jax.ShapeDtypeStruct((heads, bd * t, HG_DV), F32),
                   jax.ShapeDtypeStruct(state.shape, state.dtype)),
        grid=(bd,),
        in_specs=[blk, blk, blk, blk, blk, pl.BlockSpec((1, HG_DV), lambda bi: (0, 0)), sblk],
        out_specs=(blk, sblk),
        compiler_params=_cparams("arbitrary"),
        name="hgrn_sample",
    )(hq, hk, hv, logf, og, norm_w.reshape(1, HG_DV), state)


def _merge_kernel(x_ref, mod_ref, attn_ref, hg_ref, gates_ref, wa_ref, wb_ref, wo_ref, lng_ref, lnb_ref,
                  x1_ref, *, alpha):
    gb, rb, d = x_ref.shape
    tm = gb * rb
    hg = jnp.concatenate([hg_ref[h].astype(BF16) for h in range(hg_ref.shape[0])], axis=1)
    branch_a = jnp.dot(attn_ref[...].astype(BF16), wa_ref[...], preferred_element_type=F32)
    branch_b = jnp.dot(hg, wb_ref[...], preferred_element_type=F32)
    gate_a = gates_ref[:, 0:d].astype(F32)
    gate_b = gates_ref[:, d:2 * d].astype(F32)
    merged = gate_a * branch_a + gate_b * branch_b
    mix = jnp.dot(merged.astype(BF16), wo_ref[...], preferred_element_type=F32)
    g1 = mod_ref[:, :, 2 * d:3 * d]
    h = alpha * x_ref[...] + g1 * mix.reshape(gb, rb, d)
    x1_ref[...] = _layer_norm(h, lng_ref[...], lnb_ref[...])


def _merge(x3, mod3, attn, hg, gates, wa, wb, wo, ln_g, ln_b, gb, rb, alpha):
    g, r, d = x3.shape
    tm = gb * rb
    rt = r // rb
    row = lambda gi, ri: (gi * rt + ri, 0)
    rows = lambda a: pl.BlockSpec((tm, a.shape[1]), row)
    tile3 = pl.BlockSpec((gb, rb, d), lambda gi, ri: (gi, ri, 0))
    return pl.pallas_call(
        functools.partial(_merge_kernel, alpha=alpha),
        out_shape=jax.ShapeDtypeStruct(x3.shape, F32),
        grid=(g // gb, rt),
        in_specs=[tile3, pl.BlockSpec((gb, 1, mod3.shape[2]), lambda gi, ri: (gi, 0, 0)),
                  rows(attn), pl.BlockSpec((hg.shape[0], tm, hg.shape[2]), lambda gi, ri: (0, gi * rt + ri, 0)),
                  rows(gates),
                  _resident(wa.shape), _resident(wb.shape), _resident(wo.shape),
                  _resident((1, d)), _resident((1, d))],
        out_specs=tile3,
        compiler_params=_cparams("arbitrary", "arbitrary"),
        name="merge",
    )(x3, mod3, attn, hg, gates, wa, wb, wo, ln_g.reshape(1, d), ln_b.reshape(1, d))


def _mlp_kernel(x1_ref, mod_ref, wu_ref, bu_ref, wd_ref, bd_ref, lng_ref, lnb_ref, x2_ref, *, alpha):
    gb, rb, d = x1_ref.shape
    tm = gb * rb
    dff = wu_ref.shape[1]
    x1 = x1_ref[...]
    sh = mod_ref[:, :, 3 * d:4 * d]
    sc = mod_ref[:, :, 4 * d:5 * d]
    g2 = mod_ref[:, :, 5 * d:6 * d]
    u = (x1 * (1.0 + sc) + sh).reshape(tm, d).astype(BF16)
    acc = jnp.zeros((tm, d), F32)
    for c0 in range(0, dff, COL_CHUNK):
        hid = jnp.dot(u, wu_ref[:, c0:c0 + COL_CHUNK], preferred_element_type=F32) + bu_ref[:, c0:c0 + COL_CHUNK]
        hid = jnp.square(jnp.maximum(hid, 0.0))
        acc = acc + jnp.dot(hid.astype(BF16), wd_ref[c0:c0 + COL_CHUNK, :], preferred_element_type=F32)
    ff = acc + bd_ref[...]
    h = alpha * x1 + g2 * ff.reshape(gb, rb, d)
    x2_ref[...] = _layer_norm(h, lng_ref[...], lnb_ref[...])


def _mlp(x3, mod3, wu, bu, wd, bd, ln_g, ln_b, gb, rb, alpha):
    g, r, d = x3.shape
    dff = wu.shape[1]
    tile3 = pl.BlockSpec((gb, rb, d), lambda gi, ri: (gi, ri, 0))
    return pl.pallas_call(
        functools.partial(_mlp_kernel, alpha=alpha),
        out_shape=jax.ShapeDtypeStruct(x3.shape, F32),
        grid=(g // gb, r // rb),
        in_specs=[tile3, pl.BlockSpec((gb, 1, mod3.shape[2]), lambda gi, ri: (gi, 0, 0)),
                  _resident(wu.shape), _resident((1, dff)), _resident(wd.shape), _resident((1, d)),
                  _resident((1, d)), _resident((1, d))],
        out_specs=tile3,
        compiler_params=_cparams("arbitrary", "arbitrary"),
        name="mlp",
    )(x3, mod3, wu, bu.reshape(1, dff), wd, bd.reshape(1, d), ln_g.reshape(1, d), ln_b.reshape(1, d))


PROMPT_ROWS = 256
SAMPLE_SEQS = 32
HGRN_CHUNK = 128


def _kv_rows(qkv, b, s, group, rows):
    q3 = qkv.reshape(b, s, 3 * ATTN_W)
    k = q3[:, s - rows:, ATTN_W + group * GROUP_W:ATTN_W + (group + 1) * GROUP_W]
    v = q3[:, s - rows:, 2 * ATTN_W + group * GROUP_W:2 * ATTN_W + (group + 1) * GROUP_W]
    return jnp.stack([k, v], axis=2).astype(F32).reshape(b, rows, 2, HEADS_PER_GROUP, HEAD_DIM)


def _layer(x, mod, caches, states, layer, w, alpha):
    b, s, d = x.shape
    mod3 = mod.reshape(b, 1, mod.shape[1])
    prompt = caches is None
    if prompt:
        gb, rb, act = 1, min(PROMPT_ROWS, s), BF16
    else:
        gb, rb, act = min(SAMPLE_SEQS, b), s, F32
    qkv, hq, hk, hv, logf, og, gates = _in_projection(x, mod3, w["w_in"], w["lb_param"], layer, gb, rb, act)

    new_bufs = []
    if prompt:
        attn = _attention_prompt(qkv, b, s)
        for gi, (window, dil) in enumerate(ATTN_GROUPS):
            new_bufs.append(_kv_rows(qkv, b, s, gi, min(window, s)))
        hg, s_fin = _hgrn_prompt(hq, hk, hv, logf, og, w["hg_norm_w"], b, s, min(HGRN_CHUNK, s))
    else:
        running = None
        for gi in range(N_GROUPS):
            buf, running = _attention_sample(qkv, _kv_rows(qkv, b, s, gi, s), caches[gi], layer, running, s, gi)
            new_bufs.append(buf)
        attn = running
        hg, s_fin = _hgrn_sample(hq, hk, hv, logf, og, w["hg_norm_w"], states[layer], s)

    x1 = _merge(x, mod3, attn, hg, gates, w["w_branch_a"], w["w_branch_b"], w["w_out"],
                w["ln1_g"], w["ln1_b"], gb, rb, alpha)
    x2 = _mlp(x1, mod3, w["w_up"], w["b_up"], w["w_down"], w["b_down"], w["ln2_g"], w["ln2_b"], gb, rb, alpha)
    return x2, new_bufs, s_fin


def _stack_layers(per_layer):
    return per_layer[0][None] if len(per_layer) == 1 else jnp.stack(per_layer)


def kernel(x_prompt, x_sample, c_prompt, c_sample, cache_kv_w128, cache_kv_w512, cache_kv_w2048, state_hgrn,
           w_ada, b_ada, w_in, lb_param, hg_norm_w, w_branch_a, w_branch_b, w_out, ln1_g, ln1_b, w_up, b_up,
           w_down, b_down, ln2_g, ln2_b):
    depth = w_ada.shape[0]
    alpha = (2 * depth) ** 0.25
    caches = (cache_kv_w128, cache_kv_w512, cache_kv_w2048)
    for (window, dil), cache in zip(ATTN_GROUPS, caches):
        assert window // dil == BAND and cache.shape[2] == window
    assert x_prompt.shape[1] % (BAND * ATTN_GROUPS[-1][1]) == 0 and x_sample.shape[1] == SUBLANES
    nb = c_prompt.shape[0]
    yp, ys = x_prompt, x_sample
    p_bufs, p_states, s_bufs, s_states = [], [], [], []
    for l in range(depth):
        w = dict(w_in=w_in[l].astype(BF16), lb_param=lb_param, hg_norm_w=hg_norm_w[l],
                 w_branch_a=w_branch_a[l].astype(BF16), w_branch_b=w_branch_b[l].astype(BF16),
                 w_out=w_out[l].astype(BF16), ln1_g=ln1_g[l], ln1_b=ln1_b[l],
                 w_up=w_up[l].astype(BF16), b_up=b_up[l], w_down=w_down[l].astype(BF16), b_down=b_down[l],
                 ln2_g=ln2_g[l], ln2_b=ln2_b[l])
        mod = _modulation(jnp.concatenate([c_prompt, c_sample], axis=0), w_ada[l].astype(BF16), b_ada[l])
        yp, bufs_p, st_p = _layer(yp, mod[:nb], None, None, l, w, alpha)
        ys, bufs_s, st_s = _layer(ys, mod[nb:], caches, state_hgrn, l, w, alpha)
        p_bufs.append(bufs_p)
        p_states.append(st_p)
        s_bufs.append(bufs_s)
        s_states.append(st_s)
    group = lambda bufs, gi: _stack_layers([bl[gi] for bl in bufs])
    return (yp, ys, group(p_bufs, 0), group(p_bufs, 1), group(p_bufs, 2), _stack_layers(p_states),
            group(s_bufs, 0), group(s_bufs, 1), group(s_bufs, 2), _stack_layers(s_states))
```

```python
import functools
import math
from typing import NamedTuple

import jax
import jax.numpy as jnp
from jax import lax
from jax.experimental import pallas as pl
from jax.experimental.pallas import tpu as pltpu

F32 = jnp.float32
BF16 = jnp.bfloat16

ATTN_GROUPS = ((128, 1), (512, 4), (2048, 16))
N_GROUPS = len(ATTN_GROUPS)
HEADS_PER_GROUP = 4
HEAD_DIM = 128
N_ATTN_HEADS = N_GROUPS * HEADS_PER_GROUP
ATTN_W = N_ATTN_HEADS * HEAD_DIM
GROUP_W = HEADS_PER_GROUP * HEAD_DIM
BAND = 128
HG_DK = 128
HG_DV = 128
LN_EPS = 1e-5
RMS_EPS = 1e-6
NEG = -1e30

SUBLANES = 8
LANES = 128
VMEM_LIMIT_BYTES = 56 * 1024 * 1024
COL_CHUNK = 512

NT_DIMS = (((1,), (1,)), ((), ()))
TN_DIMS = (((0,), (0,)), ((), ()))


def _alibi_slope(group, head):
    return 2.0 ** (-8.0 * (group * HEADS_PER_GROUP + head + 1) / N_ATTN_HEADS)


def _cparams(*sem):
    return pltpu.CompilerParams(dimension_semantics=sem, vmem_limit_bytes=VMEM_LIMIT_BYTES)


def _resident(shape):
    nd = len(shape)
    return pl.BlockSpec(shape, lambda *_: (0,) * nd, pipeline_mode=pl.Buffered(1))


def _sigmoid(x):
    return 1.0 / (1.0 + jnp.exp(-x))


def _layer_norm(h, g, b):
    mu = jnp.mean(h, axis=-1, keepdims=True)
    hc = h - mu
    var = jnp.mean(hc * hc, axis=-1, keepdims=True)
    return hc * lax.rsqrt(var + LN_EPS) * g + b


class _Shift(NamedTuple):
    layer: int
    b0: int
    nbatch: int
    cb: int
    rsplit: int
    drop: int
    keep: int

    @property
    def n_chunks(self):
        return self.nbatch // self.cb * self.rsplit


def _plan_shift(layer, b0, nbatch, n_steps, drop, keep):
    cb = -(-nbatch // n_steps)
    while nbatch % cb:
        cb += 1
    rsplit = 1
    for r in (8, 4, 2):
        if nbatch // cb * r <= n_steps and keep % (r * SUBLANES) == 0:
            rsplit = r
            break
    return _Shift(layer, b0, nbatch, cb, rsplit, drop, keep)


def _shift_copy(shift, src_ref, dst_ref, sem, chunk):
    rows = shift.keep // shift.rsplit
    b = shift.b0 + (chunk // shift.rsplit) * shift.cb
    r = (chunk % shift.rsplit) * rows
    return pltpu.make_async_copy(src_ref.at[shift.layer, pl.ds(b, shift.cb), pl.ds(shift.drop + r, rows), :],
                                 dst_ref.at[pl.ds(b, shift.cb), pl.ds(r, rows), :], sem)


def _host_shift(kernel_fn, shift, n_in, n_out, aliased, grid):
    n_steps = math.prod(grid)
    assert shift.n_chunks <= n_steps

    def hosted(*refs):
        src = refs[n_in]
        pos = n_in + (2 if aliased else 1)
        outs = refs[pos:pos + n_out]
        dst = refs[pos + n_out]
        scratch = refs[pos + n_out + 1:-1]
        sem = refs[-1]
        step = 0
        for axis in range(len(grid)):
            step = step * grid[axis] + pl.program_id(axis)

        @pl.when((step > 0) & (step <= shift.n_chunks))
        def _():
            _shift_copy(shift, src, dst, sem, step - 1).wait()

        @pl.when(step < shift.n_chunks)
        def _():
            _shift_copy(shift, src, dst, sem, step).start()

        kernel_fn(*refs[:n_in], *outs, *scratch)

        if shift.n_chunks == n_steps:
            @pl.when(step == n_steps - 1)
            def _():
                _shift_copy(shift, src, dst, sem, step).wait()

    return hosted


def _hosting_call(kernel_fn, shift, cache, dst, *, out_shape, grid, in_specs, out_specs, args,
                  scratch_shapes=(), **kwargs):
    if shift is None:
        res = pl.pallas_call(kernel_fn, out_shape=tuple(out_shape), grid=grid, in_specs=list(in_specs),
                             out_specs=tuple(out_specs), scratch_shapes=list(scratch_shapes), **kwargs)(*args)
        return tuple(res), dst
    n_in, n_out = len(in_specs), len(out_shape)
    aliased = dst is not None
    anywhere = pl.BlockSpec(memory_space=pl.ANY)
    res = pl.pallas_call(
        _host_shift(kernel_fn, shift, n_in, n_out, aliased, grid),
        out_shape=tuple(out_shape) + (jax.ShapeDtypeStruct(cache.shape[1:], cache.dtype),),
        grid=grid,
        in_specs=list(in_specs) + [anywhere] * (2 if aliased else 1),
        out_specs=tuple(out_specs) + (anywhere,),
        scratch_shapes=list(scratch_shapes) + [pltpu.SemaphoreType.DMA(())],
        input_output_aliases={n_in + 1: n_out} if aliased else {},
        **kwargs,
    )(*args, cache, *([dst] if aliased else []))
    return tuple(res[:-1]), res[-1]


def _mod_kernel(c_ref, w_ref, b_ref, o_ref):
    c = c_ref[...]
    a = (c * _sigmoid(c)).astype(BF16)
    o_ref[...] = jnp.dot(a, w_ref[...], preferred_element_type=F32) + b_ref[...]


def _modulation(c, w_ada, b_ada):
    n, d = c.shape
    nout = w_ada.shape[1]
    tn = 1024
    return pl.pallas_call(
        _mod_kernel,
        out_shape=jax.ShapeDtypeStruct((n, nout), F32),
        grid=(nout // tn,),
        in_specs=[pl.BlockSpec((n, d), lambda j: (0, 0)),
                  pl.BlockSpec((d, tn), lambda j: (0, j)),
                  pl.BlockSpec((1, tn), lambda j: (0, j))],
        out_specs=pl.BlockSpec((n, tn), lambda j: (0, j)),
        compiler_params=_cparams("arbitrary"),
        name="modulation",
    )(c, w_ada, b_ada.reshape(1, nout))


def _store_heads(ref, c0, y):
    for j in range(y.shape[1] // HG_DK):
        ref[c0 // HG_DK + j] = y[:, j * HG_DK:(j + 1) * HG_DK].astype(ref.dtype)


def _inproj_kernel(x_ref, mod_ref, w_ref, lbp_ref, attn_ref, hq_ref, hk_ref, hv_ref, logf_ref, og_ref,
                   gates_ref, *, layer):
    gb, rb, d = x_ref.shape
    tm = gb * rb
    hg_w = hq_ref.shape[0] * HG_DK
    act = attn_ref.dtype
    sh = mod_ref[:, :, 0:d]
    sc = mod_ref[:, :, d:2 * d]
    u = (x_ref[...] * (1.0 + sc) + sh).reshape(tm, d).astype(BF16)

    def proj(c0):
        return jnp.dot(u, w_ref[:, c0:c0 + COL_CHUNK], preferred_element_type=F32)

    for c0 in range(0, 3 * ATTN_W, COL_CHUNK):
        attn_ref[:, c0:c0 + COL_CHUNK] = proj(c0).astype(act)
    base = 3 * ATTN_W
    for c0 in range(0, hg_w, COL_CHUNK):
        y = proj(base + c0)
        _store_heads(hq_ref, c0, y * _sigmoid(y))
    lbp = lbp_ref[...]
    e = jnp.exp(lbp - jnp.max(lbp, axis=0, keepdims=True))
    lb = jnp.sum(e[0:layer + 1], axis=0, keepdims=True) / jnp.sum(e, axis=0, keepdims=True)
    base += hg_w
    for c0 in range(0, hg_w, COL_CHUNK):
        lbc = lb[:, c0:c0 + COL_CHUNK]
        f = lbc + (1.0 - lbc) * _sigmoid(proj(base + c0))
        _store_heads(logf_ref, c0, jnp.log(f))
        _store_heads(hk_ref, c0, 1.0 - f)
    base += hg_w
    for c0 in range(0, hg_w, COL_CHUNK):
        _store_heads(hv_ref, c0, proj(base + c0))
    base += hg_w
    for c0 in range(0, hg_w, COL_CHUNK):
        y = proj(base + c0)
        _store_heads(og_ref, c0, y * _sigmoid(y))
    base += hg_w
    for c0 in range(0, 2 * d, COL_CHUNK):
        gates_ref[:, c0:c0 + COL_CHUNK] = _sigmoid(proj(base + c0)).astype(act)


def _in_projection(x3, mod3, w_in, lb_param, layer, gb, rb, act, hosted=(None, None, None)):
    g, r, d = x3.shape
    n = g * r
    tm = gb * rb
    hg_w = lb_param.shape[1]
    rt = r // rb
    heads = hg_w // HG_DK
    rows = lambda width: pl.BlockSpec((tm, width), lambda gi, ri: (gi * rt + ri, 0))
    by_head = pl.BlockSpec((heads, tm, HG_DK), lambda gi, ri: (0, gi * rt + ri, 0))
    head_major = lambda dtype: jax.ShapeDtypeStruct((heads, n, HG_DK), dtype)
    out_shapes = (
        jax.ShapeDtypeStruct((n, 3 * ATTN_W), act),
        head_major(act),
        head_major(act),
        head_major(act),
        head_major(F32),
        head_major(act),
        jax.ShapeDtypeStruct((n, 2 * d), act),
    )
    return _hosting_call(
        functools.partial(_inproj_kernel, layer=layer), *hosted,
        out_shape=out_shapes,
        grid=(g // gb, rt),
        in_specs=[pl.BlockSpec((gb, rb, d), lambda gi, ri: (gi, ri, 0)),
                  pl.BlockSpec((gb, 1, mod3.shape[2]), lambda gi, ri: (gi, 0, 0)),
                  _resident(w_in.shape),
                  _resident(lb_param.shape)],
        out_specs=(rows(3 * ATTN_W), by_head, by_head, by_head, by_head, by_head, rows(2 * d)),
        args=(x3, mod3, w_in, lb_param),
        compiler_params=_cparams("arbitrary", "arbitrary"),
        name="in_projection",
    )


def _band_scores(q, k_own, k_prev, v_own, v_prev, has_prev, slope):
    qi = lax.broadcasted_iota(jnp.int32, (BAND, BAND), 0)
    kj = lax.broadcasted_iota(jnp.int32, (BAND, BAND), 1)
    steps_own = qi - kj
    steps_prev = steps_own + BAND
    scale = HEAD_DIM ** -0.5
    s_own = lax.dot_general(q, k_own, NT_DIMS, preferred_element_type=F32)
    s_prev = lax.dot_general(q, k_prev, NT_DIMS, preferred_element_type=F32)
    s_own = jnp.where(steps_own >= 0, s_own * scale - slope * steps_own.astype(F32), NEG)
    s_prev = jnp.where((steps_prev <= BAND) & has_prev, s_prev * scale - slope * steps_prev.astype(F32), NEG)
    m = jnp.max(jnp.maximum(s_own, s_prev), axis=-1, keepdims=True)
    p_own = jnp.exp(s_own - m).astype(BF16)
    p_prev = jnp.exp(s_prev - m).astype(BF16)
    ones = jnp.ones((BAND, HEAD_DIM), BF16)
    both = (jnp.dot(p_own, jnp.concatenate([v_own, ones], axis=1), preferred_element_type=F32)
            + jnp.dot(p_prev, jnp.concatenate([v_prev, ones], axis=1), preferred_element_type=F32))
    return m, both[:, HEAD_DIM:], both[:, :HEAD_DIM]


ATTN_BLOCKS_PER_ITER = 4


def _attn_prompt_kernel(*refs, head_axis):
    in_refs = refs[:3 * N_GROUPS]
    attn_ref = refs[3 * N_GROUPS]
    qf, kf, vf, mf, lf, af = refs[3 * N_GROUPS + 1:]
    s = attn_ref.shape[0]
    nblk = s // BAND
    head = pl.program_id(head_axis)
    for g, (window, dil) in enumerate(ATTN_GROUPS):
        q_ref, k_ref, v_ref = in_refs[3 * g:3 * g + 3]
        qf[...] = q_ref[...].astype(F32)
        kf[...] = k_ref[...].astype(F32)
        vf[...] = v_ref[...].astype(F32)
        nb = nblk // dil
        slope = jnp.float32(_alibi_slope(g, HEADS_PER_GROUP - 1) * dil)
        for h in range(HEADS_PER_GROUP - 1):
            slope = jnp.where(head == h, jnp.float32(_alibi_slope(g, h) * dil), slope)

        def body(it, carry, g=g, dil=dil, nb=nb, slope=slope):
            results = []
            for u in range(ATTN_BLOCKS_PER_ITER):
                idx = it * ATTN_BLOCKS_PER_ITER + u
                r = idx // nb
                i = idx - r * nb
                start = r + i * (BAND * dil)
                start_prev = r + jnp.maximum(i - 1, 0) * (BAND * dil)
                rows = pl.ds(start, BAND, stride=dil) if dil > 1 else pl.ds(start, BAND)
                rows_prev = pl.ds(start_prev, BAND, stride=dil) if dil > 1 else pl.ds(start_prev, BAND)
                m, l, acc = _band_scores(qf[rows, :].astype(BF16), kf[rows, :].astype(BF16),
                                         kf[rows_prev, :].astype(BF16), vf[rows, :].astype(BF16),
                                         vf[rows_prev, :].astype(BF16), i > 0, slope)
                if g > 0:
                    m_old = mf[rows, :]
                    m_new = jnp.maximum(m_old, m)
                    a_old = jnp.exp(m_old - m_new)
                    a_blk = jnp.exp(m - m_new)
                    l = a_old * lf[rows, :] + a_blk * l
                    acc = a_old * af[rows, :] + a_blk * acc
                    m = m_new
                results.append((rows, m, l, acc))
            for rows, m, l, acc in results:
                mf[rows, :] = jnp.broadcast_to(m, (BAND, HEAD_DIM))
                lf[rows, :] = jnp.broadcast_to(l, (BAND, HEAD_DIM))
                af[rows, :] = acc
            return carry

        lax.fori_loop(0, nblk // ATTN_BLOCKS_PER_ITER, body, 0)
    attn_ref[...] = (af[...] / lf[...]).astype(attn_ref.dtype)


def _attention_prompt(qkv, b, s, hosted=(None, None, None)):
    hpq = ATTN_W // HEAD_DIM
    in_specs = []
    for g in range(N_GROUPS):
        for section in range(3):
            base = section * hpq + g * HEADS_PER_GROUP
            in_specs.append(pl.BlockSpec((s, HEAD_DIM), lambda bi, h, base=base: (bi, base + h)))
    (attn,), dst = _hosting_call(
        functools.partial(_attn_prompt_kernel, head_axis=1), *hosted,
        out_shape=(jax.ShapeDtypeStruct((b * s, GROUP_W), BF16),),
        grid=(b, HEADS_PER_GROUP),
        in_specs=in_specs,
        out_specs=(pl.BlockSpec((s, HEAD_DIM), lambda bi, h: (bi, h)),),
        scratch_shapes=[pltpu.VMEM((s, HEAD_DIM), F32) for _ in range(6)],
        args=[qkv] * (3 * N_GROUPS),
        compiler_params=_cparams("arbitrary", "arbitrary"),
        name="attn_prompt",
    )
    return attn, dst


def _attn_sample_kernel(*refs, group, dil, n_steps, length, seqs):
    first = group == 0
    last = group == N_GROUPS - 1
    q_ref, newkv_ref, cache_ref = refs[:3]
    pos = 3
    if not first:
        m_in, l_in, a_in = refs[pos:pos + 3]
        pos += 3
    pos += 1
    newc_ref = refs[pos]
    outs = refs[pos + 1:]
    t = q_ref.shape[0] // seqs
    rp = 2 * HEADS_PER_GROUP
    stretch = max(dil, t)
    nkeys = length // stretch * t
    j_c = lax.broadcasted_iota(jnp.int32, (t, nkeys), 0)
    c_c = lax.broadcasted_iota(jnp.int32, (t, nkeys), 1)
    dist_c = length + j_c - ((c_c // t) * stretch + (c_c & (t - 1)))
    ok_c = ((dist_c & (dil - 1)) == 0) & (dist_c <= n_steps * dil)
    j_n = lax.broadcasted_iota(jnp.int32, (t, t), 0)
    r_n = lax.broadcasted_iota(jnp.int32, (t, t), 1)
    dist_n = j_n - r_n
    ok_n = (dist_n >= 0) & ((dist_n & (dil - 1)) == 0) & (dist_n <= n_steps * dil)
    scale = HEAD_DIM ** -0.5
    qoff = group * GROUP_W
    for s in range(seqs):
        newc_ref[s] = newkv_ref[s]
        rows = slice(s * t, (s + 1) * t)
        cache_rows = lambda first_row: cache_ref[s, :, pl.ds(first_row, t, stride=rp), :].reshape(
            nkeys, HEAD_DIM).astype(BF16)
        new_rows = lambda first_row: newkv_ref[s, pl.ds(first_row, t, stride=rp), :].astype(BF16)
        for h in range(HEADS_PER_GROUP):
            sl = slice(h * HEAD_DIM, (h + 1) * HEAD_DIM)
            slope = _alibi_slope(group, h)
            q = q_ref[rows, qoff + h * HEAD_DIM:qoff + (h + 1) * HEAD_DIM].astype(BF16)
            s_c = lax.dot_general(q, cache_rows(h), NT_DIMS, preferred_element_type=F32)
            s_n = lax.dot_general(q, new_rows(h), NT_DIMS, preferred_element_type=F32)
            s_c = jnp.where(ok_c, s_c * scale - slope * dist_c.astype(F32), NEG)
            s_n = jnp.where(ok_n, s_n * scale - slope * dist_n.astype(F32), NEG)
            m = jnp.maximum(jnp.max(s_c, axis=-1, keepdims=True), jnp.max(s_n, axis=-1, keepdims=True))
            if not first:
                m_old = m_in[rows, sl]
                m = jnp.maximum(m, m_old[:, 0:1])
            p_c = jnp.exp(s_c - m)
            p_n = jnp.exp(s_n - m)
            l = jnp.sum(p_c, axis=-1, keepdims=True) + jnp.sum(p_n, axis=-1, keepdims=True)
            acc = (jnp.dot(p_c.astype(BF16), cache_rows(HEADS_PER_GROUP + h), preferred_element_type=F32)
                   + jnp.dot(p_n.astype(BF16), new_rows(HEADS_PER_GROUP + h), preferred_element_type=F32))
            if not first:
                a_old = jnp.exp(m_old - m)
                l = a_old * l_in[rows, sl] + l
                acc = a_old * a_in[rows, sl] + acc
            if last:
                outs[0][rows, sl] = acc / l
            else:
                outs[0][rows, sl] = jnp.broadcast_to(m, (t, HEAD_DIM))
                outs[1][rows, sl] = jnp.broadcast_to(l, (t, HEAD_DIM))
                outs[2][rows, sl] = acc


SAMPLE_SEQS_PER_STEP = (4, 2, 1)


def _attention_sample(qkv, new_kv, cache, shifted, layer, running, t, group):
    window, dil = ATTN_GROUPS[group]
    depth, bd, crow, dh = cache.shape
    rp = 2 * HEADS_PER_GROUP
    length = crow // rp
    stretch = max(dil, t)
    seqs = math.gcd(SAMPLE_SEQS_PER_STEP[group], bd)
    first = group == 0
    last = group == N_GROUPS - 1
    rows = pl.BlockSpec((seqs * t, GROUP_W), lambda bi: (bi, 0))
    stat = jax.ShapeDtypeStruct((bd * t, GROUP_W), F32)
    new_blk = pl.BlockSpec((seqs, t * rp, dh), lambda bi: (bi, 0, 0))
    in_specs = [pl.BlockSpec((seqs * t, 3 * ATTN_W), lambda bi: (bi, 0)),
                new_blk,
                pl.BlockSpec((None, seqs, length // stretch, t * rp, dh), lambda bi: (layer, bi, 0, 0, 0))]
    args = [qkv, new_kv, cache.reshape(depth, bd, length // stretch, stretch * rp, dh)]
    if not first:
        in_specs += [rows, rows, rows]
        args += list(running)
    in_specs.append(pl.BlockSpec(memory_space=pl.ANY))
    args.append(shifted)
    n_out = 1 if last else 3
    res = pl.pallas_call(
        functools.partial(_attn_sample_kernel, group=group, dil=dil, n_steps=window // dil, length=length,
                          seqs=seqs),
        out_shape=(jax.ShapeDtypeStruct(shifted.shape, shifted.dtype),) + (stat,) * n_out,
        grid=(bd // seqs,),
        in_specs=in_specs,
        out_specs=(pl.BlockSpec((seqs, t * rp, dh), lambda bi: (bi, length // t - 1, 0)),) + (rows,) * n_out,
        input_output_aliases={len(args) - 1: 0},
        compiler_params=_cparams("arbitrary"),
        name=f"attn_sample_g{group}",
    )(*args)
    return res[0], (res[1] if last else tuple(res[1:]))


def _cumsum_rows(x, row):
    d = 1
    while d < x.shape[0]:
        x = x + jnp.where(row >= d, pltpu.roll(x, d, 0), 0.0)
        d *= 2
    return x


def _hgrn_carry(q, k, v, g, st):
    c = q.shape[0]
    qg = (q * jnp.exp(g)).astype(BF16)
    o = lax.dot_general(qg, st.astype(BF16), NT_DIMS, preferred_element_type=F32)
    g_last = g[c - 1:c, :]
    kd = (k * jnp.exp(g_last - g)).astype(BF16)
    st_new = st * jnp.exp(g_last) + lax.dot_general(v.astype(BF16), kd, TN_DIMS, preferred_element_type=F32)
    return o, st_new


def _hgrn_chunk_pairwise(q, k, v, lf, st):
    row = lax.broadcasted_iota(jnp.int32, q.shape, 0)
    g = _cumsum_rows(lf, row)
    o = jnp.sum(q * k, axis=-1, keepdims=True) * v
    for d in range(1, q.shape[0]):
        ok = row >= d
        e = jnp.exp(jnp.where(ok, g - pltpu.roll(g, d, 0), 0.0))
        a = jnp.sum(jnp.where(ok, q * pltpu.roll(k, d, 0) * e, 0.0), axis=-1, keepdims=True)
        o = o + a * pltpu.roll(v, d, 0)
    o_carry, st_new = _hgrn_carry(q, k, v, g, st)
    return o + o_carry, st_new


def _pair_level(c):
    ri = lax.broadcasted_iota(jnp.int32, (c, c), 0)
    ci = lax.broadcasted_iota(jnp.int32, (c, c), 1)
    x = ri ^ ci
    bits = jnp.zeros((c, c), jnp.int32)
    p = 1
    while p < c:
        bits = bits + (x >= p).astype(jnp.int32)
        p *= 2
    return jnp.where(ri > ci, bits + 1, jnp.where(ri == ci, 1, 0))


def _hgrn_chunk_blocked(q, k, v, lf, st, g_ref, level):
    c = q.shape[0]
    row = lax.broadcasted_iota(jnp.int32, (c, HG_DK), 0)
    x = lf
    sub = row & (SUBLANES - 1)
    for d in (1, 2, 4):
        x = x + jnp.where(sub >= d, pltpu.roll(x, d, 0), 0.0)
    g_ref[...] = x
    pieces = [x[0:SUBLANES]]
    off = None
    for j in range(1, c // SUBLANES):
        tot = jnp.broadcast_to(g_ref[j * SUBLANES - 1:j * SUBLANES, :], (SUBLANES, HG_DK))
        off = tot if off is None else off + tot
        pieces.append(x[j * SUBLANES:(j + 1) * SUBLANES] + off)
    g = jnp.concatenate(pieces, axis=0)
    g_ref[...] = g

    def mid_rows(n):
        half = n // 2
        sub8 = lax.broadcasted_iota(jnp.int32, (SUBLANES, HG_DK), 0)
        bcast = lambda r: jnp.broadcast_to(g_ref[r:r + 1, :], (SUBLANES, HG_DK))
        out = []
        for j in range(c // SUBLANES):
            base = j * SUBLANES
            if n >= SUBLANES:
                out.append(bcast(base // n * n + half - 1))
            else:
                assert n == 4
                out.append(jnp.where(sub8 < 4, bcast(base + 1), bcast(base + 5)))
        return jnp.concatenate(out, axis=0)

    amat = jnp.where(level == 1, lax.dot_general(q.astype(BF16), k.astype(BF16), NT_DIMS,
                                                 preferred_element_type=F32), 0.0)
    n = 2
    while n <= c:
        second = (row & (n // 2)) != 0
        if n == 2:
            arg = jnp.where(second, lf, 0.0)
        else:
            gmid = mid_rows(n)
            arg = jnp.where(second, g - gmid, gmid - g)
        y = (jnp.where(second, q, k) * jnp.exp(arg)).astype(BF16)
        amat = jnp.where(level == n.bit_length(), lax.dot_general(y, y, NT_DIMS, preferred_element_type=F32),
                         amat)
        n *= 2
    o = jnp.dot(amat.astype(BF16), v.astype(BF16), preferred_element_type=F32)
    o_carry, st_new = _hgrn_carry(q, k, v, g, st)
    return o + o_carry, st_new


def _hgrn_output(o, og, nw):
    return o * lax.rsqrt(jnp.mean(o * o, axis=-1, keepdims=True) + RMS_EPS) * nw * og


HGRN_HEADS_PER_ITER = 2


def _hgrn_prompt_kernel(q_ref, k_ref, v_ref, lf_ref, og_ref, nw_ref, hg_ref, sfin_ref, st_ref, g_ref):
    ci = pl.program_id(1)
    heads, c, _ = q_ref.shape

    @pl.when(ci == 0)
    def _():
        st_ref[...] = jnp.zeros_like(st_ref)

    level = _pair_level(c)

    def head_group(i, carry):
        for u in range(HGRN_HEADS_PER_ITER):
            h = i * HGRN_HEADS_PER_ITER + u
            o, st_new = _hgrn_chunk_blocked(q_ref[h].astype(F32), k_ref[h].astype(F32), v_ref[h].astype(F32),
                                            lf_ref[h], st_ref[h], g_ref.at[u], level)
            st_ref[h] = st_new
            hg_ref[h] = _hgrn_output(o, og_ref[h].astype(F32), nw_ref[...]).astype(hg_ref.dtype)
        return carry

    lax.fori_loop(0, heads // HGRN_HEADS_PER_ITER, head_group, 0)

    @pl.when(ci == pl.num_programs(1) - 1)
    def _():
        for h in range(heads):
            sfin_ref[0, h] = st_ref[h].T


def _hgrn_prompt(hq, hk, hv, logf, og, norm_w, b, s, chunk, hosted=(None, None, None)):
    heads, n, _ = hq.shape
    nc = s // chunk
    blk = pl.BlockSpec((heads, chunk, HG_DK), lambda bi, ci: (0, bi * nc + ci, 0))
    (hg, s_fin), dst = _hosting_call(
        _hgrn_prompt_kernel, *hosted,
        out_shape=(jax.ShapeDtypeStruct((heads, n, HG_DV), BF16),
                   jax.ShapeDtypeStruct((b, heads, HG_DK, HG_DV), F32)),
        grid=(b, nc),
        in_specs=[blk, blk, blk, blk, blk, pl.BlockSpec((1, HG_DV), lambda bi, ci: (0, 0))],
        out_specs=(blk, pl.BlockSpec((1, heads, HG_DK, HG_DV), lambda bi, ci: (bi, 0, 0, 0))),
        scratch_shapes=[pltpu.VMEM((heads, HG_DV, HG_DK), F32),
                        pltpu.VMEM((HGRN_HEADS_PER_ITER, chunk, HG_DK), F32)],
        args=(hq, hk, hv, logf, og, norm_w.reshape(1, HG_DV)),
        compiler_params=_cparams("arbitrary", "arbitrary"),
        name="hgrn_prompt",
    )
    return hg, s_fin, dst


def _hgrn_sample_kernel(q_ref, k_ref, v_ref, lf_ref, og_ref, nw_ref, s_ref, hg_ref, snew_ref):
    heads = s_ref.shape[1]
    for h in range(heads):
        o, st_new = _hgrn_chunk_pairwise(q_ref[h], k_ref[h], v_ref[h], lf_ref[h], s_ref[0, h].T)
        snew_ref[0, h] = st_new.T
        hg_ref[h] = _hgrn_output(o, og_ref[h], nw_ref[...])


def _hgrn_sample(hq, hk, hv, logf, og, norm_w, state, t):
    bd, heads = state.shape[0], state.shape[1]
    blk = pl.BlockSpec((heads, t, HG_DK), lambda bi: (0, bi, 0))
    sblk = pl.BlockSpec((1, heads, HG_DK, HG_DV), lambda bi: (bi, 0, 0, 0))
    return pl.pallas_call(
        _hgrn_sample_kernel,
        out_shape=(jax.ShapeDtypeStruct((heads, bd * t, HG_DV), F32),
                   jax.ShapeDtypeStruct(state.shape, state.dtype)),
        grid=(bd,),
        in_specs=[blk, blk, blk, blk, blk, pl.BlockSpec((1, HG_DV), lambda bi: (0, 0)), sblk],
        out_specs=(blk, sblk),
        compiler_params=_cparams("arbitrary"),
        name="hgrn_sample",
    )(hq, hk, hv, logf, og, norm_w.reshape(1, HG_DV), state)


def _merge_kernel(x_ref, mod_ref, attn_ref, hg_ref, gates_ref, wa_ref, wb_ref, wo_ref, lng_ref, lnb_ref,
                  x1_ref, *, alpha):
    gb, rb, d = x_ref.shape
    tm = gb * rb
    hg = jnp.concatenate([hg_ref[h].astype(BF16) for h in range(hg_ref.shape[0])], axis=1)
    branch_a = jnp.dot(attn_ref[...].astype(BF16), wa_ref[...], preferred_element_type=F32)
    branch_b = jnp.dot(hg, wb_ref[...], preferred_element_type=F32)
    gate_a = gates_ref[:, 0:d].astype(F32)
    gate_b = gates_ref[:, d:2 * d].astype(F32)
    merged = gate_a * branch_a + gate_b * branch_b
    mix = jnp.dot(merged.astype(BF16), wo_ref[...], preferred_element_type=F32)
    g1 = mod_ref[:, :, 2 * d:3 * d]
    h = alpha * x_ref[...] + g1 * mix.reshape(gb, rb, d)
    x1_ref[...] = _layer_norm(h, lng_ref[...], lnb_ref[...])


def _merge(x3, mod3, attn, hg, gates, wa, wb, wo, ln_g, ln_b, gb, rb, alpha):
    g, r, d = x3.shape
    tm = gb * rb
    rt = r // rb
    row = lambda gi, ri: (gi * rt + ri, 0)
    rows = lambda a: pl.BlockSpec((tm, a.shape[1]), row)
    tile3 = pl.BlockSpec((gb, rb, d), lambda gi, ri: (gi, ri, 0))
    return pl.pallas_call(
        functools.partial(_merge_kernel, alpha=alpha),
        out_shape=jax.ShapeDtypeStruct(x3.shape, F32),
        grid=(g // gb, rt),
        in_specs=[tile3, pl.BlockSpec((gb, 1, mod3.shape[2]), lambda gi, ri: (gi, 0, 0)),
                  rows(attn), pl.BlockSpec((hg.shape[0], tm, hg.shape[2]), lambda gi, ri: (0, gi * rt + ri, 0)),
                  rows(gates),
                  _resident(wa.shape), _resident(wb.shape), _resident(wo.shape),
                  _resident((1, d)), _resident((1, d))],
        out_specs=tile3,
        compiler_params=_cparams("arbitrary", "arbitrary"),
        name="merge",
    )(x3, mod3, attn, hg, gates, wa, wb, wo, ln_g.reshape(1, d), ln_b.reshape(1, d))


def _mlp_kernel(x1_ref, mod_ref, wu_ref, bu_ref, wd_ref, bd_ref, lng_ref, lnb_ref, x2_ref, *, alpha):
    gb, rb, d = x1_ref.shape
    tm = gb * rb
    dff = wu_ref.shape[1]
    x1 = x1_ref[...]
    sh = mod_ref[:, :, 3 * d:4 * d]
    sc = mod_ref[:, :, 4 * d:5 * d]
    g2 = mod_ref[:, :, 5 * d:6 * d]
    u = (x1 * (1.0 + sc) + sh).reshape(tm, d).astype(BF16)
    acc = jnp.zeros((tm, d), F32)
    for c0 in range(0, dff, COL_CHUNK):
        hid = jnp.dot(u, wu_ref[:, c0:c0 + COL_CHUNK], preferred_element_type=F32) + bu_ref[:, c0:c0 + COL_CHUNK]
        hid = jnp.square(jnp.maximum(hid, 0.0))
        acc = acc + jnp.dot(hid.astype(BF16), wd_ref[c0:c0 + COL_CHUNK, :], preferred_element_type=F32)
    ff = acc + bd_ref[...]
    h = alpha * x1 + g2 * ff.reshape(gb, rb, d)
    x2_ref[...] = _layer_norm(h, lng_ref[...], lnb_ref[...])


def _mlp(x3, mod3, wu, bu, wd, bd, ln_g, ln_b, gb, rb, alpha, hosted=(None, None, None)):
    g, r, d = x3.shape
    dff = wu.shape[1]
    tile3 = pl.BlockSpec((gb, rb, d), lambda gi, ri: (gi, ri, 0))
    (x2,), dst = _hosting_call(
        functools.partial(_mlp_kernel, alpha=alpha), *hosted,
        out_shape=(jax.ShapeDtypeStruct(x3.shape, F32),),
        grid=(g // gb, r // rb),
        in_specs=[tile3, pl.BlockSpec((gb, 1, mod3.shape[2]), lambda gi, ri: (gi, 0, 0)),
                  _resident(wu.shape), _resident((1, dff)), _resident(wd.shape), _resident((1, d)),
                  _resident((1, d)), _resident((1, d))],
        out_specs=(tile3,),
        args=(x3, mod3, wu, bu.reshape(1, dff), wd, bd.reshape(1, d), ln_g.reshape(1, d), ln_b.reshape(1, d)),
        compiler_params=_cparams("arbitrary", "arbitrary"),
        name="mlp",
    )
    return x2, dst


PROMPT_ROWS = 256
SAMPLE_SEQS = 32
HGRN_CHUNK = 128


def _kv_rows(qkv, b, s, group, rows):
    q3 = qkv.reshape(b, s, 3 * ATTN_W)
    k = q3[:, s - rows:, ATTN_W + group * GROUP_W:ATTN_W + (group + 1) * GROUP_W]
    v = q3[:, s - rows:, 2 * ATTN_W + group * GROUP_W:2 * ATTN_W + (group + 1) * GROUP_W]
    return jnp.stack([k, v], axis=2).astype(F32).reshape(b, rows, 2, HEADS_PER_GROUP, HEAD_DIM)


def _prompt_layer(x, mod, caches, t_new, layer, w, alpha):
    b, s, d = x.shape
    mod3 = mod.reshape(b, 1, mod.shape[1])
    gb, rb = 1, min(PROMPT_ROWS, s)
    chunk = min(HGRN_CHUNK, s)
    rp = 2 * HEADS_PER_GROUP
    bd = caches[0].shape[1]
    steps = dict(proj=s // rb * b, hgrn=s // chunk * b, attn=b * HEADS_PER_GROUP, mlp=s // rb * b)

    def plan(gi, b0, nbatch, n_steps):
        rows = caches[gi].shape[2]
        return _plan_shift(layer, b0, nbatch, n_steps, t_new * rp, rows - t_new * rp)

    half = bd // 2 if bd % 2 == 0 else 0
    (qkv, hq, hk, hv, logf, og, gates), small = _in_projection(
        x, mod3, w["w_in"], w["lb_param"], layer, gb, rb, BF16,
        hosted=(plan(0, 0, bd, steps["proj"]), caches[0], None))
    big = None
    hosted = (plan(2, 0, half, steps["hgrn"]), caches[2], None) if half else (None, None, None)
    hg, s_fin, big = _hgrn_prompt(hq, hk, hv, logf, og, w["hg_norm_w"], b, s, chunk, hosted=hosted)
    attn, big = _attention_prompt(qkv, b, s, hosted=(plan(2, half, bd - half, steps["attn"]), caches[2], big))
    new_bufs = [_kv_rows(qkv, b, s, gi, min(window, s)) for gi, (window, dil) in enumerate(ATTN_GROUPS)]
    x1 = _merge(x, mod3, attn, hg, gates, w["w_branch_a"], w["w_branch_b"], w["w_out"],
                w["ln1_g"], w["ln1_b"], gb, rb, alpha)
    x2, mid = _mlp(x1, mod3, w["w_up"], w["b_up"], w["w_down"], w["b_down"], w["ln2_g"], w["ln2_b"], gb, rb,
                   alpha, hosted=(plan(1, 0, bd, steps["mlp"]), caches[1], None))
    return x2, new_bufs, s_fin, (small, mid, big)


def _sample_layer(x, mod, caches, shifted, states, layer, w, alpha):
    b, s, d = x.shape
    mod3 = mod.reshape(b, 1, mod.shape[1])
    gb, rb = min(SAMPLE_SEQS, b), s
    rp = 2 * HEADS_PER_GROUP
    (qkv, hq, hk, hv, logf, og, gates), _ = _in_projection(x, mod3, w["w_in"], w["lb_param"], layer, gb, rb, F32)
    new_bufs = []
    running = None
    for gi in range(N_GROUPS):
        new_kv = _kv_rows(qkv, b, s, gi, s).reshape(b, s * rp, HEAD_DIM)
        buf, running = _attention_sample(qkv, new_kv, caches[gi], shifted[gi], layer, running, s, gi)
        new_bufs.append(buf)
    hg, s_fin = _hgrn_sample(hq, hk, hv, logf, og, w["hg_norm_w"], states[layer], s)
    x1 = _merge(x, mod3, running, hg, gates, w["w_branch_a"], w["w_branch_b"], w["w_out"],
                w["ln1_g"], w["ln1_b"], gb, rb, alpha)
    x2, _ = _mlp(x1, mod3, w["w_up"], w["b_up"], w["w_down"], w["b_down"], w["ln2_g"], w["ln2_b"], gb, rb, alpha)
    return x2, new_bufs, s_fin


def _stack_layers(per_layer):
    return per_layer[0][None] if len(per_layer) == 1 else jnp.stack(per_layer)


def kernel(x_prompt, x_sample, c_prompt, c_sample, cache_kv_w128, cache_kv_w512, cache_kv_w2048, state_hgrn,
           w_ada, b_ada, w_in, lb_param, hg_norm_w, w_branch_a, w_branch_b, w_out, ln1_g, ln1_b, w_up, b_up,
           w_down, b_down, ln2_g, ln2_b):
    depth = w_ada.shape[0]
    alpha = (2 * depth) ** 0.25
    caches = (cache_kv_w128, cache_kv_w512, cache_kv_w2048)
    for (window, dil), cache in zip(ATTN_GROUPS, caches):
        assert window // dil == BAND and cache.shape[2] == window
    assert x_prompt.shape[1] % (BAND * ATTN_GROUPS[-1][1]) == 0 and x_sample.shape[1] == SUBLANES
    views = tuple(c.reshape(c.shape[0], c.shape[1], c.shape[2] * c.shape[3] * c.shape[4], c.shape[5])
                  for c in caches)
    nb = c_prompt.shape[0]
    yp, ys = x_prompt, x_sample
    p_bufs, p_states, s_bufs, s_states = [], [], [], []
    for l in range(depth):
        w = dict(w_in=w_in[l].astype(BF16), lb_param=lb_param, hg_norm_w=hg_norm_w[l],
                 w_branch_a=w_branch_a[l].astype(BF16), w_branch_b=w_branch_b[l].astype(BF16),
                 w_out=w_out[l].astype(BF16), ln1_g=ln1_g[l], ln1_b=ln1_b[l],
                 w_up=w_up[l].astype(BF16), b_up=b_up[l], w_down=w_down[l].astype(BF16), b_down=b_down[l],
                 ln2_g=ln2_g[l], ln2_b=ln2_b[l])
        mod = _modulation(jnp.concatenate([c_prompt, c_sample], axis=0), w_ada[l].astype(BF16), b_ada[l])
        yp, bufs_p, st_p, shifted = _prompt_layer(yp, mod[:nb], views, x_sample.shape[1], l, w, alpha)
        ys, bufs_s, st_s = _sample_layer(ys, mod[nb:], views, shifted, state_hgrn, l, w, alpha)
        p_bufs.append(bufs_p)
        p_states.append(st_p)
        s_bufs.append([buf.reshape(c.shape[1:]) for buf, c in zip(bufs_s, caches)])
        s_states.append(st_s)
    group = lambda bufs, gi: _stack_layers([bl[gi] for bl in bufs])
    return (yp, ys, group(p_bufs, 0), group(p_bufs, 1), group(p_bufs, 2), _stack_layers(p_states),
            group(s_bufs, 0), group(s_bufs, 1), group(s_bufs, 2), _stack_layers(s_states))
```

```python
import functools
import math
from typing import NamedTuple

import jax
import jax.numpy as jnp
from jax import lax
from jax.experimental import pallas as pl
from jax.experimental.pallas import tpu as pltpu

F32 = jnp.float32
BF16 = jnp.bfloat16

ATTN_GROUPS = ((128, 1), (512, 4), (2048, 16))
N_GROUPS = len(ATTN_GROUPS)
HEADS_PER_GROUP = 4
HEAD_DIM = 128
N_ATTN_HEADS = N_GROUPS * HEADS_PER_GROUP
ATTN_W = N_ATTN_HEADS * HEAD_DIM
GROUP_W = HEADS_PER_GROUP * HEAD_DIM
BAND = 128
HG_DK = 128
HG_DV = 128
LN_EPS = 1e-5
RMS_EPS = 1e-6
NEG = -1e30

SUBLANES = 8
LANES = 128
VMEM_LIMIT_BYTES = 56 * 1024 * 1024
COL_CHUNK = 512

NT_DIMS = (((1,), (1,)), ((), ()))
TN_DIMS = (((0,), (0,)), ((), ()))


def _alibi_slope(group, head):
    return 2.0 ** (-8.0 * (group * HEADS_PER_GROUP + head + 1) / N_ATTN_HEADS)


def _cparams(*sem):
    return pltpu.CompilerParams(dimension_semantics=sem, vmem_limit_bytes=VMEM_LIMIT_BYTES)


def _resident(shape):
    nd = len(shape)
    return pl.BlockSpec(shape, lambda *_: (0,) * nd, pipeline_mode=pl.Buffered(1))


def _sigmoid(x):
    return 1.0 / (1.0 + jnp.exp(-x))


def _layer_norm(h, g, b):
    mu = jnp.mean(h, axis=-1, keepdims=True)
    hc = h - mu
    var = jnp.mean(hc * hc, axis=-1, keepdims=True)
    return hc * lax.rsqrt(var + LN_EPS) * g + b


class _Shift(NamedTuple):
    layer: int
    b0: int
    nbatch: int
    cb: int
    rsplit: int
    drop: int
    keep: int

    @property
    def n_chunks(self):
        return self.nbatch // self.cb * self.rsplit


def _plan_shift(layer, b0, nbatch, n_steps, drop, keep):
    cb = -(-nbatch // n_steps)
    while nbatch % cb:
        cb += 1
    rsplit = 1
    for r in (8, 4, 2):
        if nbatch // cb * r <= n_steps and keep % (r * SUBLANES) == 0:
            rsplit = r
            break
    return _Shift(layer, b0, nbatch, cb, rsplit, drop, keep)


def _shift_copy(shift, src_ref, dst_ref, buf, in_sem, out_sem, chunk, fetch):
    rows = shift.keep // shift.rsplit
    b = shift.b0 + (chunk // shift.rsplit) * shift.cb
    r = (chunk % shift.rsplit) * rows
    slot = chunk % 2
    if fetch:
        return pltpu.make_async_copy(
            src_ref.at[shift.layer, pl.ds(b, shift.cb), pl.ds(shift.drop + r, rows), :], buf.at[slot], in_sem.at[slot])
    return pltpu.make_async_copy(buf.at[slot], dst_ref.at[pl.ds(b, shift.cb), pl.ds(r, rows), :], out_sem.at[slot])


def _host_shift(kernel_fn, shift, n_in, n_out, aliased, grid):
    n_steps = math.prod(grid)
    n_chunks = shift.n_chunks
    assert n_chunks <= n_steps

    def hosted(*refs):
        src = refs[n_in]
        pos = n_in + (2 if aliased else 1)
        outs = refs[pos:pos + n_out]
        dst = refs[pos + n_out]
        scratch = refs[pos + n_out + 1:-3]
        buf, in_sem, out_sem = refs[-3:]
        fetch = functools.partial(_shift_copy, shift, src, dst, buf, in_sem, out_sem, fetch=True)
        put = functools.partial(_shift_copy, shift, src, dst, buf, in_sem, out_sem, fetch=False)
        step = 0
        for axis in range(len(grid)):
            step = step * grid[axis] + pl.program_id(axis)

        @pl.when((step >= 1) & (step <= n_chunks))
        def _():
            fetch(step - 1).wait()
            put(step - 1).start()

        @pl.when((step >= 2) & (step <= n_chunks + 1))
        def _():
            put(step - 2).wait()

        @pl.when(step < n_chunks)
        def _():
            fetch(step).start()

        kernel_fn(*refs[:n_in], *outs, *scratch)

        @pl.when(step == n_steps - 1)
        def _():
            if n_chunks == n_steps:
                fetch(n_steps - 1).wait()
                put(n_steps - 1).start()
            for c in range(max(n_steps - 2, 0), n_chunks):
                put(c).wait()

    return hosted


def _hosting_call(kernel_fn, shift, cache, dst, *, out_shape, grid, in_specs, out_specs, args,
                  scratch_shapes=(), **kwargs):
    if shift is None:
        res = pl.pallas_call(kernel_fn, out_shape=tuple(out_shape), grid=grid, in_specs=list(in_specs),
                             out_specs=tuple(out_specs), scratch_shapes=list(scratch_shapes), **kwargs)(*args)
        return tuple(res), dst
    n_in, n_out = len(in_specs), len(out_shape)
    aliased = dst is not None
    anywhere = pl.BlockSpec(memory_space=pl.ANY)
    res = pl.pallas_call(
        _host_shift(kernel_fn, shift, n_in, n_out, aliased, grid),
        out_shape=tuple(out_shape) + (jax.ShapeDtypeStruct(cache.shape[1:], cache.dtype),),
        grid=grid,
        in_specs=list(in_specs) + [anywhere] * (2 if aliased else 1),
        out_specs=tuple(out_specs) + (anywhere,),
        scratch_shapes=list(scratch_shapes) + [
            pltpu.VMEM((2, shift.cb, shift.keep // shift.rsplit, cache.shape[3]), cache.dtype),
            pltpu.SemaphoreType.DMA((2,)), pltpu.SemaphoreType.DMA((2,))],
        input_output_aliases={n_in + 1: n_out} if aliased else {},
        **kwargs,
    )(*args, cache, *([dst] if aliased else []))
    return tuple(res[:-1]), res[-1]


def _mod_kernel(c_ref, w_ref, b_ref, o_ref):
    c = c_ref[...]
    a = (c * _sigmoid(c)).astype(BF16)
    o_ref[...] = jnp.dot(a, w_ref[...], preferred_element_type=F32) + b_ref[...]


def _modulation(c, w_ada, b_ada):
    n, d = c.shape
    nout = w_ada.shape[1]
    tn = 1024
    return pl.pallas_call(
        _mod_kernel,
        out_shape=jax.ShapeDtypeStruct((n, nout), F32),
        grid=(nout // tn,),
        in_specs=[pl.BlockSpec((n, d), lambda j: (0, 0)),
                  pl.BlockSpec((d, tn), lambda j: (0, j)),
                  pl.BlockSpec((1, tn), lambda j: (0, j))],
        out_specs=pl.BlockSpec((n, tn), lambda j: (0, j)),
        compiler_params=_cparams("arbitrary"),
        name="modulation",
    )(c, w_ada, b_ada.reshape(1, nout))


def _store_heads(ref, c0, y):
    for j in range(y.shape[1] // HG_DK):
        ref[c0 // HG_DK + j] = y[:, j * HG_DK:(j + 1) * HG_DK].astype(ref.dtype)


def _inproj_kernel(x_ref, mod_ref, w_ref, lbp_ref, attn_ref, hq_ref, hk_ref, hv_ref, logf_ref, og_ref,
                   gates_ref, *, layer):
    gb, rb, d = x_ref.shape
    tm = gb * rb
    hg_w = hq_ref.shape[0] * HG_DK
    act = attn_ref.dtype
    sh = mod_ref[:, :, 0:d]
    sc = mod_ref[:, :, d:2 * d]
    u = (x_ref[...] * (1.0 + sc) + sh).reshape(tm, d).astype(BF16)

    def proj(c0):
        return jnp.dot(u, w_ref[:, c0:c0 + COL_CHUNK], preferred_element_type=F32)

    for c0 in range(0, 3 * ATTN_W, COL_CHUNK):
        attn_ref[:, c0:c0 + COL_CHUNK] = proj(c0).astype(act)
    base = 3 * ATTN_W
    for c0 in range(0, hg_w, COL_CHUNK):
        y = proj(base + c0)
        _store_heads(hq_ref, c0, y * _sigmoid(y))
    lbp = lbp_ref[...]
    e = jnp.exp(lbp - jnp.max(lbp, axis=0, keepdims=True))
    lb = jnp.sum(e[0:layer + 1], axis=0, keepdims=True) / jnp.sum(e, axis=0, keepdims=True)
    base += hg_w
    for c0 in range(0, hg_w, COL_CHUNK):
        lbc = lb[:, c0:c0 + COL_CHUNK]
        f = lbc + (1.0 - lbc) * _sigmoid(proj(base + c0))
        _store_heads(logf_ref, c0, jnp.log(f))
        _store_heads(hk_ref, c0, 1.0 - f)
    base += hg_w
    for c0 in range(0, hg_w, COL_CHUNK):
        _store_heads(hv_ref, c0, proj(base + c0))
    base += hg_w
    for c0 in range(0, hg_w, COL_CHUNK):
        y = proj(base + c0)
        _store_heads(og_ref, c0, y * _sigmoid(y))
    base += hg_w
    for c0 in range(0, 2 * d, COL_CHUNK):
        gates_ref[:, c0:c0 + COL_CHUNK] = _sigmoid(proj(base + c0)).astype(act)


def _in_projection(x3, mod3, w_in, lb_param, layer, gb, rb, act, hosted=(None, None, None)):
    g, r, d = x3.shape
    n = g * r
    tm = gb * rb
    hg_w = lb_param.shape[1]
    rt = r // rb
    heads = hg_w // HG_DK
    rows = lambda width: pl.BlockSpec((tm, width), lambda gi, ri: (gi * rt + ri, 0))
    by_head = pl.BlockSpec((heads, tm, HG_DK), lambda gi, ri: (0, gi * rt + ri, 0))
    head_major = lambda dtype: jax.ShapeDtypeStruct((heads, n, HG_DK), dtype)
    out_shapes = (
        jax.ShapeDtypeStruct((n, 3 * ATTN_W), act),
        head_major(act),
        head_major(act),
        head_major(act),
        head_major(F32),
        head_major(act),
        jax.ShapeDtypeStruct((n, 2 * d), act),
    )
    return _hosting_call(
        functools.partial(_inproj_kernel, layer=layer), *hosted,
        out_shape=out_shapes,
        grid=(g // gb, rt),
        in_specs=[pl.BlockSpec((gb, rb, d), lambda gi, ri: (gi, ri, 0)),
                  pl.BlockSpec((gb, 1, mod3.shape[2]), lambda gi, ri: (gi, 0, 0)),
                  _resident(w_in.shape),
                  _resident(lb_param.shape)],
        out_specs=(rows(3 * ATTN_W), by_head, by_head, by_head, by_head, by_head, rows(2 * d)),
        args=(x3, mod3, w_in, lb_param),
        compiler_params=_cparams("arbitrary", "arbitrary"),
        name="in_projection",
    )


ATTN_BLOCKS_PER_ITER = 4


def _attn_prompt_kernel(*refs, head_axis):
    in_refs = refs[:3 * N_GROUPS]
    attn_ref = refs[3 * N_GROUPS]
    qf, kf, vf, mf, lf, af = refs[3 * N_GROUPS + 1:]
    s = attn_ref.shape[0]
    nblk = s // BAND
    head = pl.program_id(head_axis)
    qi = lax.broadcasted_iota(jnp.int32, (BAND, BAND), 0)
    kj = lax.broadcasted_iota(jnp.int32, (BAND, BAND), 1)
    steps_own = qi - kj
    steps_prev = steps_own + BAND
    scale = HEAD_DIM ** -0.5
    ones = jnp.ones((BAND, HEAD_DIM), BF16)
    for g, (window, dil) in enumerate(ATTN_GROUPS):
        q_ref, k_ref, v_ref = in_refs[3 * g:3 * g + 3]
        qf[...] = q_ref[...].astype(F32)
        kf[...] = k_ref[...].astype(F32)
        vf[...] = v_ref[...].astype(F32)
        nb = nblk // dil
        slope = jnp.float32(_alibi_slope(g, HEADS_PER_GROUP - 1) * dil)
        for h in range(HEADS_PER_GROUP - 1):
            slope = jnp.where(head == h, jnp.float32(_alibi_slope(g, h) * dil), slope)
        bias_own = jnp.where(steps_own >= 0, -slope * steps_own.astype(F32), NEG)
        bias_prev = jnp.where(steps_prev <= BAND, -slope * steps_prev.astype(F32), NEG)

        def body(it, carry, g=g, dil=dil, nb=nb, bias_own=bias_own, bias_prev=bias_prev):
            blocks = []
            for u in range(ATTN_BLOCKS_PER_ITER):
                idx = it * ATTN_BLOCKS_PER_ITER + u
                r = idx // nb
                i = idx - r * nb
                start = r + i * (BAND * dil)
                start_prev = r + jnp.maximum(i - 1, 0) * (BAND * dil)
                rows = pl.ds(start, BAND, stride=dil) if dil > 1 else pl.ds(start, BAND)
                rows_prev = pl.ds(start_prev, BAND, stride=dil) if dil > 1 else pl.ds(start_prev, BAND)
                blocks.append((rows, rows_prev, i > 0))
            scores = []
            for rows, rows_prev, has_prev in blocks:
                q = qf[rows, :].astype(BF16)
                scores.append((lax.dot_general(q, kf[rows, :].astype(BF16), NT_DIMS, preferred_element_type=F32),
                               lax.dot_general(q, kf[rows_prev, :].astype(BF16), NT_DIMS,
                                               preferred_element_type=F32)))
            probs = []
            for (rows, rows_prev, has_prev), (s_own, s_prev) in zip(blocks, scores):
                s_own = s_own * scale + bias_own
                s_prev = jnp.where(has_prev, s_prev * scale + bias_prev, NEG)
                m = jnp.max(jnp.maximum(s_own, s_prev), axis=-1, keepdims=True)
                probs.append((m, jnp.exp(s_own - m).astype(BF16), jnp.exp(s_prev - m).astype(BF16)))
            results = []
            for (rows, rows_prev, has_prev), (m, p_own, p_prev) in zip(blocks, probs):
                both = (jnp.dot(p_own, jnp.concatenate([vf[rows, :].astype(BF16), ones], axis=1),
                                preferred_element_type=F32)
                        + jnp.dot(p_prev, jnp.concatenate([vf[rows_prev, :].astype(BF16), ones], axis=1),
                                  preferred_element_type=F32))
                acc, l = both[:, :HEAD_DIM], both[:, HEAD_DIM:]
                if g > 0:
                    m_old = mf[rows, :]
                    m_new = jnp.maximum(m_old, m)
                    a_old = jnp.exp(m_old - m_new)
                    a_blk = jnp.exp(m - m_new)
                    l = a_old * lf[rows, :] + a_blk * l
                    acc = a_old * af[rows, :] + a_blk * acc
                    m = m_new
                results.append((rows, m, l, acc))
            for rows, m, l, acc in results:
                mf[rows, :] = jnp.broadcast_to(m, (BAND, HEAD_DIM))
                lf[rows, :] = l
                af[rows, :] = acc
            return carry

        lax.fori_loop(0, nblk // ATTN_BLOCKS_PER_ITER, body, 0)
    attn_ref[...] = (af[...] / lf[...]).astype(attn_ref.dtype)


def _attention_prompt(qkv, b, s, hosted=(None, None, None)):
    hpq = ATTN_W // HEAD_DIM
    in_specs = []
    for g in range(N_GROUPS):
        for section in range(3):
            base = section * hpq + g * HEADS_PER_GROUP
            in_specs.append(pl.BlockSpec((s, HEAD_DIM), lambda bi, h, base=base: (bi, base + h)))
    (attn,), dst = _hosting_call(
        functools.partial(_attn_prompt_kernel, head_axis=1), *hosted,
        out_shape=(jax.ShapeDtypeStruct((b * s, GROUP_W), BF16),),
        grid=(b, HEADS_PER_GROUP),
        in_specs=in_specs,
        out_specs=(pl.BlockSpec((s, HEAD_DIM), lambda bi, h: (bi, h)),),
        scratch_shapes=[pltpu.VMEM((s, HEAD_DIM), F32) for _ in range(6)],
        args=[qkv] * (3 * N_GROUPS),
        compiler_params=_cparams("arbitrary", "arbitrary"),
        name="attn_prompt",
    )
    return attn, dst


def _attn_sample_kernel(*refs, group, dil, n_steps, length, seqs):
    first = group == 0
    last = group == N_GROUPS - 1
    q_ref, newkv_ref, cache_ref = refs[:3]
    pos = 3
    if not first:
        m_in, l_in, a_in = refs[pos:pos + 3]
        pos += 3
    pos += 1
    newc_ref = refs[pos]
    outs = refs[pos + 1:]
    t = q_ref.shape[0] // seqs
    rp = 2 * HEADS_PER_GROUP
    stretch = max(dil, t)
    nkeys = length // stretch * t
    j_c = lax.broadcasted_iota(jnp.int32, (t, nkeys), 0)
    c_c = lax.broadcasted_iota(jnp.int32, (t, nkeys), 1)
    dist_c = length + j_c - ((c_c // t) * stretch + (c_c & (t - 1)))
    ok_c = ((dist_c & (dil - 1)) == 0) & (dist_c <= n_steps * dil)
    j_n = lax.broadcasted_iota(jnp.int32, (t, t), 0)
    r_n = lax.broadcasted_iota(jnp.int32, (t, t), 1)
    dist_n = j_n - r_n
    ok_n = (dist_n >= 0) & ((dist_n & (dil - 1)) == 0) & (dist_n <= n_steps * dil)
    scale = HEAD_DIM ** -0.5
    qoff = group * GROUP_W
    cache_rows = lambda s, first_row: cache_ref[s, :, pl.ds(first_row, t, stride=rp), :].reshape(
        nkeys, HEAD_DIM).astype(BF16)
    new_rows = lambda s, first_row: newkv_ref[s, pl.ds(first_row, t, stride=rp), :].astype(BF16)
    chains = [(s, h) for s in range(seqs) for h in range(HEADS_PER_GROUP)]
    scores = []
    for s, h in chains:
        q = q_ref[s * t:(s + 1) * t, qoff + h * HEAD_DIM:qoff + (h + 1) * HEAD_DIM].astype(BF16)
        scores.append((lax.dot_general(q, cache_rows(s, h), NT_DIMS, preferred_element_type=F32),
                       lax.dot_general(q, new_rows(s, h), NT_DIMS, preferred_element_type=F32)))
    probs = []
    for (s, h), (s_c, s_n) in zip(chains, scores):
        rows, sl = slice(s * t, (s + 1) * t), slice(h * HEAD_DIM, (h + 1) * HEAD_DIM)
        slope = _alibi_slope(group, h)
        s_c = jnp.where(ok_c, s_c * scale - slope * dist_c.astype(F32), NEG)
        s_n = jnp.where(ok_n, s_n * scale - slope * dist_n.astype(F32), NEG)
        m = jnp.maximum(jnp.max(s_c, axis=-1, keepdims=True), jnp.max(s_n, axis=-1, keepdims=True))
        if not first:
            m = jnp.maximum(m, m_in[rows, sl][:, 0:1])
        p_c = jnp.exp(s_c - m)
        p_n = jnp.exp(s_n - m)
        l = jnp.sum(p_c, axis=-1, keepdims=True) + jnp.sum(p_n, axis=-1, keepdims=True)
        probs.append((m, l, p_c.astype(BF16), p_n.astype(BF16)))
    for (s, h), (m, l, p_c, p_n) in zip(chains, probs):
        rows, sl = slice(s * t, (s + 1) * t), slice(h * HEAD_DIM, (h + 1) * HEAD_DIM)
        acc = (jnp.dot(p_c, cache_rows(s, HEADS_PER_GROUP + h), preferred_element_type=F32)
               + jnp.dot(p_n, new_rows(s, HEADS_PER_GROUP + h), preferred_element_type=F32))
        if not first:
            a_old = jnp.exp(m_in[rows, sl] - m)
            l = a_old * l_in[rows, sl] + l
            acc = a_old * a_in[rows, sl] + acc
        if last:
            outs[0][rows, sl] = acc / l
        else:
            outs[0][rows, sl] = jnp.broadcast_to(m, (t, HEAD_DIM))
            outs[1][rows, sl] = jnp.broadcast_to(l, (t, HEAD_DIM))
            outs[2][rows, sl] = acc
    for s in range(seqs):
        newc_ref[s] = newkv_ref[s]


SAMPLE_SEQS_PER_STEP = (8, 4, 2)


def _attention_sample(qkv, new_kv, cache, shifted, layer, running, t, group):
    window, dil = ATTN_GROUPS[group]
    depth, bd, crow, dh = cache.shape
    rp = 2 * HEADS_PER_GROUP
    length = crow // rp
    stretch = max(dil, t)
    seqs = math.gcd(SAMPLE_SEQS_PER_STEP[group], bd)
    first = group == 0
    last = group == N_GROUPS - 1
    rows = pl.BlockSpec((seqs * t, GROUP_W), lambda bi: (bi, 0))
    stat = jax.ShapeDtypeStruct((bd * t, GROUP_W), F32)
    new_blk = pl.BlockSpec((seqs, t * rp, dh), lambda bi: (bi, 0, 0))
    in_specs = [pl.BlockSpec((seqs * t, 3 * ATTN_W), lambda bi: (bi, 0)),
                new_blk,
                pl.BlockSpec((None, seqs, length // stretch, t * rp, dh), lambda bi: (layer, bi, 0, 0, 0))]
    args = [qkv, new_kv, cache.reshape(depth, bd, length // stretch, stretch * rp, dh)]
    if not first:
        in_specs += [rows, rows, rows]
        args += list(running)
    in_specs.append(pl.BlockSpec(memory_space=pl.ANY))
    args.append(shifted)
    n_out = 1 if last else 3
    res = pl.pallas_call(
        functools.partial(_attn_sample_kernel, group=group, dil=dil, n_steps=window // dil, length=length,
                          seqs=seqs),
        out_shape=(jax.ShapeDtypeStruct(shifted.shape, shifted.dtype),) + (stat,) * n_out,
        grid=(bd // seqs,),
        in_specs=in_specs,
        out_specs=(pl.BlockSpec((seqs, t * rp, dh), lambda bi: (bi, length // t - 1, 0)),) + (rows,) * n_out,
        input_output_aliases={len(args) - 1: 0},
        compiler_params=_cparams("arbitrary"),
        name=f"attn_sample_g{group}",
    )(*args)
    return res[0], (res[1] if last else tuple(res[1:]))


def _cumsum_rows(x, row):
    d = 1
    while d < x.shape[0]:
        x = x + jnp.where(row >= d, pltpu.roll(x, d, 0), 0.0)
        d *= 2
    return x


def _hgrn_carry(q, k, v, g, st):
    c = q.shape[0]
    qg = (q * jnp.exp(g)).astype(BF16)
    o = lax.dot_general(qg, st.astype(BF16), NT_DIMS, preferred_element_type=F32)
    g_last = g[c - 1:c, :]
    kd = (k * jnp.exp(g_last - g)).astype(BF16)
    st_new = st * jnp.exp(g_last) + lax.dot_general(v.astype(BF16), kd, TN_DIMS, preferred_element_type=F32)
    return o, st_new


def _hgrn_chunk_pairwise(q, k, v, lf, st):
    row = lax.broadcasted_iota(jnp.int32, q.shape, 0)
    g = _cumsum_rows(lf, row)
    o = jnp.sum(q * k, axis=-1, keepdims=True) * v
    for d in range(1, q.shape[0]):
        ok = row >= d
        e = jnp.exp(jnp.where(ok, g - pltpu.roll(g, d, 0), 0.0))
        a = jnp.sum(jnp.where(ok, q * pltpu.roll(k, d, 0) * e, 0.0), axis=-1, keepdims=True)
        o = o + a * pltpu.roll(v, d, 0)
    o_carry, st_new = _hgrn_carry(q, k, v, g, st)
    return o + o_carry, st_new


def _pair_level(c):
    ri = lax.broadcasted_iota(jnp.int32, (c, c), 0)
    ci = lax.broadcasted_iota(jnp.int32, (c, c), 1)
    x = ri ^ ci
    bits = jnp.zeros((c, c), jnp.int32)
    p = 1
    while p < c:
        bits = bits + (x >= p).astype(jnp.int32)
        p *= 2
    return jnp.where(ri > ci, bits + 1, jnp.where(ri == ci, 1, 0))


def _hgrn_chunk_blocked(q, k, v, lf, st, g_ref, level):
    c = q.shape[0]
    row = lax.broadcasted_iota(jnp.int32, (c, HG_DK), 0)
    x = lf
    sub = row & (SUBLANES - 1)
    for d in (1, 2, 4):
        x = x + jnp.where(sub >= d, pltpu.roll(x, d, 0), 0.0)
    g_ref[...] = x
    pieces = [x[0:SUBLANES]]
    off = None
    for j in range(1, c // SUBLANES):
        tot = jnp.broadcast_to(g_ref[j * SUBLANES - 1:j * SUBLANES, :], (SUBLANES, HG_DK))
        off = tot if off is None else off + tot
        pieces.append(x[j * SUBLANES:(j + 1) * SUBLANES] + off)
    g = jnp.concatenate(pieces, axis=0)
    g_ref[...] = g

    def mid_rows(n):
        half = n // 2
        sub8 = lax.broadcasted_iota(jnp.int32, (SUBLANES, HG_DK), 0)
        bcast = lambda r: jnp.broadcast_to(g_ref[r:r + 1, :], (SUBLANES, HG_DK))
        out = []
        for j in range(c // SUBLANES):
            base = j * SUBLANES
            if n >= SUBLANES:
                out.append(bcast(base // n * n + half - 1))
            else:
                assert n == 4
                out.append(jnp.where(sub8 < 4, bcast(base + 1), bcast(base + 5)))
        return jnp.concatenate(out, axis=0)

    amat = jnp.where(level == 1, lax.dot_general(q.astype(BF16), k.astype(BF16), NT_DIMS,
                                                 preferred_element_type=F32), 0.0)
    n = 2
    while n <= c:
        second = (row & (n // 2)) != 0
        if n == 2:
            arg = jnp.where(second, lf, 0.0)
        else:
            gmid = mid_rows(n)
            arg = jnp.where(second, g - gmid, gmid - g)
        y = (jnp.where(second, q, k) * jnp.exp(arg)).astype(BF16)
        amat = jnp.where(level == n.bit_length(), lax.dot_general(y, y, NT_DIMS, preferred_element_type=F32),
                         amat)
        n *= 2
    o = jnp.dot(amat.astype(BF16), v.astype(BF16), preferred_element_type=F32)
    o_carry, st_new = _hgrn_carry(q, k, v, g, st)
    return o + o_carry, st_new


def _hgrn_output(o, og, nw):
    return o * lax.rsqrt(jnp.mean(o * o, axis=-1, keepdims=True) + RMS_EPS) * nw * og


HGRN_HEADS_PER_ITER = 2


def _hgrn_prompt_kernel(q_ref, k_ref, v_ref, lf_ref, og_ref, nw_ref, hg_ref, sfin_ref, st_ref, g_ref):
    ci = pl.program_id(1)
    heads, c, _ = q_ref.shape

    @pl.when(ci == 0)
    def _():
        st_ref[...] = jnp.zeros_like(st_ref)

    level = _pair_level(c)

    def head_group(i, carry):
        for u in range(HGRN_HEADS_PER_ITER):
            h = i * HGRN_HEADS_PER_ITER + u
            o, st_new = _hgrn_chunk_blocked(q_ref[h].astype(F32), k_ref[h].astype(F32), v_ref[h].astype(F32),
                                            lf_ref[h], st_ref[h], g_ref.at[u], level)
            st_ref[h] = st_new
            hg_ref[h] = _hgrn_output(o, og_ref[h].astype(F32), nw_ref[...]).astype(hg_ref.dtype)
        return carry

    lax.fori_loop(0, heads // HGRN_HEADS_PER_ITER, head_group, 0)

    @pl.when(ci == pl.num_programs(1) - 1)
    def _():
        for h in range(heads):
            sfin_ref[0, h] = st_ref[h].T


def _hgrn_prompt(hq, hk, hv, logf, og, norm_w, b, s, chunk, hosted=(None, None, None)):
    heads, n, _ = hq.shape
    nc = s // chunk
    blk = pl.BlockSpec((heads, chunk, HG_DK), lambda bi, ci: (0, bi * nc + ci, 0))
    (hg, s_fin), dst = _hosting_call(
        _hgrn_prompt_kernel, *hosted,
        out_shape=(jax.ShapeDtypeStruct((heads, n, HG_DV), BF16),
                   jax.ShapeDtypeStruct((b, heads, HG_DK, HG_DV), F32)),
        grid=(b, nc),
        in_specs=[blk, blk, blk, blk, blk, pl.BlockSpec((1, HG_DV), lambda bi, ci: (0, 0))],
        out_specs=(blk, pl.BlockSpec((1, heads, HG_DK, HG_DV), lambda bi, ci: (bi, 0, 0, 0))),
        scratch_shapes=[pltpu.VMEM((heads, HG_DV, HG_DK), F32),
                        pltpu.VMEM((HGRN_HEADS_PER_ITER, chunk, HG_DK), F32)],
        args=(hq, hk, hv, logf, og, norm_w.reshape(1, HG_DV)),
        compiler_params=_cparams("arbitrary", "arbitrary"),
        name="hgrn_prompt",
    )
    return hg, s_fin, dst


def _hgrn_sample_kernel(q_ref, k_ref, v_ref, lf_ref, og_ref, nw_ref, s_ref, hg_ref, snew_ref):
    heads = s_ref.shape[1]
    for h in range(heads):
        o, st_new = _hgrn_chunk_pairwise(q_ref[h], k_ref[h], v_ref[h], lf_ref[h], s_ref[0, h].T)
        snew_ref[0, h] = st_new.T
        hg_ref[h] = _hgrn_output(o, og_ref[h], nw_ref[...])


def _hgrn_sample(hq, hk, hv, logf, og, norm_w, state, t):
    bd, heads = state.shape[0], state.shape[1]
    blk = pl.BlockSpec((heads, t, HG_DK), lambda bi: (0, bi, 0))
    sblk = pl.BlockSpec((1, heads, HG_DK, HG_DV), lambda bi: (bi, 0, 0, 0))
    return pl.pallas_call(
        _hgrn_sample_kernel,
        out_shape=(jax.ShapeDtypeStruct((heads, bd * t, HG_DV), F32),
                   jax.ShapeDtypeStruct(state.shape, state.dtype)),
        grid=(bd,),
        in_specs=[blk, blk, blk, blk, blk, pl.BlockSpec((1, HG_DV), lambda bi: (0, 0)), sblk],
        out_specs=(blk, sblk),
        compiler_params=_cparams("arbitrary"),
        name="hgrn_sample",
    )(hq, hk, hv, logf, og, norm_w.reshape(1, HG_DV), state)


def _merge_kernel(x_ref, mod_ref, attn_ref, hg_ref, gates_ref, wa_ref, wb_ref, wo_ref, lng_ref, lnb_ref,
                  x1_ref, *, alpha):
    gb, rb, d = x_ref.shape
    tm = gb * rb
    hg = jnp.concatenate([hg_ref[h].astype(BF16) for h in range(hg_ref.shape[0])], axis=1)
    branch_a = jnp.dot(attn_ref[...].astype(BF16), wa_ref[...], preferred_element_type=F32)
    branch_b = jnp.dot(hg, wb_ref[...], preferred_element_type=F32)
    gate_a = gates_ref[:, 0:d].astype(F32)
    gate_b = gates_ref[:, d:2 * d].astype(F32)
    merged = gate_a * branch_a + gate_b * branch_b
    mix = jnp.dot(merged.astype(BF16), wo_ref[...], preferred_element_type=F32)
    g1 = mod_ref[:, :, 2 * d:3 * d]
    h = alpha * x_ref[...] + g1 * mix.reshape(gb, rb, d)
    x1_ref[...] = _layer_norm(h, lng_ref[...], lnb_ref[...])


def _merge(x3, mod3, attn, hg, gates, wa, wb, wo, ln_g, ln_b, gb, rb, alpha, hosted=(None, None, None)):
    g, r, d = x3.shape
    tm = gb * rb
    rt = r // rb
    row = lambda gi, ri: (gi * rt + ri, 0)
    rows = lambda a: pl.BlockSpec((tm, a.shape[1]), row)
    tile3 = pl.BlockSpec((gb, rb, d), lambda gi, ri: (gi, ri, 0))
    (x1,), dst = _hosting_call(
        functools.partial(_merge_kernel, alpha=alpha), *hosted,
        out_shape=(jax.ShapeDtypeStruct(x3.shape, F32),),
        grid=(g // gb, rt),
        in_specs=[tile3, pl.BlockSpec((gb, 1, mod3.shape[2]), lambda gi, ri: (gi, 0, 0)),
                  rows(attn), pl.BlockSpec((hg.shape[0], tm, hg.shape[2]), lambda gi, ri: (0, gi * rt + ri, 0)),
                  rows(gates),
                  _resident(wa.shape), _resident(wb.shape), _resident(wo.shape),
                  _resident((1, d)), _resident((1, d))],
        out_specs=(tile3,),
        args=(x3, mod3, attn, hg, gates, wa, wb, wo, ln_g.reshape(1, d), ln_b.reshape(1, d)),
        compiler_params=_cparams("arbitrary", "arbitrary"),
        name="merge",
    )
    return x1, dst


def _mlp_kernel(x1_ref, mod_ref, wu_ref, bu_ref, wd_ref, bd_ref, lng_ref, lnb_ref, x2_ref, *, alpha):
    gb, rb, d = x1_ref.shape
    tm = gb * rb
    dff = wu_ref.shape[1]
    x1 = x1_ref[...]
    sh = mod_ref[:, :, 3 * d:4 * d]
    sc = mod_ref[:, :, 4 * d:5 * d]
    g2 = mod_ref[:, :, 5 * d:6 * d]
    u = (x1 * (1.0 + sc) + sh).reshape(tm, d).astype(BF16)
    acc = jnp.zeros((tm, d), F32)
    for c0 in range(0, dff, COL_CHUNK):
        hid = jnp.dot(u, wu_ref[:, c0:c0 + COL_CHUNK], preferred_element_type=F32) + bu_ref[:, c0:c0 + COL_CHUNK]
        hid = jnp.square(jnp.maximum(hid, 0.0))
        acc = acc + jnp.dot(hid.astype(BF16), wd_ref[c0:c0 + COL_CHUNK, :], preferred_element_type=F32)
    ff = acc + bd_ref[...]
    h = alpha * x1 + g2 * ff.reshape(gb, rb, d)
    x2_ref[...] = _layer_norm(h, lng_ref[...], lnb_ref[...])


def _mlp(x3, mod3, wu, bu, wd, bd, ln_g, ln_b, gb, rb, alpha, hosted=(None, None, None)):
    g, r, d = x3.shape
    dff = wu.shape[1]
    tile3 = pl.BlockSpec((gb, rb, d), lambda gi, ri: (gi, ri, 0))
    (x2,), dst = _hosting_call(
        functools.partial(_mlp_kernel, alpha=alpha), *hosted,
        out_shape=(jax.ShapeDtypeStruct(x3.shape, F32),),
        grid=(g // gb, r // rb),
        in_specs=[tile3, pl.BlockSpec((gb, 1, mod3.shape[2]), lambda gi, ri: (gi, 0, 0)),
                  _resident(wu.shape), _resident((1, dff)), _resident(wd.shape), _resident((1, d)),
                  _resident((1, d)), _resident((1, d))],
        out_specs=(tile3,),
        args=(x3, mod3, wu, bu.reshape(1, dff), wd, bd.reshape(1, d), ln_g.reshape(1, d), ln_b.reshape(1, d)),
        compiler_params=_cparams("arbitrary", "arbitrary"),
        name="mlp",
    )
    return x2, dst


PROMPT_ROWS = 256
SAMPLE_SEQS = 32
HGRN_CHUNK = 128


def _kv_rows(qkv, b, s, group, rows):
    q3 = qkv.reshape(b, s, 3 * ATTN_W)
    k = q3[:, s - rows:, ATTN_W + group * GROUP_W:ATTN_W + (group + 1) * GROUP_W]
    v = q3[:, s - rows:, 2 * ATTN_W + group * GROUP_W:2 * ATTN_W + (group + 1) * GROUP_W]
    return jnp.stack([k, v], axis=2).astype(F32).reshape(b, rows, 2, HEADS_PER_GROUP, HEAD_DIM)


def _prompt_layer(x, mod, caches, t_new, layer, w, alpha):
    b, s, d = x.shape
    mod3 = mod.reshape(b, 1, mod.shape[1])
    gb, rb = 1, min(PROMPT_ROWS, s)
    chunk = min(HGRN_CHUNK, s)
    rp = 2 * HEADS_PER_GROUP
    bd = caches[0].shape[1]
    tile_steps = s // rb * b
    hgrn_steps = s // chunk * b

    def plan(gi, b0, nbatch, n_steps):
        rows = caches[gi].shape[2]
        return _plan_shift(layer, b0, nbatch, n_steps, t_new * rp, rows - t_new * rp)

    half = bd // 2 if bd % 2 == 0 else 0
    (qkv, hq, hk, hv, logf, og, gates), mid = _in_projection(
        x, mod3, w["w_in"], w["lb_param"], layer, gb, rb, BF16,
        hosted=(plan(1, 0, bd, tile_steps), caches[1], None))
    hosted = (plan(2, 0, half, hgrn_steps), caches[2], None) if half else (None, None, None)
    hg, s_fin, big = _hgrn_prompt(hq, hk, hv, logf, og, w["hg_norm_w"], b, s, chunk, hosted=hosted)
    attn, _ = _attention_prompt(qkv, b, s)
    new_bufs = [_kv_rows(qkv, b, s, gi, min(window, s)) for gi, (window, dil) in enumerate(ATTN_GROUPS)]
    x1, small = _merge(x, mod3, attn, hg, gates, w["w_branch_a"], w["w_branch_b"], w["w_out"],
                       w["ln1_g"], w["ln1_b"], gb, rb, alpha, hosted=(plan(0, 0, bd, tile_steps), caches[0], None))
    x2, big = _mlp(x1, mod3, w["w_up"], w["b_up"], w["w_down"], w["b_down"], w["ln2_g"], w["ln2_b"], gb, rb,
                   alpha, hosted=(plan(2, half, bd - half, tile_steps), caches[2], big))
    return x2, new_bufs, s_fin, (small, mid, big)


def _sample_layer(x, mod, caches, shifted, states, layer, w, alpha):
    b, s, d = x.shape
    mod3 = mod.reshape(b, 1, mod.shape[1])
    gb, rb = min(SAMPLE_SEQS, b), s
    rp = 2 * HEADS_PER_GROUP
    (qkv, hq, hk, hv, logf, og, gates), _ = _in_projection(x, mod3, w["w_in"], w["lb_param"], layer, gb, rb, F32)
    new_bufs = []
    running = None
    for gi in range(N_GROUPS):
        new_kv = _kv_rows(qkv, b, s, gi, s).reshape(b, s * rp, HEAD_DIM)
        buf, running = _attention_sample(qkv, new_kv, caches[gi], shifted[gi], layer, running, s, gi)
        new_bufs.append(buf)
    hg, s_fin = _hgrn_sample(hq, hk, hv, logf, og, w["hg_norm_w"], states[layer], s)
    x1, _ = _merge(x, mod3, running, hg, gates, w["w_branch_a"], w["w_branch_b"], w["w_out"],
                   w["ln1_g"], w["ln1_b"], gb, rb, alpha)
    x2, _ = _mlp(x1, mod3, w["w_up"], w["b_up"], w["w_down"], w["b_down"], w["ln2_g"], w["ln2_b"], gb, rb, alpha)
    return x2, new_bufs, s_fin


def _stack_layers(per_layer):
    return per_layer[0][None] if len(per_layer) == 1 else jnp.stack(per_layer)


def kernel(x_prompt, x_sample, c_prompt, c_sample, cache_kv_w128, cache_kv_w512, cache_kv_w2048, state_hgrn,
           w_ada, b_ada, w_in, lb_param, hg_norm_w, w_branch_a, w_branch_b, w_out, ln1_g, ln1_b, w_up, b_up,
           w_down, b_down, ln2_g, ln2_b):
    depth = w_ada.shape[0]
    alpha = (2 * depth) ** 0.25
    caches = (cache_kv_w128, cache_kv_w512, cache_kv_w2048)
    for (window, dil), cache in zip(ATTN_GROUPS, caches):
        assert window // dil == BAND and cache.shape[2] == window
    assert x_prompt.shape[1] % (BAND * ATTN_GROUPS[-1][1]) == 0 and x_sample.shape[1] == SUBLANES
    views = tuple(c.reshape(c.shape[0], c.shape[1], c.shape[2] * c.shape[3] * c.shape[4], c.shape[5])
                  for c in caches)
    nb = c_prompt.shape[0]
    yp, ys = x_prompt, x_sample
    p_bufs, p_states, s_bufs, s_states = [], [], [], []
    for l in range(depth):
        w = dict(w_in=w_in[l].astype(BF16), lb_param=lb_param, hg_norm_w=hg_norm_w[l],
                 w_branch_a=w_branch_a[l].astype(BF16), w_branch_b=w_branch_b[l].astype(BF16),
                 w_out=w_out[l].astype(BF16), ln1_g=ln1_g[l], ln1_b=ln1_b[l],
                 w_up=w_up[l].astype(BF16), b_up=b_up[l], w_down=w_down[l].astype(BF16), b_down=b_down[l],
                 ln2_g=ln2_g[l], ln2_b=ln2_b[l])
        mod = _modulation(jnp.concatenate([c_prompt, c_sample], axis=0), w_ada[l].astype(BF16), b_ada[l])
        yp, bufs_p, st_p, shifted = _prompt_layer(yp, mod[:nb], views, x_sample.shape[1], l, w, alpha)
        ys, bufs_s, st_s = _sample_layer(ys, mod[nb:], views, shifted, state_hgrn, l, w, alpha)
        p_bufs.append(bufs_p)
        p_states.append(st_p)
        s_bufs.append([buf.reshape(c.shape[1:]) for buf, c in zip(bufs_s, caches)])
        s_states.append(st_s)
    group = lambda bufs, gi: _stack_layers([bl[gi] for bl in bufs])
    return (yp, ys, group(p_bufs, 0), group(p_bufs, 1), group(p_bufs, 2), _stack_layers(p_states),
            group(s_bufs, 0), group(s_bufs, 1), group(s_bufs, 2), _stack_layers(s_states))
```

```python
import functools
import math
from typing import NamedTuple

import jax
import jax.numpy as jnp
from jax import lax
from jax.experimental import pallas as pl
from jax.experimental.pallas import tpu as pltpu

F32 = jnp.float32
BF16 = jnp.bfloat16

ATTN_GROUPS = ((128, 1), (512, 4), (2048, 16))
N_GROUPS = len(ATTN_GROUPS)
HEADS_PER_GROUP = 4
HEAD_DIM = 128
N_ATTN_HEADS = N_GROUPS * HEADS_PER_GROUP
ATTN_W = N_ATTN_HEADS * HEAD_DIM
GROUP_W = HEADS_PER_GROUP * HEAD_DIM
BAND = 128
HG_DK = 128
HG_DV = 128
LN_EPS = 1e-5
RMS_EPS = 1e-6
NEG = -1e30

SUBLANES = 8
LANES = 128
VMEM_LIMIT_BYTES = 56 * 1024 * 1024
COL_CHUNK = 512

NT_DIMS = (((1,), (1,)), ((), ()))
TN_DIMS = (((0,), (0,)), ((), ()))


def _alibi_slope(group, head):
    return 2.0 ** (-8.0 * (group * HEADS_PER_GROUP + head + 1) / N_ATTN_HEADS)


def _cparams(*sem):
    return pltpu.CompilerParams(dimension_semantics=sem, vmem_limit_bytes=VMEM_LIMIT_BYTES)


def _resident(shape):
    nd = len(shape)
    return pl.BlockSpec(shape, lambda *_: (0,) * nd, pipeline_mode=pl.Buffered(1))


def _sigmoid(x):
    return 1.0 / (1.0 + jnp.exp(-x))


def _layer_norm(h, g, b):
    mu = jnp.mean(h, axis=-1, keepdims=True)
    hc = h - mu
    var = jnp.mean(hc * hc, axis=-1, keepdims=True)
    return hc * lax.rsqrt(var + LN_EPS) * g + b


class _Shift(NamedTuple):
    layer: int
    b0: int
    nbatch: int
    cb: int
    rsplit: int
    k: int
    drop: int
    keep: int

    @property
    def n_chunks(self):
        return self.nbatch // self.cb * self.rsplit // self.k


def _plan_shift(layer, b0, nbatch, n_steps, drop, keep):
    if nbatch >= n_steps:
        cb = -(-nbatch // n_steps)
        while nbatch % cb:
            cb += 1
        return _Shift(layer, b0, nbatch, cb, 1, 1, drop, keep)
    best = None
    for r in (1, 2, 4, 8):
        if keep % (r * SUBLANES):
            continue
        k = -(-nbatch * r // n_steps)
        if nbatch * r % k == 0 and (best is None or k * best[0] < best[1] * r):
            best = (r, k)
    return _Shift(layer, b0, nbatch, 1, best[0], best[1], drop, keep)


def _shift_copies(shift, src_ref, dst_ref, buf, in_sem, out_sem, chunk, fetch):
    rows = shift.keep // shift.rsplit
    slot = chunk % 2
    copies = []
    for j in range(shift.k):
        part = chunk * shift.k + j
        b = shift.b0 + (part // shift.rsplit) * shift.cb
        r = (part % shift.rsplit) * rows
        if fetch:
            copies.append(pltpu.make_async_copy(
                src_ref.at[shift.layer, pl.ds(b, shift.cb), pl.ds(shift.drop + r, rows), :],
                buf.at[slot, j], in_sem.at[slot, j]))
        else:
            copies.append(pltpu.make_async_copy(
                buf.at[slot, j], dst_ref.at[pl.ds(b, shift.cb), pl.ds(r, rows), :], out_sem.at[slot, j]))
    return copies


def _host_shift(kernel_fn, shift, n_in, n_out, aliased, grid):
    n_steps = math.prod(grid)
    n_chunks = shift.n_chunks
    assert n_chunks <= n_steps

    def hosted(*refs):
        src = refs[n_in]
        pos = n_in + (2 if aliased else 1)
        outs = refs[pos:pos + n_out]
        dst = refs[pos + n_out]
        scratch = refs[pos + n_out + 1:-3]
        buf, in_sem, out_sem = refs[-3:]
        copies = functools.partial(_shift_copies, shift, src, dst, buf, in_sem, out_sem)

        def start(chunk, fetch):
            for copy in copies(chunk, fetch):
                copy.start()

        def wait(chunk, fetch):
            for copy in copies(chunk, fetch):
                copy.wait()

        step = 0
        for axis in range(len(grid)):
            step = step * grid[axis] + pl.program_id(axis)

        @pl.when((step >= 1) & (step <= n_chunks))
        def _():
            wait(step - 1, True)
            start(step - 1, False)

        @pl.when((step >= 2) & (step <= n_chunks + 1))
        def _():
            wait(step - 2, False)

        @pl.when(step < n_chunks)
        def _():
            start(step, True)

        kernel_fn(*refs[:n_in], *outs, *scratch)

        @pl.when(step == n_steps - 1)
        def _():
            if n_chunks == n_steps:
                wait(n_steps - 1, True)
                start(n_steps - 1, False)
            for c in range(max(n_steps - 2, 0), n_chunks):
                wait(c, False)

    return hosted


def _hosting_call(kernel_fn, shift, cache, dst, *, out_shape, grid, in_specs, out_specs, args,
                  scratch_shapes=(), **kwargs):
    if shift is None:
        res = pl.pallas_call(kernel_fn, out_shape=tuple(out_shape), grid=grid, in_specs=list(in_specs),
                             out_specs=tuple(out_specs), scratch_shapes=list(scratch_shapes), **kwargs)(*args)
        return tuple(res), dst
    n_in, n_out = len(in_specs), len(out_shape)
    aliased = dst is not None
    anywhere = pl.BlockSpec(memory_space=pl.ANY)
    res = pl.pallas_call(
        _host_shift(kernel_fn, shift, n_in, n_out, aliased, grid),
        out_shape=tuple(out_shape) + (jax.ShapeDtypeStruct(cache.shape[1:], cache.dtype),),
        grid=grid,
        in_specs=list(in_specs) + [anywhere] * (2 if aliased else 1),
        out_specs=tuple(out_specs) + (anywhere,),
        scratch_shapes=list(scratch_shapes) + [
            pltpu.VMEM((2, shift.k, shift.cb, shift.keep // shift.rsplit, cache.shape[3]), cache.dtype),
            pltpu.SemaphoreType.DMA((2, shift.k)), pltpu.SemaphoreType.DMA((2, shift.k))],
        input_output_aliases={n_in + 1: n_out} if aliased else {},
        **kwargs,
    )(*args, cache, *([dst] if aliased else []))
    return tuple(res[:-1]), res[-1]


def _mod_kernel(c_ref, w_ref, b_ref, o_ref):
    c = c_ref[...]
    a = (c * _sigmoid(c)).astype(BF16)
    o_ref[...] = jnp.dot(a, w_ref[...], preferred_element_type=F32) + b_ref[...]


def _modulation(c, w_ada, b_ada):
    n, d = c.shape
    nout = w_ada.shape[1]
    tn = 1024
    return pl.pallas_call(
        _mod_kernel,
        out_shape=jax.ShapeDtypeStruct((n, nout), F32),
        grid=(nout // tn,),
        in_specs=[pl.BlockSpec((n, d), lambda j: (0, 0)),
                  pl.BlockSpec((d, tn), lambda j: (0, j)),
                  pl.BlockSpec((1, tn), lambda j: (0, j))],
        out_specs=pl.BlockSpec((n, tn), lambda j: (0, j)),
        compiler_params=_cparams("arbitrary"),
        name="modulation",
    )(c, w_ada, b_ada.reshape(1, nout))


def _store_heads(ref, c0, y):
    for j in range(y.shape[1] // HG_DK):
        ref[c0 // HG_DK + j] = y[:, j * HG_DK:(j + 1) * HG_DK].astype(ref.dtype)


def _inproj_kernel(x_ref, mod_ref, w_ref, lbp_ref, attn_ref, hq_ref, hk_ref, hv_ref, logf_ref, og_ref,
                   gates_ref, *, layer):
    gb, rb, d = x_ref.shape
    tm = gb * rb
    hg_w = hq_ref.shape[0] * HG_DK
    act = attn_ref.dtype
    sh = mod_ref[:, :, 0:d]
    sc = mod_ref[:, :, d:2 * d]
    u = (x_ref[...] * (1.0 + sc) + sh).reshape(tm, d).astype(BF16)

    def proj(c0):
        return jnp.dot(u, w_ref[:, c0:c0 + COL_CHUNK], preferred_element_type=F32)

    for c0 in range(0, 3 * ATTN_W, COL_CHUNK):
        attn_ref[:, c0:c0 + COL_CHUNK] = proj(c0).astype(act)
    base = 3 * ATTN_W
    for c0 in range(0, hg_w, COL_CHUNK):
        y = proj(base + c0)
        _store_heads(hq_ref, c0, y * _sigmoid(y))
    lbp = lbp_ref[...]
    e = jnp.exp(lbp - jnp.max(lbp, axis=0, keepdims=True))
    lb = jnp.sum(e[0:layer + 1], axis=0, keepdims=True) / jnp.sum(e, axis=0, keepdims=True)
    base += hg_w
    for c0 in range(0, hg_w, COL_CHUNK):
        lbc = lb[:, c0:c0 + COL_CHUNK]
        f = lbc + (1.0 - lbc) * _sigmoid(proj(base + c0))
        _store_heads(logf_ref, c0, jnp.log(f))
        _store_heads(hk_ref, c0, 1.0 - f)
    base += hg_w
    for c0 in range(0, hg_w, COL_CHUNK):
        _store_heads(hv_ref, c0, proj(base + c0))
    base += hg_w
    for c0 in range(0, hg_w, COL_CHUNK):
        y = proj(base + c0)
        _store_heads(og_ref, c0, y * _sigmoid(y))
    base += hg_w
    for c0 in range(0, 2 * d, COL_CHUNK):
        gates_ref[:, c0:c0 + COL_CHUNK] = _sigmoid(proj(base + c0)).astype(act)


def _in_projection(x3, mod3, w_in, lb_param, layer, gb, rb, act, hosted=(None, None, None)):
    g, r, d = x3.shape
    n = g * r
    tm = gb * rb
    hg_w = lb_param.shape[1]
    rt = r // rb
    heads = hg_w // HG_DK
    rows = lambda width: pl.BlockSpec((tm, width), lambda gi, ri: (gi * rt + ri, 0))
    by_head = pl.BlockSpec((heads, tm, HG_DK), lambda gi, ri: (0, gi * rt + ri, 0))
    head_major = lambda dtype: jax.ShapeDtypeStruct((heads, n, HG_DK), dtype)
    out_shapes = (
        jax.ShapeDtypeStruct((n, 3 * ATTN_W), act),
        head_major(act),
        head_major(act),
        head_major(act),
        head_major(F32),
        head_major(act),
        jax.ShapeDtypeStruct((n, 2 * d), act),
    )
    return _hosting_call(
        functools.partial(_inproj_kernel, layer=layer), *hosted,
        out_shape=out_shapes,
        grid=(g // gb, rt),
        in_specs=[pl.BlockSpec((gb, rb, d), lambda gi, ri: (gi, ri, 0)),
                  pl.BlockSpec((gb, 1, mod3.shape[2]), lambda gi, ri: (gi, 0, 0)),
                  _resident(w_in.shape),
                  _resident(lb_param.shape)],
        out_specs=(rows(3 * ATTN_W), by_head, by_head, by_head, by_head, by_head, rows(2 * d)),
        args=(x3, mod3, w_in, lb_param),
        compiler_params=_cparams("arbitrary", "arbitrary"),
        name="in_projection",
    )


ATTN_BLOCKS_PER_ITER = 4


def _attn_prompt_kernel(q_ref, k_ref, v_ref, attn_ref, qf, kf, vf, mf, lf, af, *, head_axis, group_axis):
    s = attn_ref.shape[0]
    nblk = s // BAND
    head = pl.program_id(head_axis)
    group = pl.program_id(group_axis)
    qi = lax.broadcasted_iota(jnp.int32, (BAND, BAND), 0)
    kj = lax.broadcasted_iota(jnp.int32, (BAND, BAND), 1)
    steps_own = qi - kj
    steps_prev = steps_own + BAND
    scale = HEAD_DIM ** -0.5
    ones = jnp.ones((BAND, HEAD_DIM), BF16)
    qf[...] = q_ref[...].astype(F32)
    kf[...] = k_ref[...].astype(F32)
    vf[...] = v_ref[...].astype(F32)
    for g, (window, dil) in enumerate(ATTN_GROUPS):
        nb = nblk // dil
        slope = jnp.float32(_alibi_slope(g, HEADS_PER_GROUP - 1) * dil)
        for h in range(HEADS_PER_GROUP - 1):
            slope = jnp.where(head == h, jnp.float32(_alibi_slope(g, h) * dil), slope)
        bias_own = jnp.where(steps_own >= 0, -slope * steps_own.astype(F32), NEG)
        bias_prev = jnp.where(steps_prev <= BAND, -slope * steps_prev.astype(F32), NEG)

        def body(it, carry, g=g, dil=dil, nb=nb, bias_own=bias_own, bias_prev=bias_prev):
            blocks = []
            for u in range(ATTN_BLOCKS_PER_ITER):
                idx = it * ATTN_BLOCKS_PER_ITER + u
                r = idx // nb
                i = idx - r * nb
                start = r + i * (BAND * dil)
                start_prev = r + jnp.maximum(i - 1, 0) * (BAND * dil)
                rows = pl.ds(start, BAND, stride=dil) if dil > 1 else pl.ds(start, BAND)
                rows_prev = pl.ds(start_prev, BAND, stride=dil) if dil > 1 else pl.ds(start_prev, BAND)
                blocks.append((rows, rows_prev, i > 0))
            scores = []
            for rows, rows_prev, has_prev in blocks:
                q = qf[rows, :].astype(BF16)
                scores.append((lax.dot_general(q, kf[rows, :].astype(BF16), NT_DIMS, preferred_element_type=F32),
                               lax.dot_general(q, kf[rows_prev, :].astype(BF16), NT_DIMS,
                                               preferred_element_type=F32)))
            probs = []
            for (rows, rows_prev, has_prev), (s_own, s_prev) in zip(blocks, scores):
                s_own = s_own * scale + bias_own
                s_prev = jnp.where(has_prev, s_prev * scale + bias_prev, NEG)
                m = jnp.max(jnp.maximum(s_own, s_prev), axis=-1, keepdims=True)
                probs.append((m, jnp.exp(s_own - m).astype(BF16), jnp.exp(s_prev - m).astype(BF16)))
            results = []
            for (rows, rows_prev, has_prev), (m, p_own, p_prev) in zip(blocks, probs):
                both = (jnp.dot(p_own, jnp.concatenate([vf[rows, :].astype(BF16), ones], axis=1),
                                preferred_element_type=F32)
                        + jnp.dot(p_prev, jnp.concatenate([vf[rows_prev, :].astype(BF16), ones], axis=1),
                                  preferred_element_type=F32))
                acc, l = both[:, :HEAD_DIM], both[:, HEAD_DIM:]
                if g > 0:
                    m_old = mf[rows, :]
                    m_new = jnp.maximum(m_old, m)
                    a_old = jnp.exp(m_old - m_new)
                    a_blk = jnp.exp(m - m_new)
                    l = a_old * lf[rows, :] + a_blk * l
                    acc = a_old * af[rows, :] + a_blk * acc
                    m = m_new
                results.append((rows, m, l, acc))
            for rows, m, l, acc in results:
                mf[rows, :] = jnp.broadcast_to(m, (BAND, HEAD_DIM))
                lf[rows, :] = l
                af[rows, :] = acc
            return carry

        @pl.when(group == g)
        def _(body=body):
            lax.fori_loop(0, nblk // ATTN_BLOCKS_PER_ITER, body, 0)

    @pl.when(group == N_GROUPS - 1)
    def _():
        attn_ref[...] = (af[...] / lf[...]).astype(attn_ref.dtype)


def _attention_prompt(qkv, b, s, hosted=(None, None, None)):
    hpq = ATTN_W // HEAD_DIM
    in_specs = [pl.BlockSpec((s, HEAD_DIM), lambda bi, h, g, section=section:
                             (bi, section * hpq + g * HEADS_PER_GROUP + h)) for section in range(3)]
    (attn,), dst = _hosting_call(
        functools.partial(_attn_prompt_kernel, head_axis=1, group_axis=2), *hosted,
        out_shape=(jax.ShapeDtypeStruct((b * s, GROUP_W), BF16),),
        grid=(b, HEADS_PER_GROUP, N_GROUPS),
        in_specs=in_specs,
        out_specs=(pl.BlockSpec((s, HEAD_DIM), lambda bi, h, g: (bi, h)),),
        scratch_shapes=[pltpu.VMEM((s, HEAD_DIM), F32) for _ in range(6)],
        args=[qkv] * 3,
        compiler_params=_cparams("arbitrary", "arbitrary", "arbitrary"),
        name="attn_prompt",
    )
    return attn, dst


def _attn_sample_kernel(*refs, group, dil, n_steps, length, seqs):
    first = group == 0
    last = group == N_GROUPS - 1
    q_ref, newkv_ref, cache_ref = refs[:3]
    pos = 3
    if not first:
        m_in, l_in, a_in = refs[pos:pos + 3]
        pos += 3
    pos += 1
    newc_ref = refs[pos]
    outs = refs[pos + 1:]
    t = q_ref.shape[0] // seqs
    rp = 2 * HEADS_PER_GROUP
    stretch = max(dil, t)
    nkeys = length // stretch * t
    j_c = lax.broadcasted_iota(jnp.int32, (t, nkeys), 0)
    c_c = lax.broadcasted_iota(jnp.int32, (t, nkeys), 1)
    dist_c = length + j_c - ((c_c // t) * stretch + (c_c & (t - 1)))
    ok_c = ((dist_c & (dil - 1)) == 0) & (dist_c <= n_steps * dil)
    j_n = lax.broadcasted_iota(jnp.int32, (t, t), 0)
    r_n = lax.broadcasted_iota(jnp.int32, (t, t), 1)
    dist_n = j_n - r_n
    ok_n = (dist_n >= 0) & ((dist_n & (dil - 1)) == 0) & (dist_n <= n_steps * dil)
    scale = HEAD_DIM ** -0.5
    qoff = group * GROUP_W
    cache_rows = lambda s, first_row: cache_ref[s, :, pl.ds(first_row, t, stride=rp), :].reshape(
        nkeys, HEAD_DIM).astype(BF16)
    new_rows = lambda s, first_row: newkv_ref[s, pl.ds(first_row, t, stride=rp), :].astype(BF16)
    chains = [(s, h) for s in range(seqs) for h in range(HEADS_PER_GROUP)]
    scores = []
    for s, h in chains:
        q = q_ref[s * t:(s + 1) * t, qoff + h * HEAD_DIM:qoff + (h + 1) * HEAD_DIM].astype(BF16)
        scores.append((lax.dot_general(q, cache_rows(s, h), NT_DIMS, preferred_element_type=F32),
                       lax.dot_general(q, new_rows(s, h), NT_DIMS, preferred_element_type=F32)))
    probs = []
    for (s, h), (s_c, s_n) in zip(chains, scores):
        rows, sl = slice(s * t, (s + 1) * t), slice(h * HEAD_DIM, (h + 1) * HEAD_DIM)
        slope = _alibi_slope(group, h)
        s_c = jnp.where(ok_c, s_c * scale - slope * dist_c.astype(F32), NEG)
        s_n = jnp.where(ok_n, s_n * scale - slope * dist_n.astype(F32), NEG)
        m = jnp.maximum(jnp.max(s_c, axis=-1, keepdims=True), jnp.max(s_n, axis=-1, keepdims=True))
        if not first:
            m = jnp.maximum(m, m_in[rows, sl][:, 0:1])
        p_c = jnp.exp(s_c - m)
        p_n = jnp.exp(s_n - m)
        l = jnp.sum(p_c, axis=-1, keepdims=True) + jnp.sum(p_n, axis=-1, keepdims=True)
        probs.append((m, l, p_c.astype(BF16), p_n.astype(BF16)))
    for (s, h), (m, l, p_c, p_n) in zip(chains, probs):
        rows, sl = slice(s * t, (s + 1) * t), slice(h * HEAD_DIM, (h + 1) * HEAD_DIM)
        acc = (jnp.dot(p_c, cache_rows(s, HEADS_PER_GROUP + h), preferred_element_type=F32)
               + jnp.dot(p_n, new_rows(s, HEADS_PER_GROUP + h), preferred_element_type=F32))
        if not first:
            a_old = jnp.exp(m_in[rows, sl] - m)
            l = a_old * l_in[rows, sl] + l
            acc = a_old * a_in[rows, sl] + acc
        if last:
            outs[0][rows, sl] = acc / l
        else:
            outs[0][rows, sl] = jnp.broadcast_to(m, (t, HEAD_DIM))
            outs[1][rows, sl] = jnp.broadcast_to(l, (t, HEAD_DIM))
            outs[2][rows, sl] = acc
    for s in range(seqs):
        newc_ref[s] = newkv_ref[s]


SAMPLE_SEQS_PER_STEP = (8, 4, 2)


def _attention_sample(qkv, new_kv, cache, shifted, layer, running, t, group):
    window, dil = ATTN_GROUPS[group]
    depth, bd, crow, dh = cache.shape
    rp = 2 * HEADS_PER_GROUP
    length = crow // rp
    stretch = max(dil, t)
    seqs = math.gcd(SAMPLE_SEQS_PER_STEP[group], bd)
    first = group == 0
    last = group == N_GROUPS - 1
    rows = pl.BlockSpec((seqs * t, GROUP_W), lambda bi: (bi, 0))
    stat = jax.ShapeDtypeStruct((bd * t, GROUP_W), F32)
    new_blk = pl.BlockSpec((seqs, t * rp, dh), lambda bi: (bi, 0, 0))
    in_specs = [pl.BlockSpec((seqs * t, 3 * ATTN_W), lambda bi: (bi, 0)),
                new_blk,
                pl.BlockSpec((None, seqs, length // stretch, t * rp, dh), lambda bi: (layer, bi, 0, 0, 0))]
    args = [qkv, new_kv, cache.reshape(depth, bd, length // stretch, stretch * rp, dh)]
    if not first:
        in_specs += [rows, rows, rows]
        args += list(running)
    in_specs.append(pl.BlockSpec(memory_space=pl.ANY))
    args.append(shifted)
    n_out = 1 if last else 3
    res = pl.pallas_call(
        functools.partial(_attn_sample_kernel, group=group, dil=dil, n_steps=window // dil, length=length,
                          seqs=seqs),
        out_shape=(jax.ShapeDtypeStruct(shifted.shape, shifted.dtype),) + (stat,) * n_out,
        grid=(bd // seqs,),
        in_specs=in_specs,
        out_specs=(pl.BlockSpec((seqs, t * rp, dh), lambda bi: (bi, length // t - 1, 0)),) + (rows,) * n_out,
        input_output_aliases={len(args) - 1: 0},
        compiler_params=_cparams("arbitrary"),
        name=f"attn_sample_g{group}",
    )(*args)
    return res[0], (res[1] if last else tuple(res[1:]))


def _cumsum_rows(x, row):
    d = 1
    while d < x.shape[0]:
        x = x + jnp.where(row >= d, pltpu.roll(x, d, 0), 0.0)
        d *= 2
    return x


def _hgrn_carry(q, k, v, g, st):
    c = q.shape[0]
    qg = (q * jnp.exp(g)).astype(BF16)
    o = lax.dot_general(qg, st.astype(BF16), NT_DIMS, preferred_element_type=F32)
    g_last = g[c - 1:c, :]
    kd = (k * jnp.exp(g_last - g)).astype(BF16)
    st_new = st * jnp.exp(g_last) + lax.dot_general(v.astype(BF16), kd, TN_DIMS, preferred_element_type=F32)
    return o, st_new


def _hgrn_chunks_pairwise(qs, ks, vs, lfs, states):
    c = qs[0].shape[0]
    row = lax.broadcasted_iota(jnp.int32, qs[0].shape, 0)
    gs = [_cumsum_rows(lf, row) for lf in lfs]
    outs = []
    for q, k, v, g in zip(qs, ks, vs, gs):
        o = jnp.sum(q * k, axis=-1, keepdims=True) * v
        for d in range(1, c):
            ok = row >= d
            e = jnp.exp(jnp.where(ok, g - pltpu.roll(g, d, 0), 0.0))
            a = jnp.sum(jnp.where(ok, q * pltpu.roll(k, d, 0) * e, 0.0), axis=-1, keepdims=True)
            o = o + a * pltpu.roll(v, d, 0)
        outs.append(o)
    cols = []
    for k, g in zip(ks, gs):
        g_last = g[c - 1:c, :]
        kd = k * jnp.exp(g_last - g)
        cols.append(jnp.concatenate([kd, jnp.broadcast_to(jnp.exp(g_last), kd.shape)], axis=0).T)
    carried = [jnp.dot((q * jnp.exp(g)).astype(BF16), s.astype(BF16), preferred_element_type=F32)
               for q, g, s in zip(qs, gs, states)]
    new_states = [s * col[:, c:c + 1] + jnp.dot(col[:, 0:c].astype(BF16), v.astype(BF16),
                                                preferred_element_type=F32)
                  for s, col, v in zip(states, cols, vs)]
    return [o + oc for o, oc in zip(outs, carried)], new_states


def _pair_level(c):
    ri = lax.broadcasted_iota(jnp.int32, (c, c), 0)
    ci = lax.broadcasted_iota(jnp.int32, (c, c), 1)
    x = ri ^ ci
    bits = jnp.zeros((c, c), jnp.int32)
    p = 1
    while p < c:
        bits = bits + (x >= p).astype(jnp.int32)
        p *= 2
    return jnp.where(ri > ci, bits + 1, jnp.where(ri == ci, 1, 0))


def _hgrn_chunks_blocked(qs, ks, vs, lfs, sts, g_refs, level):
    nh = len(qs)
    c = qs[0].shape[0]
    row = lax.broadcasted_iota(jnp.int32, (c, HG_DK), 0)
    sub = row & (SUBLANES - 1)
    sub8 = lax.broadcasted_iota(jnp.int32, (SUBLANES, HG_DK), 0)
    gs = []
    for lf, g_ref in zip(lfs, g_refs):
        x = lf
        for d in (1, 2, 4):
            x = x + jnp.where(sub >= d, pltpu.roll(x, d, 0), 0.0)
        g_ref[...] = x
        pieces = [x[0:SUBLANES]]
        off = None
        for j in range(1, c // SUBLANES):
            tot = jnp.broadcast_to(g_ref[j * SUBLANES - 1:j * SUBLANES, :], (SUBLANES, HG_DK))
            off = tot if off is None else off + tot
            pieces.append(x[j * SUBLANES:(j + 1) * SUBLANES] + off)
        g = jnp.concatenate(pieces, axis=0)
        g_ref[...] = g
        gs.append(g)

    def mid_rows(g_ref, n):
        half = n // 2
        bcast = lambda r: jnp.broadcast_to(g_ref[r:r + 1, :], (SUBLANES, HG_DK))
        out = []
        for j in range(c // SUBLANES):
            base = j * SUBLANES
            if n >= SUBLANES:
                out.append(bcast(base // n * n + half - 1))
            else:
                assert n == 4
                out.append(jnp.where(sub8 < 4, bcast(base + 1), bcast(base + 5)))
        return jnp.concatenate(out, axis=0)

    pair = lambda y, z: lax.dot_general(y, z, NT_DIMS, preferred_element_type=F32)
    prods = [[pair(q.astype(BF16), k.astype(BF16))] for q, k in zip(qs, ks)]
    n = 2
    while n <= c:
        second = (row & (n // 2)) != 0
        for u in range(nh):
            if n == 2:
                arg = jnp.where(second, lfs[u], 0.0)
            else:
                gmid = mid_rows(g_refs[u], n)
                arg = jnp.where(second, gs[u] - gmid, gmid - gs[u])
            y = (jnp.where(second, qs[u], ks[u]) * jnp.exp(arg)).astype(BF16)
            prods[u].append(pair(y, y))
        n *= 2
    outs = []
    for u in range(nh):
        amat = jnp.where(level == 1, prods[u][0], 0.0)
        for j, p in enumerate(prods[u][1:]):
            amat = jnp.where(level == j + 2, p, amat)
        outs.append(jnp.dot(amat.astype(BF16), vs[u].astype(BF16), preferred_element_type=F32))
    states = []
    for u in range(nh):
        o_carry, st_new = _hgrn_carry(qs[u], ks[u], vs[u], gs[u], sts[u])
        outs[u] = outs[u] + o_carry
        states.append(st_new)
    return outs, states


def _hgrn_output(o, og, nw):
    return o * lax.rsqrt(jnp.mean(o * o, axis=-1, keepdims=True) + RMS_EPS) * nw * og


HGRN_HEADS_PER_ITER = 4


def _hgrn_prompt_kernel(q_ref, k_ref, v_ref, lf_ref, og_ref, nw_ref, hg_ref, sfin_ref, st_ref, g_ref):
    ci = pl.program_id(1)
    heads, c, _ = q_ref.shape

    @pl.when(ci == 0)
    def _():
        st_ref[...] = jnp.zeros_like(st_ref)

    level = _pair_level(c)

    def head_group(i, carry):
        hs = [i * HGRN_HEADS_PER_ITER + u for u in range(HGRN_HEADS_PER_ITER)]
        load = lambda ref: [ref[h].astype(F32) for h in hs]
        outs, states = _hgrn_chunks_blocked(load(q_ref), load(k_ref), load(v_ref), load(lf_ref), load(st_ref),
                                            [g_ref.at[u] for u in range(HGRN_HEADS_PER_ITER)], level)
        for h, o, st_new in zip(hs, outs, states):
            st_ref[h] = st_new
            hg_ref[h] = _hgrn_output(o, og_ref[h].astype(F32), nw_ref[...]).astype(hg_ref.dtype)
        return carry

    lax.fori_loop(0, heads // HGRN_HEADS_PER_ITER, head_group, 0)

    @pl.when(ci == pl.num_programs(1) - 1)
    def _():
        for h in range(heads):
            sfin_ref[0, h] = st_ref[h].T


def _hgrn_prompt(hq, hk, hv, logf, og, norm_w, b, s, chunk, hosted=(None, None, None)):
    heads, n, _ = hq.shape
    nc = s // chunk
    blk = pl.BlockSpec((heads, chunk, HG_DK), lambda bi, ci: (0, bi * nc + ci, 0))
    (hg, s_fin), dst = _hosting_call(
        _hgrn_prompt_kernel, *hosted,
        out_shape=(jax.ShapeDtypeStruct((heads, n, HG_DV), BF16),
                   jax.ShapeDtypeStruct((b, heads, HG_DK, HG_DV), F32)),
        grid=(b, nc),
        in_specs=[blk, blk, blk, blk, blk, pl.BlockSpec((1, HG_DV), lambda bi, ci: (0, 0))],
        out_specs=(blk, pl.BlockSpec((1, heads, HG_DK, HG_DV), lambda bi, ci: (bi, 0, 0, 0))),
        scratch_shapes=[pltpu.VMEM((heads, HG_DV, HG_DK), F32),
                        pltpu.VMEM((HGRN_HEADS_PER_ITER, chunk, HG_DK), F32)],
        args=(hq, hk, hv, logf, og, norm_w.reshape(1, HG_DV)),
        compiler_params=_cparams("arbitrary", "arbitrary"),
        name="hgrn_prompt",
    )
    return hg, s_fin, dst


HGRN_SAMPLE_SEQS_PER_STEP = 2


def _hgrn_sample_kernel(q_ref, k_ref, v_ref, lf_ref, og_ref, nw_ref, s_ref, hg_ref, snew_ref):
    seqs, heads = s_ref.shape[0], s_ref.shape[1]
    t = q_ref.shape[1] // seqs
    chains = [(s, h) for s in range(seqs) for h in range(heads)]
    load = lambda ref: [ref[h, s * t:(s + 1) * t, :] for s, h in chains]
    outs, new_states = _hgrn_chunks_pairwise(load(q_ref), load(k_ref), load(v_ref), load(lf_ref),
                                             [s_ref[s, h] for s, h in chains])
    for (s, h), o, st_new in zip(chains, outs, new_states):
        snew_ref[s, h] = st_new
        hg_ref[h, s * t:(s + 1) * t, :] = _hgrn_output(o, og_ref[h, s * t:(s + 1) * t, :], nw_ref[...])


def _hgrn_sample(hq, hk, hv, logf, og, norm_w, state, t):
    bd, heads = state.shape[0], state.shape[1]
    seqs = math.gcd(HGRN_SAMPLE_SEQS_PER_STEP, bd)
    blk = pl.BlockSpec((heads, seqs * t, HG_DK), lambda bi: (0, bi, 0))
    sblk = pl.BlockSpec((seqs, heads, HG_DK, HG_DV), lambda bi: (bi, 0, 0, 0))
    return pl.pallas_call(
        _hgrn_sample_kernel,
        out_shape=(jax.ShapeDtypeStruct((heads, bd * t, HG_DV), F32),
                   jax.ShapeDtypeStruct(state.shape, state.dtype)),
        grid=(bd // seqs,),
        in_specs=[blk, blk, blk, blk, blk, pl.BlockSpec((1, HG_DV), lambda bi: (0, 0)), sblk],
        out_specs=(blk, sblk),
        compiler_params=_cparams("arbitrary"),
        name="hgrn_sample",
    )(hq, hk, hv, logf, og, norm_w.reshape(1, HG_DV), state)


def _merge_kernel(x_ref, mod_ref, attn_ref, hg_ref, gates_ref, wa_ref, wb_ref, wo_ref, lng_ref, lnb_ref,
                  x1_ref, *, alpha):
    gb, rb, d = x_ref.shape
    tm = gb * rb
    hg = jnp.concatenate([hg_ref[h].astype(BF16) for h in range(hg_ref.shape[0])], axis=1)
    branch_a = jnp.dot(attn_ref[...].astype(BF16), wa_ref[...], preferred_element_type=F32)
    branch_b = jnp.dot(hg, wb_ref[...], preferred_element_type=F32)
    gate_a = gates_ref[:, 0:d].astype(F32)
    gate_b = gates_ref[:, d:2 * d].astype(F32)
    merged = gate_a * branch_a + gate_b * branch_b
    mix = jnp.dot(merged.astype(BF16), wo_ref[...], preferred_element_type=F32)
    g1 = mod_ref[:, :, 2 * d:3 * d]
    h = alpha * x_ref[...] + g1 * mix.reshape(gb, rb, d)
    x1_ref[...] = _layer_norm(h, lng_ref[...], lnb_ref[...])


def _merge(x3, mod3, attn, hg, gates, wa, wb, wo, ln_g, ln_b, gb, rb, alpha, hosted=(None, None, None)):
    g, r, d = x3.shape
    tm = gb * rb
    rt = r // rb
    row = lambda gi, ri: (gi * rt + ri, 0)
    rows = lambda a: pl.BlockSpec((tm, a.shape[1]), row)
    tile3 = pl.BlockSpec((gb, rb, d), lambda gi, ri: (gi, ri, 0))
    (x1,), dst = _hosting_call(
        functools.partial(_merge_kernel, alpha=alpha), *hosted,
        out_shape=(jax.ShapeDtypeStruct(x3.shape, F32),),
        grid=(g // gb, rt),
        in_specs=[tile3, pl.BlockSpec((gb, 1, mod3.shape[2]), lambda gi, ri: (gi, 0, 0)),
                  rows(attn), pl.BlockSpec((hg.shape[0], tm, hg.shape[2]), lambda gi, ri: (0, gi * rt + ri, 0)),
                  rows(gates),
                  _resident(wa.shape), _resident(wb.shape), _resident(wo.shape),
                  _resident((1, d)), _resident((1, d))],
        out_specs=(tile3,),
        args=(x3, mod3, attn, hg, gates, wa, wb, wo, ln_g.reshape(1, d), ln_b.reshape(1, d)),
        compiler_params=_cparams("arbitrary", "arbitrary"),
        name="merge",
    )
    return x1, dst


def _mlp_kernel(x1_ref, mod_ref, wu_ref, bu_ref, wd_ref, bd_ref, lng_ref, lnb_ref, x2_ref, *, alpha):
    gb, rb, d = x1_ref.shape
    tm = gb * rb
    dff = wu_ref.shape[1]
    x1 = x1_ref[...]
    sh = mod_ref[:, :, 3 * d:4 * d]
    sc = mod_ref[:, :, 4 * d:5 * d]
    g2 = mod_ref[:, :, 5 * d:6 * d]
    u = (x1 * (1.0 + sc) + sh).reshape(tm, d).astype(BF16)
    acc = jnp.zeros((tm, d), F32)
    for c0 in range(0, dff, COL_CHUNK):
        hid = jnp.dot(u, wu_ref[:, c0:c0 + COL_CHUNK], preferred_element_type=F32) + bu_ref[:, c0:c0 + COL_CHUNK]
        hid = jnp.square(jnp.maximum(hid, 0.0))
        acc = acc + jnp.dot(hid.astype(BF16), wd_ref[c0:c0 + COL_CHUNK, :], preferred_element_type=F32)
    ff = acc + bd_ref[...]
    h = alpha * x1 + g2 * ff.reshape(gb, rb, d)
    x2_ref[...] = _layer_norm(h, lng_ref[...], lnb_ref[...])


def _mlp(x3, mod3, wu, bu, wd, bd, ln_g, ln_b, gb, rb, alpha, hosted=(None, None, None)):
    g, r, d = x3.shape
    dff = wu.shape[1]
    tile3 = pl.BlockSpec((gb, rb, d), lambda gi, ri: (gi, ri, 0))
    (x2,), dst = _hosting_call(
        functools.partial(_mlp_kernel, alpha=alpha), *hosted,
        out_shape=(jax.ShapeDtypeStruct(x3.shape, F32),),
        grid=(g // gb, r // rb),
        in_specs=[tile3, pl.BlockSpec((gb, 1, mod3.shape[2]), lambda gi, ri: (gi, 0, 0)),
                  _resident(wu.shape), _resident((1, dff)), _resident(wd.shape), _resident((1, d)),
                  _resident((1, d)), _resident((1, d))],
        out_specs=(tile3,),
        args=(x3, mod3, wu, bu.reshape(1, dff), wd, bd.reshape(1, d), ln_g.reshape(1, d), ln_b.reshape(1, d)),
        compiler_params=_cparams("arbitrary", "arbitrary"),
        name="mlp",
    )
    return x2, dst


PROMPT_ROWS = 256
SAMPLE_SEQS = 32
HGRN_CHUNK = 128


def _kv_rows(qkv, b, s, group, rows):
    q3 = qkv.reshape(b, s, 3 * ATTN_W)
    k = q3[:, s - rows:, ATTN_W + group * GROUP_W:ATTN_W + (group + 1) * GROUP_W]
    v = q3[:, s - rows:, 2 * ATTN_W + group * GROUP_W:2 * ATTN_W + (group + 1) * GROUP_W]
    return jnp.stack([k, v], axis=2).astype(F32).reshape(b, rows, 2, HEADS_PER_GROUP, HEAD_DIM)


def _prompt_layer(x, mod, caches, t_new, layer, w, alpha):
    b, s, d = x.shape
    mod3 = mod.reshape(b, 1, mod.shape[1])
    gb, rb = 1, min(PROMPT_ROWS, s)
    chunk = min(HGRN_CHUNK, s)
    rp = 2 * HEADS_PER_GROUP
    bd = caches[0].shape[1]
    tile_steps = s // rb * b
    hgrn_steps = s // chunk * b

    def plan(gi, b0, nbatch, n_steps):
        rows = caches[gi].shape[2]
        return _plan_shift(layer, b0, nbatch, n_steps, t_new * rp, rows - t_new * rp)

    attn_steps = b * HEADS_PER_GROUP * N_GROUPS
    n_hgrn = bd // 4
    n_attn = (bd - n_hgrn) // 2
    big = None

    def share(b0, nbatch, n_steps):
        return (plan(2, b0, nbatch, n_steps), caches[2], big) if nbatch else (None, None, big)

    (qkv, hq, hk, hv, logf, og, gates), mid = _in_projection(
        x, mod3, w["w_in"], w["lb_param"], layer, gb, rb, BF16,
        hosted=(plan(1, 0, bd, tile_steps), caches[1], None))
    hg, s_fin, big = _hgrn_prompt(hq, hk, hv, logf, og, w["hg_norm_w"], b, s, chunk,
                                  hosted=share(0, n_hgrn, hgrn_steps))
    attn, big = _attention_prompt(qkv, b, s, hosted=share(n_hgrn, n_attn, attn_steps))
    new_bufs = [_kv_rows(qkv, b, s, gi, min(window, s)) for gi, (window, dil) in enumerate(ATTN_GROUPS)]
    x1, small = _merge(x, mod3, attn, hg, gates, w["w_branch_a"], w["w_branch_b"], w["w_out"],
                       w["ln1_g"], w["ln1_b"], gb, rb, alpha, hosted=(plan(0, 0, bd, tile_steps), caches[0], None))
    x2, big = _mlp(x1, mod3, w["w_up"], w["b_up"], w["w_down"], w["b_down"], w["ln2_g"], w["ln2_b"], gb, rb,
                   alpha, hosted=share(n_hgrn + n_attn, bd - n_hgrn - n_attn, tile_steps))
    return x2, new_bufs, s_fin, (small, mid, big)


def _sample_layer(x, mod, caches, shifted, states, layer, w, alpha):
    b, s, d = x.shape
    mod3 = mod.reshape(b, 1, mod.shape[1])
    gb, rb = min(SAMPLE_SEQS, b), s
    rp = 2 * HEADS_PER_GROUP
    (qkv, hq, hk, hv, logf, og, gates), _ = _in_projection(x, mod3, w["w_in"], w["lb_param"], layer, gb, rb, F32)
    new_bufs = []
    running = None
    for gi in range(N_GROUPS):
        new_kv = _kv_rows(qkv, b, s, gi, s).reshape(b, s * rp, HEAD_DIM)
        buf, running = _attention_sample(qkv, new_kv, caches[gi], shifted[gi], layer, running, s, gi)
        new_bufs.append(buf)
    hg, s_fin = _hgrn_sample(hq, hk, hv, logf, og, w["hg_norm_w"], states[layer], s)
    x1, _ = _merge(x, mod3, running, hg, gates, w["w_branch_a"], w["w_branch_b"], w["w_out"],
                   w["ln1_g"], w["ln1_b"], gb, rb, alpha)
    x2, _ = _mlp(x1, mod3, w["w_up"], w["b_up"], w["w_down"], w["b_down"], w["ln2_g"], w["ln2_b"], gb, rb, alpha)
    return x2, new_bufs, s_fin


def _stack_layers(per_layer):
    return per_layer[0][None] if len(per_layer) == 1 else jnp.stack(per_layer)


def kernel(x_prompt, x_sample, c_prompt, c_sample, cache_kv_w128, cache_kv_w512, cache_kv_w2048, state_hgrn,
           w_ada, b_ada, w_in, lb_param, hg_norm_w, w_branch_a, w_branch_b, w_out, ln1_g, ln1_b, w_up, b_up,
           w_down, b_down, ln2_g, ln2_b):
    depth = w_ada.shape[0]
    alpha = (2 * depth) ** 0.25
    caches = (cache_kv_w128, cache_kv_w512, cache_kv_w2048)
    for (window, dil), cache in zip(ATTN_GROUPS, caches):
        assert window // dil == BAND and cache.shape[2] == window
    assert x_prompt.shape[1] % (BAND * ATTN_GROUPS[-1][1]) == 0 and x_sample.shape[1] == SUBLANES
    views = tuple(c.reshape(c.shape[0], c.shape[1], c.shape[2] * c.shape[3] * c.shape[4], c.shape[5])
                  for c in caches)
    nb = c_prompt.shape[0]
    yp, ys = x_prompt, x_sample
    p_bufs, p_states, s_bufs, s_states = [], [], [], []
    for l in range(depth):
        w = dict(w_in=w_in[l].astype(BF16), lb_param=lb_param, hg_norm_w=hg_norm_w[l],
                 w_branch_a=w_branch_a[l].astype(BF16), w_branch_b=w_branch_b[l].astype(BF16),
                 w_out=w_out[l].astype(BF16), ln1_g=ln1_g[l], ln1_b=ln1_b[l],
                 w_up=w_up[l].astype(BF16), b_up=b_up[l], w_down=w_down[l].astype(BF16), b_down=b_down[l],
                 ln2_g=ln2_g[l], ln2_b=ln2_b[l])
        mod = _modulation(jnp.concatenate([c_prompt, c_sample], axis=0), w_ada[l].astype(BF16), b_ada[l])
        yp, bufs_p, st_p, shifted = _prompt_layer(yp, mod[:nb], views, x_sample.shape[1], l, w, alpha)
        ys, bufs_s, st_s = _sample_layer(ys, mod[nb:], views, shifted, state_hgrn, l, w, alpha)
        p_bufs.append(bufs_p)
        p_states.append(st_p)
        s_bufs.append([buf.reshape(c.shape[1:]) for buf, c in zip(bufs_s, caches)])
        s_states.append(st_s)
    group = lambda bufs, gi: _stack_layers([bl[gi] for bl in bufs])
    return (yp, ys, group(p_bufs, 0), group(p_bufs, 1), group(p_bufs, 2), _stack_layers(p_states),
            group(s_bufs, 0), group(s_bufs, 1), group(s_bufs, 2), _stack_layers(s_states))
```

```python
import functools
import math
from typing import NamedTuple

import jax
import jax.numpy as jnp
from jax import lax
from jax.experimental import pallas as pl
from jax.experimental.pallas import tpu as pltpu

F32 = jnp.float32
BF16 = jnp.bfloat16

ATTN_GROUPS = ((128, 1), (512, 4), (2048, 16))
N_GROUPS = len(ATTN_GROUPS)
HEADS_PER_GROUP = 4
HEAD_DIM = 128
N_ATTN_HEADS = N_GROUPS * HEADS_PER_GROUP
ATTN_W = N_ATTN_HEADS * HEAD_DIM
GROUP_W = HEADS_PER_GROUP * HEAD_DIM
BAND = 128
HG_DK = 128
HG_DV = 128
LN_EPS = 1e-5
RMS_EPS = 1e-6
NEG = -1e30

SUBLANES = 8
LANES = 128
VMEM_LIMIT_BYTES = 56 * 1024 * 1024
COL_CHUNK = 512

NT_DIMS = (((1,), (1,)), ((), ()))
TN_DIMS = (((0,), (0,)), ((), ()))


def _alibi_slope(group, head):
    return 2.0 ** (-8.0 * (group * HEADS_PER_GROUP + head + 1) / N_ATTN_HEADS)


def _cparams(*sem):
    return pltpu.CompilerParams(dimension_semantics=sem, vmem_limit_bytes=VMEM_LIMIT_BYTES)


def _resident(shape):
    nd = len(shape)
    return pl.BlockSpec(shape, lambda *_: (0,) * nd, pipeline_mode=pl.Buffered(1))


def _sigmoid(x):
    return 1.0 / (1.0 + jnp.exp(-x))


def _layer_norm(h, g, b):
    mu = jnp.mean(h, axis=-1, keepdims=True)
    hc = h - mu
    var = jnp.mean(hc * hc, axis=-1, keepdims=True)
    return hc * lax.rsqrt(var + LN_EPS) * g + b


class _Shift(NamedTuple):
    layer: int
    b0: int
    nbatch: int
    cb: int
    rsplit: int
    k: int
    drop: int
    keep: int

    @property
    def n_chunks(self):
        return self.nbatch // self.cb * self.rsplit // self.k


def _plan_shift(layer, b0, nbatch, n_steps, drop, keep):
    if nbatch >= n_steps:
        cb = -(-nbatch // n_steps)
        while nbatch % cb:
            cb += 1
        return _Shift(layer, b0, nbatch, cb, 1, 1, drop, keep)
    best = None
    for r in (1, 2, 4, 8):
        if keep % (r * SUBLANES):
            continue
        k = -(-nbatch * r // n_steps)
        if nbatch * r % k == 0 and (best is None or k * best[0] < best[1] * r):
            best = (r, k)
    return _Shift(layer, b0, nbatch, 1, best[0], best[1], drop, keep)


def _shift_copies(shift, src_ref, dst_ref, buf, in_sem, out_sem, chunk, fetch):
    rows = shift.keep // shift.rsplit
    slot = chunk % 2
    copies = []
    for j in range(shift.k):
        part = chunk * shift.k + j
        b = shift.b0 + (part // shift.rsplit) * shift.cb
        r = (part % shift.rsplit) * rows
        if fetch:
            copies.append(pltpu.make_async_copy(
                src_ref.at[shift.layer, pl.ds(b, shift.cb), pl.ds(shift.drop + r, rows), :],
                buf.at[slot, j], in_sem.at[slot, j]))
        else:
            copies.append(pltpu.make_async_copy(
                buf.at[slot, j], dst_ref.at[pl.ds(b, shift.cb), pl.ds(r, rows), :], out_sem.at[slot, j]))
    return copies


def _host_shift(kernel_fn, shift, n_in, n_out, aliased, grid):
    n_steps = math.prod(grid)
    n_chunks = shift.n_chunks
    assert n_chunks <= n_steps

    def hosted(*refs):
        src = refs[n_in]
        pos = n_in + (2 if aliased else 1)
        outs = refs[pos:pos + n_out]
        dst = refs[pos + n_out]
        scratch = refs[pos + n_out + 1:-3]
        buf, in_sem, out_sem = refs[-3:]
        copies = functools.partial(_shift_copies, shift, src, dst, buf, in_sem, out_sem)

        def start(chunk, fetch):
            for copy in copies(chunk, fetch):
                copy.start()

        def wait(chunk, fetch):
            for copy in copies(chunk, fetch):
                copy.wait()

        def advance(s):
            if isinstance(s, int):
                if 1 <= s <= n_chunks:
                    wait(s - 1, True)
                    start(s - 1, False)
                if 2 <= s <= n_chunks + 1:
                    wait(s - 2, False)
                if s < n_chunks:
                    start(s, True)
            else:
                wait(s - 1, True)
                start(s - 1, False)
                wait(s - 2, False)
                start(s, True)

        step = 0
        for axis in range(len(grid)):
            step = step * grid[axis] + pl.program_id(axis)

        pl.when((step >= 2) & (step < n_chunks))(lambda: advance(step))
        for s in sorted({0, 1, n_chunks, n_chunks + 1}):
            if s < n_steps and not 2 <= s < n_chunks:
                pl.when(step == s)(functools.partial(advance, s))

        kernel_fn(*refs[:n_in], *outs, *scratch)

        @pl.when(step == n_steps - 1)
        def _():
            for s in range(n_steps, n_chunks + 2):
                advance(s)

    return hosted


def _hosting_call(kernel_fn, shift, cache, dst, *, out_shape, grid, in_specs, out_specs, args,
                  scratch_shapes=(), **kwargs):
    if shift is None:
        res = pl.pallas_call(kernel_fn, out_shape=tuple(out_shape), grid=grid, in_specs=list(in_specs),
                             out_specs=tuple(out_specs), scratch_shapes=list(scratch_shapes), **kwargs)(*args)
        return tuple(res), dst
    n_in, n_out = len(in_specs), len(out_shape)
    aliased = dst is not None
    anywhere = pl.BlockSpec(memory_space=pl.ANY)
    res = pl.pallas_call(
        _host_shift(kernel_fn, shift, n_in, n_out, aliased, grid),
        out_shape=tuple(out_shape) + (jax.ShapeDtypeStruct(cache.shape[1:], cache.dtype),),
        grid=grid,
        in_specs=list(in_specs) + [anywhere] * (2 if aliased else 1),
        out_specs=tuple(out_specs) + (anywhere,),
        scratch_shapes=list(scratch_shapes) + [
            pltpu.VMEM((2, shift.k, shift.cb, shift.keep // shift.rsplit, cache.shape[3]), cache.dtype),
            pltpu.SemaphoreType.DMA((2, shift.k)), pltpu.SemaphoreType.DMA((2, shift.k))],
        input_output_aliases={n_in + 1: n_out} if aliased else {},
        **kwargs,
    )(*args, cache, *([dst] if aliased else []))
    return tuple(res[:-1]), res[-1]


def _mod_kernel(c_ref, w_ref, b_ref, o_ref):
    c = c_ref[...]
    a = (c * _sigmoid(c)).astype(BF16)
    o_ref[...] = jnp.dot(a, w_ref[...], preferred_element_type=F32) + b_ref[...]


def _modulation(c, w_ada, b_ada):
    n, d = c.shape
    nout = w_ada.shape[1]
    tn = 1024
    return pl.pallas_call(
        _mod_kernel,
        out_shape=jax.ShapeDtypeStruct((n, nout), F32),
        grid=(nout // tn,),
        in_specs=[pl.BlockSpec((n, d), lambda j: (0, 0)),
                  pl.BlockSpec((d, tn), lambda j: (0, j)),
                  pl.BlockSpec((1, tn), lambda j: (0, j))],
        out_specs=pl.BlockSpec((n, tn), lambda j: (0, j)),
        compiler_params=_cparams("arbitrary"),
        name="modulation",
    )(c, w_ada, b_ada.reshape(1, nout))


def _store_heads(ref, c0, y):
    for j in range(y.shape[1] // HG_DK):
        ref[c0 // HG_DK + j] = y[:, j * HG_DK:(j + 1) * HG_DK].astype(ref.dtype)


def _inproj_kernel(x_ref, mod_ref, w_ref, lbp_ref, attn_ref, hq_ref, hk_ref, hv_ref, logf_ref, og_ref,
                   gates_ref, *, layer):
    gb, rb, d = x_ref.shape
    tm = gb * rb
    hg_w = hq_ref.shape[0] * HG_DK
    act = attn_ref.dtype
    sh = mod_ref[:, :, 0:d]
    sc = mod_ref[:, :, d:2 * d]
    u = (x_ref[...] * (1.0 + sc) + sh).reshape(tm, d).astype(BF16)

    def proj(c0):
        return jnp.dot(u, w_ref[:, c0:c0 + COL_CHUNK], preferred_element_type=F32)

    for c0 in range(0, 3 * ATTN_W, COL_CHUNK):
        attn_ref[:, c0:c0 + COL_CHUNK] = proj(c0).astype(act)
    base = 3 * ATTN_W
    for c0 in range(0, hg_w, COL_CHUNK):
        y = proj(base + c0)
        _store_heads(hq_ref, c0, y * _sigmoid(y))
    lbp = lbp_ref[...]
    e = jnp.exp(lbp - jnp.max(lbp, axis=0, keepdims=True))
    lb = jnp.sum(e[0:layer + 1], axis=0, keepdims=True) / jnp.sum(e, axis=0, keepdims=True)
    base += hg_w
    for c0 in range(0, hg_w, COL_CHUNK):
        lbc = lb[:, c0:c0 + COL_CHUNK]
        f = lbc + (1.0 - lbc) * _sigmoid(proj(base + c0))
        _store_heads(logf_ref, c0, jnp.log(f))
        _store_heads(hk_ref, c0, 1.0 - f)
    base += hg_w
    for c0 in range(0, hg_w, COL_CHUNK):
        _store_heads(hv_ref, c0, proj(base + c0))
    base += hg_w
    for c0 in range(0, hg_w, COL_CHUNK):
        y = proj(base + c0)
        _store_heads(og_ref, c0, y * _sigmoid(y))
    base += hg_w
    for c0 in range(0, 2 * d, COL_CHUNK):
        gates_ref[:, c0:c0 + COL_CHUNK] = _sigmoid(proj(base + c0)).astype(act)


def _in_projection(x3, mod3, w_in, lb_param, layer, gb, rb, act, hosted=(None, None, None)):
    g, r, d = x3.shape
    n = g * r
    tm = gb * rb
    hg_w = lb_param.shape[1]
    rt = r // rb
    heads = hg_w // HG_DK
    rows = lambda width: pl.BlockSpec((tm, width), lambda gi, ri: (gi * rt + ri, 0))
    by_head = pl.BlockSpec((heads, tm, HG_DK), lambda gi, ri: (0, gi * rt + ri, 0))
    head_major = lambda dtype: jax.ShapeDtypeStruct((heads, n, HG_DK), dtype)
    out_shapes = (
        jax.ShapeDtypeStruct((n, 3 * ATTN_W), act),
        head_major(act),
        head_major(act),
        head_major(act),
        head_major(F32),
        head_major(act),
        jax.ShapeDtypeStruct((n, 2 * d), act),
    )
    return _hosting_call(
        functools.partial(_inproj_kernel, layer=layer), *hosted,
        out_shape=out_shapes,
        grid=(g // gb, rt),
        in_specs=[pl.BlockSpec((gb, rb, d), lambda gi, ri: (gi, ri, 0)),
                  pl.BlockSpec((gb, 1, mod3.shape[2]), lambda gi, ri: (gi, 0, 0)),
                  _resident(w_in.shape),
                  _resident(lb_param.shape)],
        out_specs=(rows(3 * ATTN_W), by_head, by_head, by_head, by_head, by_head, rows(2 * d)),
        args=(x3, mod3, w_in, lb_param),
        compiler_params=_cparams("arbitrary", "arbitrary"),
        name="in_projection",
    )


ATTN_BLOCKS_PER_ITER = 4


def _attn_prompt_kernel(q_ref, k_ref, v_ref, attn_ref, qf, kf, vf, mf, lf, af, *, head_axis, group_axis):
    s = attn_ref.shape[0]
    nblk = s // BAND
    head = pl.program_id(head_axis)
    group = pl.program_id(group_axis)
    qi = lax.broadcasted_iota(jnp.int32, (BAND, BAND), 0)
    kj = lax.broadcasted_iota(jnp.int32, (BAND, BAND), 1)
    steps_own = qi - kj
    steps_prev = steps_own + BAND
    scale = HEAD_DIM ** -0.5
    ones = jnp.ones((BAND, HEAD_DIM), BF16)
    qf[...] = q_ref[...].astype(F32)
    kf[...] = k_ref[...].astype(F32)
    vf[...] = v_ref[...].astype(F32)
    for g, (window, dil) in enumerate(ATTN_GROUPS):
        nb = nblk // dil
        slope = jnp.float32(_alibi_slope(g, HEADS_PER_GROUP - 1) * dil)
        for h in range(HEADS_PER_GROUP - 1):
            slope = jnp.where(head == h, jnp.float32(_alibi_slope(g, h) * dil), slope)
        bias_own = jnp.where(steps_own >= 0, -slope * steps_own.astype(F32), NEG)
        bias_prev = jnp.where(steps_prev <= BAND, -slope * steps_prev.astype(F32), NEG)

        def body(it, carry, g=g, dil=dil, nb=nb, bias_own=bias_own, bias_prev=bias_prev):
            blocks = []
            for u in range(ATTN_BLOCKS_PER_ITER):
                idx = it * ATTN_BLOCKS_PER_ITER + u
                r = idx // nb
                i = idx - r * nb
                start = r + i * (BAND * dil)
                start_prev = r + jnp.maximum(i - 1, 0) * (BAND * dil)
                rows = pl.ds(start, BAND, stride=dil) if dil > 1 else pl.ds(start, BAND)
                rows_prev = pl.ds(start_prev, BAND, stride=dil) if dil > 1 else pl.ds(start_prev, BAND)
                blocks.append((rows, rows_prev, i > 0))
            scores = []
            for rows, rows_prev, has_prev in blocks:
                q = qf[rows, :].astype(BF16)
                scores.append((lax.dot_general(q, kf[rows, :].astype(BF16), NT_DIMS, preferred_element_type=F32),
                               lax.dot_general(q, kf[rows_prev, :].astype(BF16), NT_DIMS,
                                               preferred_element_type=F32)))
            probs = []
            for (rows, rows_prev, has_prev), (s_own, s_prev) in zip(blocks, scores):
                s_own = s_own * scale + bias_own
                s_prev = jnp.where(has_prev, s_prev * scale + bias_prev, NEG)
                m = jnp.max(jnp.maximum(s_own, s_prev), axis=-1, keepdims=True)
                probs.append((m, jnp.exp(s_own - m).astype(BF16), jnp.exp(s_prev - m).astype(BF16)))
            results = []
            for (rows, rows_prev, has_prev), (m, p_own, p_prev) in zip(blocks, probs):
                both = (jnp.dot(p_own, jnp.concatenate([vf[rows, :].astype(BF16), ones], axis=1),
                                preferred_element_type=F32)
                        + jnp.dot(p_prev, jnp.concatenate([vf[rows_prev, :].astype(BF16), ones], axis=1),
                                  preferred_element_type=F32))
                acc, l = both[:, :HEAD_DIM], both[:, HEAD_DIM:]
                if g > 0:
                    m_old = mf[rows, :]
                    m_new = jnp.maximum(m_old, m)
                    a_old = jnp.exp(m_old - m_new)
                    a_blk = jnp.exp(m - m_new)
                    l = a_old * lf[rows, :] + a_blk * l
                    acc = a_old * af[rows, :] + a_blk * acc
                    m = m_new
                results.append((rows, m, l, acc))
            for rows, m, l, acc in results:
                mf[rows, :] = jnp.broadcast_to(m, (BAND, HEAD_DIM))
                lf[rows, :] = l
                af[rows, :] = acc
            return carry

        @pl.when(group == g)
        def _(body=body):
            lax.fori_loop(0, nblk // ATTN_BLOCKS_PER_ITER, body, 0)

    @pl.when(group == N_GROUPS - 1)
    def _():
        attn_ref[...] = (af[...] / lf[...]).astype(attn_ref.dtype)


def _attention_prompt(qkv, b, s, hosted=(None, None, None)):
    hpq = ATTN_W // HEAD_DIM
    in_specs = [pl.BlockSpec((s, HEAD_DIM), lambda bi, h, g, section=section:
                             (bi, section * hpq + g * HEADS_PER_GROUP + h)) for section in range(3)]
    (attn,), dst = _hosting_call(
        functools.partial(_attn_prompt_kernel, head_axis=1, group_axis=2), *hosted,
        out_shape=(jax.ShapeDtypeStruct((b * s, GROUP_W), BF16),),
        grid=(b, HEADS_PER_GROUP, N_GROUPS),
        in_specs=in_specs,
        out_specs=(pl.BlockSpec((s, HEAD_DIM), lambda bi, h, g: (bi, h)),),
        scratch_shapes=[pltpu.VMEM((s, HEAD_DIM), F32) for _ in range(6)],
        args=[qkv] * 3,
        compiler_params=_cparams("arbitrary", "arbitrary", "arbitrary"),
        name="attn_prompt",
    )
    return attn, dst


def _attn_sample_kernel(*refs, group, dil, n_steps, length, seqs):
    first = group == 0
    last = group == N_GROUPS - 1
    q_ref, newkv_ref, cache_ref = refs[:3]
    pos = 3
    if not first:
        m_in, l_in, a_in = refs[pos:pos + 3]
        pos += 3
    pos += 1
    newc_ref = refs[pos]
    outs = refs[pos + 1:]
    t = q_ref.shape[0] // seqs
    rp = 2 * HEADS_PER_GROUP
    stretch = max(dil, t)
    nkeys = length // stretch * t
    j_c = lax.broadcasted_iota(jnp.int32, (t, nkeys), 0)
    c_c = lax.broadcasted_iota(jnp.int32, (t, nkeys), 1)
    dist_c = length + j_c - ((c_c // t) * stretch + (c_c & (t - 1)))
    ok_c = ((dist_c & (dil - 1)) == 0) & (dist_c <= n_steps * dil)
    j_n = lax.broadcasted_iota(jnp.int32, (t, t), 0)
    r_n = lax.broadcasted_iota(jnp.int32, (t, t), 1)
    dist_n = j_n - r_n
    ok_n = (dist_n >= 0) & ((dist_n & (dil - 1)) == 0) & (dist_n <= n_steps * dil)
    scale = HEAD_DIM ** -0.5
    qoff = group * GROUP_W
    cache_rows = lambda s, first_row: cache_ref[s, :, pl.ds(first_row, t, stride=rp), :].reshape(
        nkeys, HEAD_DIM).astype(BF16)
    new_rows = lambda s, first_row: newkv_ref[s, pl.ds(first_row, t, stride=rp), :].astype(BF16)
    chains = [(s, h) for s in range(seqs) for h in range(HEADS_PER_GROUP)]
    scores = []
    for s, h in chains:
        q = q_ref[s * t:(s + 1) * t, qoff + h * HEAD_DIM:qoff + (h + 1) * HEAD_DIM].astype(BF16)
        scores.append((lax.dot_general(q, cache_rows(s, h), NT_DIMS, preferred_element_type=F32),
                       lax.dot_general(q, new_rows(s, h), NT_DIMS, preferred_element_type=F32)))
    probs = []
    for (s, h), (s_c, s_n) in zip(chains, scores):
        rows, sl = slice(s * t, (s + 1) * t), slice(h * HEAD_DIM, (h + 1) * HEAD_DIM)
        slope = _alibi_slope(group, h)
        s_c = jnp.where(ok_c, s_c * scale - slope * dist_c.astype(F32), NEG)
        s_n = jnp.where(ok_n, s_n * scale - slope * dist_n.astype(F32), NEG)
        m = jnp.maximum(jnp.max(s_c, axis=-1, keepdims=True), jnp.max(s_n, axis=-1, keepdims=True))
        if not first:
            m = jnp.maximum(m, m_in[rows, sl][:, 0:1])
        p_c = jnp.exp(s_c - m)
        p_n = jnp.exp(s_n - m)
        l = jnp.sum(p_c, axis=-1, keepdims=True) + jnp.sum(p_n, axis=-1, keepdims=True)
        probs.append((m, l, p_c.astype(BF16), p_n.astype(BF16)))
    for (s, h), (m, l, p_c, p_n) in zip(chains, probs):
        rows, sl = slice(s * t, (s + 1) * t), slice(h * HEAD_DIM, (h + 1) * HEAD_DIM)
        acc = (jnp.dot(p_c, cache_rows(s, HEADS_PER_GROUP + h), preferred_element_type=F32)
               + jnp.dot(p_n, new_rows(s, HEADS_PER_GROUP + h), preferred_element_type=F32))
        if not first:
            a_old = jnp.exp(m_in[rows, sl] - m)
            l = a_old * l_in[rows, sl] + l
            acc = a_old * a_in[rows, sl] + acc
        if last:
            outs[0][rows, sl] = acc / l
        else:
            outs[0][rows, sl] = jnp.broadcast_to(m, (t, HEAD_DIM))
            outs[1][rows, sl] = jnp.broadcast_to(l, (t, HEAD_DIM))
            outs[2][rows, sl] = acc
    for s in range(seqs):
        newc_ref[s] = newkv_ref[s]


SAMPLE_SEQS_PER_STEP = (8, 4, 2)


def _attention_sample(qkv, new_kv, cache, shifted, layer, running, t, group):
    window, dil = ATTN_GROUPS[group]
    depth, bd, crow, dh = cache.shape
    rp = 2 * HEADS_PER_GROUP
    length = crow // rp
    stretch = max(dil, t)
    seqs = math.gcd(SAMPLE_SEQS_PER_STEP[group], bd)
    first = group == 0
    last = group == N_GROUPS - 1
    rows = pl.BlockSpec((seqs * t, GROUP_W), lambda bi: (bi, 0))
    stat = jax.ShapeDtypeStruct((bd * t, GROUP_W), F32)
    new_blk = pl.BlockSpec((seqs, t * rp, dh), lambda bi: (bi, 0, 0))
    in_specs = [pl.BlockSpec((seqs * t, 3 * ATTN_W), lambda bi: (bi, 0)),
                new_blk,
                pl.BlockSpec((None, seqs, length // stretch, t * rp, dh), lambda bi: (layer, bi, 0, 0, 0))]
    args = [qkv, new_kv, cache.reshape(depth, bd, length // stretch, stretch * rp, dh)]
    if not first:
        in_specs += [rows, rows, rows]
        args += list(running)
    in_specs.append(pl.BlockSpec(memory_space=pl.ANY))
    args.append(shifted)
    n_out = 1 if last else 3
    res = pl.pallas_call(
        functools.partial(_attn_sample_kernel, group=group, dil=dil, n_steps=window // dil, length=length,
                          seqs=seqs),
        out_shape=(jax.ShapeDtypeStruct(shifted.shape, shifted.dtype),) + (stat,) * n_out,
        grid=(bd // seqs,),
        in_specs=in_specs,
        out_specs=(pl.BlockSpec((seqs, t * rp, dh), lambda bi: (bi, length // t - 1, 0)),) + (rows,) * n_out,
        input_output_aliases={len(args) - 1: 0},
        compiler_params=_cparams("arbitrary"),
        name=f"attn_sample_g{group}",
    )(*args)
    return res[0], (res[1] if last else tuple(res[1:]))


def _cumsum_rows(x, row):
    d = 1
    while d < x.shape[0]:
        x = x + jnp.where(row >= d, pltpu.roll(x, d, 0), 0.0)
        d *= 2
    return x


def _hgrn_carry(q, k, v, g, st):
    c = q.shape[0]
    qg = (q * jnp.exp(g)).astype(BF16)
    o = lax.dot_general(qg, st.astype(BF16), NT_DIMS, preferred_element_type=F32)
    g_last = g[c - 1:c, :]
    kd = (k * jnp.exp(g_last - g)).astype(BF16)
    st_new = st * jnp.exp(g_last) + lax.dot_general(v.astype(BF16), kd, TN_DIMS, preferred_element_type=F32)
    return o, st_new


def _hgrn_chunks_pairwise(qs, ks, vs, lfs, states):
    c = qs[0].shape[0]
    row = lax.broadcasted_iota(jnp.int32, qs[0].shape, 0)
    gs = [_cumsum_rows(lf, row) for lf in lfs]
    outs = []
    for q, k, v, g in zip(qs, ks, vs, gs):
        o = jnp.sum(q * k, axis=-1, keepdims=True) * v
        for d in range(1, c):
            ok = row >= d
            e = jnp.exp(jnp.where(ok, g - pltpu.roll(g, d, 0), 0.0))
            a = jnp.sum(jnp.where(ok, q * pltpu.roll(k, d, 0) * e, 0.0), axis=-1, keepdims=True)
            o = o + a * pltpu.roll(v, d, 0)
        outs.append(o)
    cols = []
    for k, g in zip(ks, gs):
        g_last = g[c - 1:c, :]
        kd = k * jnp.exp(g_last - g)
        cols.append(jnp.concatenate([kd, jnp.broadcast_to(jnp.exp(g_last), kd.shape)], axis=0).T)
    carried = [jnp.dot((q * jnp.exp(g)).astype(BF16), s.astype(BF16), preferred_element_type=F32)
               for q, g, s in zip(qs, gs, states)]
    new_states = [s * col[:, c:c + 1] + jnp.dot(col[:, 0:c].astype(BF16), v.astype(BF16),
                                                preferred_element_type=F32)
                  for s, col, v in zip(states, cols, vs)]
    return [o + oc for o, oc in zip(outs, carried)], new_states


def _pair_level(c):
    ri = lax.broadcasted_iota(jnp.int32, (c, c), 0)
    ci = lax.broadcasted_iota(jnp.int32, (c, c), 1)
    x = ri ^ ci
    bits = jnp.zeros((c, c), jnp.int32)
    p = 1
    while p < c:
        bits = bits + (x >= p).astype(jnp.int32)
        p *= 2
    return jnp.where(ri > ci, bits + 1, jnp.where(ri == ci, 1, 0))


def _hgrn_chunks_blocked(qs, ks, vs, lfs, sts, g_refs, level):
    nh = len(qs)
    c = qs[0].shape[0]
    row = lax.broadcasted_iota(jnp.int32, (c, HG_DK), 0)
    sub = row & (SUBLANES - 1)
    sub8 = lax.broadcasted_iota(jnp.int32, (SUBLANES, HG_DK), 0)
    gs = []
    for lf, g_ref in zip(lfs, g_refs):
        x = lf
        for d in (1, 2, 4):
            x = x + jnp.where(sub >= d, pltpu.roll(x, d, 0), 0.0)
        g_ref[...] = x
        pieces = [x[0:SUBLANES]]
        off = None
        for j in range(1, c // SUBLANES):
            tot = jnp.broadcast_to(g_ref[j * SUBLANES - 1:j * SUBLANES, :], (SUBLANES, HG_DK))
            off = tot if off is None else off + tot
            pieces.append(x[j * SUBLANES:(j + 1) * SUBLANES] + off)
        g = jnp.concatenate(pieces, axis=0)
        g_ref[...] = g
        gs.append(g)

    def mid_rows(g_ref, n):
        half = n // 2
        bcast = lambda r: jnp.broadcast_to(g_ref[r:r + 1, :], (SUBLANES, HG_DK))
        out = []
        for j in range(c // SUBLANES):
            base = j * SUBLANES
            if n >= SUBLANES:
                out.append(bcast(base // n * n + half - 1))
            else:
                assert n == 4
                out.append(jnp.where(sub8 < 4, bcast(base + 1), bcast(base + 5)))
        return jnp.concatenate(out, axis=0)

    pair = lambda y, z: lax.dot_general(y, z, NT_DIMS, preferred_element_type=F32)
    prods = [[pair(q.astype(BF16), k.astype(BF16))] for q, k in zip(qs, ks)]
    n = 2
    while n <= c:
        second = (row & (n // 2)) != 0
        for u in range(nh):
            if n == 2:
                arg = jnp.where(second, lfs[u], 0.0)
            else:
                gmid = mid_rows(g_refs[u], n)
                arg = jnp.where(second, gs[u] - gmid, gmid - gs[u])
            y = (jnp.where(second, qs[u], ks[u]) * jnp.exp(arg)).astype(BF16)
            prods[u].append(pair(y, y))
        n *= 2
    outs = []
    for u in range(nh):
        amat = jnp.where(level == 1, prods[u][0], 0.0)
        for j, p in enumerate(prods[u][1:]):
            amat = jnp.where(level == j + 2, p, amat)
        outs.append(jnp.dot(amat.astype(BF16), vs[u].astype(BF16), preferred_element_type=F32))
    states = []
    for u in range(nh):
        o_carry, st_new = _hgrn_carry(qs[u], ks[u], vs[u], gs[u], sts[u])
        outs[u] = outs[u] + o_carry
        states.append(st_new)
    return outs, states


def _hgrn_output(o, og, nw):
    return o * lax.rsqrt(jnp.mean(o * o, axis=-1, keepdims=True) + RMS_EPS) * nw * og


HGRN_HEADS_PER_ITER = 4


def _hgrn_prompt_kernel(q_ref, k_ref, v_ref, lf_ref, og_ref, nw_ref, hg_ref, sfin_ref, st_ref, g_ref):
    ci = pl.program_id(1)
    heads, c, _ = q_ref.shape

    @pl.when(ci == 0)
    def _():
        st_ref[...] = jnp.zeros_like(st_ref)

    level = _pair_level(c)

    def head_group(i, carry):
        hs = [i * HGRN_HEADS_PER_ITER + u for u in range(HGRN_HEADS_PER_ITER)]
        load = lambda ref: [ref[h].astype(F32) for h in hs]
        outs, states = _hgrn_chunks_blocked(load(q_ref), load(k_ref), load(v_ref), load(lf_ref), load(st_ref),
                                            [g_ref.at[u] for u in range(HGRN_HEADS_PER_ITER)], level)
        for h, o, st_new in zip(hs, outs, states):
            st_ref[h] = st_new
            hg_ref[h] = _hgrn_output(o, og_ref[h].astype(F32), nw_ref[...]).astype(hg_ref.dtype)
        return carry

    lax.fori_loop(0, heads // HGRN_HEADS_PER_ITER, head_group, 0)

    @pl.when(ci == pl.num_programs(1) - 1)
    def _():
        for h in range(heads):
            sfin_ref[0, h] = st_ref[h].T


def _hgrn_prompt(hq, hk, hv, logf, og, norm_w, b, s, chunk, hosted=(None, None, None)):
    heads, n, _ = hq.shape
    nc = s // chunk
    blk = pl.BlockSpec((heads, chunk, HG_DK), lambda bi, ci: (0, bi * nc + ci, 0))
    (hg, s_fin), dst = _hosting_call(
        _hgrn_prompt_kernel, *hosted,
        out_shape=(jax.ShapeDtypeStruct((heads, n, HG_DV), BF16),
                   jax.ShapeDtypeStruct((b, heads, HG_DK, HG_DV), F32)),
        grid=(b, nc),
        in_specs=[blk, blk, blk, blk, blk, pl.BlockSpec((1, HG_DV), lambda bi, ci: (0, 0))],
        out_specs=(blk, pl.BlockSpec((1, heads, HG_DK, HG_DV), lambda bi, ci: (bi, 0, 0, 0))),
        scratch_shapes=[pltpu.VMEM((heads, HG_DV, HG_DK), F32),
                        pltpu.VMEM((HGRN_HEADS_PER_ITER, chunk, HG_DK), F32)],
        args=(hq, hk, hv, logf, og, norm_w.reshape(1, HG_DV)),
        compiler_params=_cparams("arbitrary", "arbitrary"),
        name="hgrn_prompt",
    )
    return hg, s_fin, dst


HGRN_SAMPLE_SEQS_PER_STEP = 2


def _hgrn_sample_kernel(q_ref, k_ref, v_ref, lf_ref, og_ref, nw_ref, s_ref, hg_ref, snew_ref):
    seqs, heads = s_ref.shape[0], s_ref.shape[1]
    t = q_ref.shape[1] // seqs
    chains = [(s, h) for s in range(seqs) for h in range(heads)]
    load = lambda ref: [ref[h, s * t:(s + 1) * t, :] for s, h in chains]
    outs, new_states = _hgrn_chunks_pairwise(load(q_ref), load(k_ref), load(v_ref), load(lf_ref),
                                             [s_ref[s, h] for s, h in chains])
    for (s, h), o, st_new in zip(chains, outs, new_states):
        snew_ref[s, h] = st_new
        hg_ref[h, s * t:(s + 1) * t, :] = _hgrn_output(o, og_ref[h, s * t:(s + 1) * t, :], nw_ref[...])


def _hgrn_sample(hq, hk, hv, logf, og, norm_w, state, t):
    bd, heads = state.shape[0], state.shape[1]
    seqs = math.gcd(HGRN_SAMPLE_SEQS_PER_STEP, bd)
    blk = pl.BlockSpec((heads, seqs * t, HG_DK), lambda bi: (0, bi, 0))
    sblk = pl.BlockSpec((seqs, heads, HG_DK, HG_DV), lambda bi: (bi, 0, 0, 0))
    return pl.pallas_call(
        _hgrn_sample_kernel,
        out_shape=(jax.ShapeDtypeStruct((heads, bd * t, HG_DV), F32),
                   jax.ShapeDtypeStruct(state.shape, state.dtype)),
        grid=(bd // seqs,),
        in_specs=[blk, blk, blk, blk, blk, pl.BlockSpec((1, HG_DV), lambda bi: (0, 0)), sblk],
        out_specs=(blk, sblk),
        compiler_params=_cparams("arbitrary"),
        name="hgrn_sample",
    )(hq, hk, hv, logf, og, norm_w.reshape(1, HG_DV), state)


def _merge_kernel(x_ref, mod_ref, attn_ref, hg_ref, gates_ref, wa_ref, wb_ref, wo_ref, lng_ref, lnb_ref,
                  x1_ref, *, alpha):
    gb, rb, d = x_ref.shape
    tm = gb * rb
    hg = jnp.concatenate([hg_ref[h].astype(BF16) for h in range(hg_ref.shape[0])], axis=1)
    branch_a = jnp.dot(attn_ref[...].astype(BF16), wa_ref[...], preferred_element_type=F32)
    branch_b = jnp.dot(hg, wb_ref[...], preferred_element_type=F32)
    gate_a = gates_ref[:, 0:d].astype(F32)
    gate_b = gates_ref[:, d:2 * d].astype(F32)
    merged = gate_a * branch_a + gate_b * branch_b
    mix = jnp.dot(merged.astype(BF16), wo_ref[...], preferred_element_type=F32)
    g1 = mod_ref[:, :, 2 * d:3 * d]
    h = alpha * x_ref[...] + g1 * mix.reshape(gb, rb, d)
    x1_ref[...] = _layer_norm(h, lng_ref[...], lnb_ref[...])


def _merge(x3, mod3, attn, hg, gates, wa, wb, wo, ln_g, ln_b, gb, rb, alpha, hosted=(None, None, None)):
    g, r, d = x3.shape
    tm = gb * rb
    rt = r // rb
    row = lambda gi, ri: (gi * rt + ri, 0)
    rows = lambda a: pl.BlockSpec((tm, a.shape[1]), row)
    tile3 = pl.BlockSpec((gb, rb, d), lambda gi, ri: (gi, ri, 0))
    (x1,), dst = _hosting_call(
        functools.partial(_merge_kernel, alpha=alpha), *hosted,
        out_shape=(jax.ShapeDtypeStruct(x3.shape, F32),),
        grid=(g // gb, rt),
        in_specs=[tile3, pl.BlockSpec((gb, 1, mod3.shape[2]), lambda gi, ri: (gi, 0, 0)),
                  rows(attn), pl.BlockSpec((hg.shape[0], tm, hg.shape[2]), lambda gi, ri: (0, gi * rt + ri, 0)),
                  rows(gates),
                  _resident(wa.shape), _resident(wb.shape), _resident(wo.shape),
                  _resident((1, d)), _resident((1, d))],
        out_specs=(tile3,),
        args=(x3, mod3, attn, hg, gates, wa, wb, wo, ln_g.reshape(1, d), ln_b.reshape(1, d)),
        compiler_params=_cparams("arbitrary", "arbitrary"),
        name="merge",
    )
    return x1, dst


def _mlp_kernel(x1_ref, mod_ref, wu_ref, bu_ref, wd_ref, bd_ref, lng_ref, lnb_ref, x2_ref, *, alpha):
    gb, rb, d = x1_ref.shape
    tm = gb * rb
    dff = wu_ref.shape[1]
    x1 = x1_ref[...]
    sh = mod_ref[:, :, 3 * d:4 * d]
    sc = mod_ref[:, :, 4 * d:5 * d]
    g2 = mod_ref[:, :, 5 * d:6 * d]
    u = (x1 * (1.0 + sc) + sh).reshape(tm, d).astype(BF16)
    acc = jnp.zeros((tm, d), F32)
    for c0 in range(0, dff, COL_CHUNK):
        hid = jnp.dot(u, wu_ref[:, c0:c0 + COL_CHUNK], preferred_element_type=F32) + bu_ref[:, c0:c0 + COL_CHUNK]
        hid = jnp.square(jnp.maximum(hid, 0.0))
        acc = acc + jnp.dot(hid.astype(BF16), wd_ref[c0:c0 + COL_CHUNK, :], preferred_element_type=F32)
    ff = acc + bd_ref[...]
    h = alpha * x1 + g2 * ff.reshape(gb, rb, d)
    x2_ref[...] = _layer_norm(h, lng_ref[...], lnb_ref[...])


def _mlp(x3, mod3, wu, bu, wd, bd, ln_g, ln_b, gb, rb, alpha, hosted=(None, None, None)):
    g, r, d = x3.shape
    dff = wu.shape[1]
    tile3 = pl.BlockSpec((gb, rb, d), lambda gi, ri: (gi, ri, 0))
    (x2,), dst = _hosting_call(
        functools.partial(_mlp_kernel, alpha=alpha), *hosted,
        out_shape=(jax.ShapeDtypeStruct(x3.shape, F32),),
        grid=(g // gb, r // rb),
        in_specs=[tile3, pl.BlockSpec((gb, 1, mod3.shape[2]), lambda gi, ri: (gi, 0, 0)),
                  _resident(wu.shape), _resident((1, dff)), _resident(wd.shape), _resident((1, d)),
                  _resident((1, d)), _resident((1, d))],
        out_specs=(tile3,),
        args=(x3, mod3, wu, bu.reshape(1, dff), wd, bd.reshape(1, d), ln_g.reshape(1, d), ln_b.reshape(1, d)),
        compiler_params=_cparams("arbitrary", "arbitrary"),
        name="mlp",
    )
    return x2, dst


PROMPT_ROWS = 256
SAMPLE_SEQS = 32
HGRN_CHUNK = 128


def _kv_rows(qkv, b, s, group, rows):
    q3 = qkv.reshape(b, s, 3 * ATTN_W)
    k = q3[:, s - rows:, ATTN_W + group * GROUP_W:ATTN_W + (group + 1) * GROUP_W]
    v = q3[:, s - rows:, 2 * ATTN_W + group * GROUP_W:2 * ATTN_W + (group + 1) * GROUP_W]
    return jnp.stack([k, v], axis=2).astype(F32).reshape(b, rows, 2, HEADS_PER_GROUP, HEAD_DIM)


def _prompt_layer(x, mod, caches, t_new, layer, w, alpha):
    b, s, d = x.shape
    mod3 = mod.reshape(b, 1, mod.shape[1])
    gb, rb = 1, min(PROMPT_ROWS, s)
    chunk = min(HGRN_CHUNK, s)
    rp = 2 * HEADS_PER_GROUP
    bd = caches[0].shape[1]
    tile_steps = s // rb * b
    hgrn_steps = s // chunk * b

    def plan(gi, b0, nbatch, n_steps):
        rows = caches[gi].shape[2]
        return _plan_shift(layer, b0, nbatch, n_steps, t_new * rp, rows - t_new * rp)

    attn_steps = b * HEADS_PER_GROUP * N_GROUPS
    n_hgrn = bd // 4
    n_attn = (bd - n_hgrn) // 2
    big = None

    def share(b0, nbatch, n_steps):
        return (plan(2, b0, nbatch, n_steps), caches[2], big) if nbatch else (None, None, big)

    (qkv, hq, hk, hv, logf, og, gates), mid = _in_projection(
        x, mod3, w["w_in"], w["lb_param"], layer, gb, rb, BF16,
        hosted=(plan(1, 0, bd, tile_steps), caches[1], None))
    hg, s_fin, big = _hgrn_prompt(hq, hk, hv, logf, og, w["hg_norm_w"], b, s, chunk,
                                  hosted=share(0, n_hgrn, hgrn_steps))
    attn, big = _attention_prompt(qkv, b, s, hosted=share(n_hgrn, n_attn, attn_steps))
    new_bufs = [_kv_rows(qkv, b, s, gi, min(window, s)) for gi, (window, dil) in enumerate(ATTN_GROUPS)]
    x1, small = _merge(x, mod3, attn, hg, gates, w["w_branch_a"], w["w_branch_b"], w["w_out"],
                       w["ln1_g"], w["ln1_b"], gb, rb, alpha, hosted=(plan(0, 0, bd, tile_steps), caches[0], None))
    x2, big = _mlp(x1, mod3, w["w_up"], w["b_up"], w["w_down"], w["b_down"], w["ln2_g"], w["ln2_b"], gb, rb,
                   alpha, hosted=share(n_hgrn + n_attn, bd - n_hgrn - n_attn, tile_steps))
    return x2, new_bufs, s_fin, (small, mid, big)


def _sample_layer(x, mod, caches, shifted, states, layer, w, alpha):
    b, s, d = x.shape
    mod3 = mod.reshape(b, 1, mod.shape[1])
    gb, rb = min(SAMPLE_SEQS, b), s
    rp = 2 * HEADS_PER_GROUP
    (qkv, hq, hk, hv, logf, og, gates), _ = _in_projection(x, mod3, w["w_in"], w["lb_param"], layer, gb, rb, F32)
    new_bufs = []
    running = None
    for gi in range(N_GROUPS):
        new_kv = _kv_rows(qkv, b, s, gi, s).reshape(b, s * rp, HEAD_DIM)
        buf, running = _attention_sample(qkv, new_kv, caches[gi], shifted[gi], layer, running, s, gi)
        new_bufs.append(buf)
    hg, s_fin = _hgrn_sample(hq, hk, hv, logf, og, w["hg_norm_w"], states[layer], s)
    x1, _ = _merge(x, mod3, running, hg, gates, w["w_branch_a"], w["w_branch_b"], w["w_out"],
                   w["ln1_g"], w["ln1_b"], gb, rb, alpha)
    x2, _ = _mlp(x1, mod3, w["w_up"], w["b_up"], w["w_down"], w["b_down"], w["ln2_g"], w["ln2_b"], gb, rb, alpha)
    return x2, new_bufs, s_fin


def _stack_layers(per_layer):
    return per_layer[0][None] if len(per_layer) == 1 else jnp.stack(per_layer)


def kernel(x_prompt, x_sample, c_prompt, c_sample, cache_kv_w128, cache_kv_w512, cache_kv_w2048, state_hgrn,
           w_ada, b_ada, w_in, lb_param, hg_norm_w, w_branch_a, w_branch_b, w_out, ln1_g, ln1_b, w_up, b_up,
           w_down, b_down, ln2_g, ln2_b):
    depth = w_ada.shape[0]
    alpha = (2 * depth) ** 0.25
    caches = (cache_kv_w128, cache_kv_w512, cache_kv_w2048)
    for (window, dil), cache in zip(ATTN_GROUPS, caches):
        assert window // dil == BAND and cache.shape[2] == window
    assert x_prompt.shape[1] % (BAND * ATTN_GROUPS[-1][1]) == 0 and x_sample.shape[1] == SUBLANES
    views = tuple(c.reshape(c.shape[0], c.shape[1], c.shape[2] * c.shape[3] * c.shape[4], c.shape[5])
                  for c in caches)
    nb = c_prompt.shape[0]
    yp, ys = x_prompt, x_sample
    p_bufs, p_states, s_bufs, s_states = [], [], [], []
    for l in range(depth):
        w = dict(w_in=w_in[l].astype(BF16), lb_param=lb_param, hg_norm_w=hg_norm_w[l],
                 w_branch_a=w_branch_a[l].astype(BF16), w_branch_b=w_branch_b[l].astype(BF16),
                 w_out=w_out[l].astype(BF16), ln1_g=ln1_g[l], ln1_b=ln1_b[l],
                 w_up=w_up[l].astype(BF16), b_up=b_up[l], w_down=w_down[l].astype(BF16), b_down=b_down[l],
                 ln2_g=ln2_g[l], ln2_b=ln2_b[l])
        mod = _modulation(jnp.concatenate([c_prompt, c_sample], axis=0), w_ada[l].astype(BF16), b_ada[l])
        yp, bufs_p, st_p, shifted = _prompt_layer(yp, mod[:nb], views, x_sample.shape[1], l, w, alpha)
        ys, bufs_s, st_s = _sample_layer(ys, mod[nb:], views, shifted, state_hgrn, l, w, alpha)
        p_bufs.append(bufs_p)
        p_states.append(st_p)
        s_bufs.append([buf.reshape(c.shape[1:]) for buf, c in zip(bufs_s, caches)])
        s_states.append(st_s)
    group = lambda bufs, gi: _stack_layers([bl[gi] for bl in bufs])
    return (yp, ys, group(p_bufs, 0), group(p_bufs, 1), group(p_bufs, 2), _stack_layers(p_states),
            group(s_bufs, 0), group(s_bufs, 1), group(s_bufs, 2), _stack_layers(s_states))
```

```python
import functools
import math
from typing import NamedTuple

import jax
import jax.numpy as jnp
from jax import lax
from jax.experimental import pallas as pl
from jax.experimental.pallas import tpu as pltpu

F32 = jnp.float32
BF16 = jnp.bfloat16

ATTN_GROUPS = ((128, 1), (512, 4), (2048, 16))
N_GROUPS = len(ATTN_GROUPS)
HEADS_PER_GROUP = 4
HEAD_DIM = 128
N_ATTN_HEADS = N_GROUPS * HEADS_PER_GROUP
ATTN_W = N_ATTN_HEADS * HEAD_DIM
GROUP_W = HEADS_PER_GROUP * HEAD_DIM
BAND = 128
HG_DK = 128
HG_DV = 128
LN_EPS = 1e-5
RMS_EPS = 1e-6
NEG = -1e30

SUBLANES = 8
LANES = 128
VMEM_LIMIT_BYTES = 56 * 1024 * 1024
COL_CHUNK = 512

NT_DIMS = (((1,), (1,)), ((), ()))
TN_DIMS = (((0,), (0,)), ((), ()))


def _alibi_slope(group, head):
    return 2.0 ** (-8.0 * (group * HEADS_PER_GROUP + head + 1) / N_ATTN_HEADS)


def _cparams(*sem):
    return pltpu.CompilerParams(dimension_semantics=sem, vmem_limit_bytes=VMEM_LIMIT_BYTES)


def _resident(shape):
    nd = len(shape)
    return pl.BlockSpec(shape, lambda *_: (0,) * nd, pipeline_mode=pl.Buffered(1))


def _sigmoid(x):
    return 1.0 / (1.0 + jnp.exp(-x))


def _layer_norm(h, g, b):
    mu = jnp.mean(h, axis=-1, keepdims=True)
    hc = h - mu
    var = jnp.mean(hc * hc, axis=-1, keepdims=True)
    return hc * lax.rsqrt(var + LN_EPS) * g + b


class _Shift(NamedTuple):
    layer: int
    b0: int
    nbatch: int
    cb: int
    rsplit: int
    k: int
    drop: int
    keep: int

    @property
    def n_chunks(self):
        return self.nbatch // self.cb * self.rsplit // self.k


def _plan_shift(layer, b0, nbatch, n_steps, drop, keep):
    if nbatch >= n_steps:
        cb = -(-nbatch // n_steps)
        while nbatch % cb:
            cb += 1
        return _Shift(layer, b0, nbatch, cb, 1, 1, drop, keep)
    best = None
    for r in (1, 2, 4, 8):
        if keep % (r * SUBLANES):
            continue
        k = -(-nbatch * r // n_steps)
        if nbatch * r % k == 0 and (best is None or k * best[0] < best[1] * r):
            best = (r, k)
    return _Shift(layer, b0, nbatch, 1, best[0], best[1], drop, keep)


def _shift_copies(shift, src_ref, dst_ref, buf, in_sem, out_sem, chunk, fetch):
    rows = shift.keep // shift.rsplit
    slot = chunk % 2
    copies = []
    for j in range(shift.k):
        part = chunk * shift.k + j
        b = shift.b0 + (part // shift.rsplit) * shift.cb
        r = (part % shift.rsplit) * rows
        if fetch:
            copies.append(pltpu.make_async_copy(
                src_ref.at[shift.layer, pl.ds(b, shift.cb), pl.ds(shift.drop + r, rows), :],
                buf.at[slot, j], in_sem.at[slot, j]))
        else:
            copies.append(pltpu.make_async_copy(
                buf.at[slot, j], dst_ref.at[pl.ds(b, shift.cb), pl.ds(r, rows), :], out_sem.at[slot, j]))
    return copies


def _host_shift(kernel_fn, shift, n_in, n_out, aliased, grid):
    n_steps = math.prod(grid)
    n_chunks = shift.n_chunks
    assert n_chunks <= n_steps

    def hosted(*refs):
        src = refs[n_in]
        pos = n_in + (2 if aliased else 1)
        outs = refs[pos:pos + n_out]
        dst = refs[pos + n_out]
        scratch = refs[pos + n_out + 1:-3]
        buf, in_sem, out_sem = refs[-3:]
        copies = functools.partial(_shift_copies, shift, src, dst, buf, in_sem, out_sem)

        def start(chunk, fetch):
            for copy in copies(chunk, fetch):
                copy.start()

        def wait(chunk, fetch):
            for copy in copies(chunk, fetch):
                copy.wait()

        def advance(s):
            if isinstance(s, int):
                if 1 <= s <= n_chunks:
                    wait(s - 1, True)
                    start(s - 1, False)
                if 2 <= s <= n_chunks + 1:
                    wait(s - 2, False)
                if s < n_chunks:
                    start(s, True)
            else:
                wait(s - 1, True)
                start(s - 1, False)
                wait(s - 2, False)
                start(s, True)

        step = 0
        for axis in range(len(grid)):
            step = step * grid[axis] + pl.program_id(axis)

        pl.when((step >= 2) & (step < n_chunks))(lambda: advance(step))
        for s in sorted({0, 1, n_chunks, n_chunks + 1}):
            if s < n_steps and not 2 <= s < n_chunks:
                pl.when(step == s)(functools.partial(advance, s))

        kernel_fn(*refs[:n_in], *outs, *scratch)

        @pl.when(step == n_steps - 1)
        def _():
            for s in range(n_steps, n_chunks + 2):
                advance(s)

    return hosted


def _hosting_call(kernel_fn, shift, cache, dst, *, out_shape, grid, in_specs, out_specs, args,
                  scratch_shapes=(), **kwargs):
    if shift is None:
        res = pl.pallas_call(kernel_fn, out_shape=tuple(out_shape), grid=grid, in_specs=list(in_specs),
                             out_specs=tuple(out_specs), scratch_shapes=list(scratch_shapes), **kwargs)(*args)
        return tuple(res), dst
    n_in, n_out = len(in_specs), len(out_shape)
    aliased = dst is not None
    anywhere = pl.BlockSpec(memory_space=pl.ANY)
    res = pl.pallas_call(
        _host_shift(kernel_fn, shift, n_in, n_out, aliased, grid),
        out_shape=tuple(out_shape) + (jax.ShapeDtypeStruct(cache.shape[1:], cache.dtype),),
        grid=grid,
        in_specs=list(in_specs) + [anywhere] * (2 if aliased else 1),
        out_specs=tuple(out_specs) + (anywhere,),
        scratch_shapes=list(scratch_shapes) + [
            pltpu.VMEM((2, shift.k, shift.cb, shift.keep // shift.rsplit, cache.shape[3]), cache.dtype),
            pltpu.SemaphoreType.DMA((2, shift.k)), pltpu.SemaphoreType.DMA((2, shift.k))],
        input_output_aliases={n_in + 1: n_out} if aliased else {},
        **kwargs,
    )(*args, cache, *([dst] if aliased else []))
    return tuple(res[:-1]), res[-1]


def _mod_kernel(c_ref, w_ref, b_ref, o_ref):
    c = c_ref[...]
    a = (c * _sigmoid(c)).astype(BF16)
    o_ref[...] = jnp.dot(a, w_ref[...], preferred_element_type=F32) + b_ref[...]


def _modulation(c, w_ada, b_ada):
    n, d = c.shape
    nout = w_ada.shape[1]
    tn = 1024
    return pl.pallas_call(
        _mod_kernel,
        out_shape=jax.ShapeDtypeStruct((n, nout), F32),
        grid=(nout // tn,),
        in_specs=[pl.BlockSpec((n, d), lambda j: (0, 0)),
                  pl.BlockSpec((d, tn), lambda j: (0, j)),
                  pl.BlockSpec((1, tn), lambda j: (0, j))],
        out_specs=pl.BlockSpec((n, tn), lambda j: (0, j)),
        compiler_params=_cparams("arbitrary"),
        name="modulation",
    )(c, w_ada, b_ada.reshape(1, nout))


def _store_heads(ref, c0, y):
    for j in range(y.shape[1] // HG_DK):
        ref[c0 // HG_DK + j] = y[:, j * HG_DK:(j + 1) * HG_DK].astype(ref.dtype)


def _store_kv(ref, kv, y3, tile, seq_rows, kept):
    gb, rb, _ = y3.shape
    rp = 2 * HEADS_PER_GROUP
    n = min(kept, rb)

    def store():
        for h in range(HEADS_PER_GROUP):
            ref[:, pl.ds(kv * HEADS_PER_GROUP + h, n, stride=rp), :] = y3[:, rb - n:, h * HEAD_DIM:(h + 1) * HEAD_DIM]

    first_tile = (seq_rows - max(kept, rb)) // rb
    if first_tile == 0:
        store()
    else:
        pl.when(tile >= first_tile)(store)


def _inproj_kernel(x_ref, mod_ref, w_ref, lbp_ref, attn_ref, hq_ref, hk_ref, hv_ref, logf_ref, og_ref,
                   gates_ref, *kv_refs, layer, seq_rows, kv_rows):
    gb, rb, d = x_ref.shape
    tm = gb * rb
    hg_w = hq_ref.shape[0] * HG_DK
    act = attn_ref.dtype
    sh = mod_ref[:, :, 0:d]
    sc = mod_ref[:, :, d:2 * d]
    u = (x_ref[...] * (1.0 + sc) + sh).reshape(tm, d).astype(BF16)

    def proj(c0):
        return jnp.dot(u, w_ref[:, c0:c0 + COL_CHUNK], preferred_element_type=F32)

    assert COL_CHUNK == GROUP_W
    for c0 in range(0, 3 * ATTN_W, COL_CHUNK):
        y = proj(c0)
        attn_ref[:, c0:c0 + COL_CHUNK] = y.astype(act)
        section, g = divmod(c0 // GROUP_W, N_GROUPS)
        if section:
            _store_kv(kv_refs[g], section - 1, y.reshape(gb, rb, GROUP_W), pl.program_id(1), seq_rows, kv_rows[g])
    base = 3 * ATTN_W
    for c0 in range(0, hg_w, COL_CHUNK):
        y = proj(base + c0)
        _store_heads(hq_ref, c0, y * _sigmoid(y))
    lbp = lbp_ref[...]
    e = jnp.exp(lbp - jnp.max(lbp, axis=0, keepdims=True))
    lb = jnp.sum(e[0:layer + 1], axis=0, keepdims=True) / jnp.sum(e, axis=0, keepdims=True)
    base += hg_w
    for c0 in range(0, hg_w, COL_CHUNK):
        lbc = lb[:, c0:c0 + COL_CHUNK]
        f = lbc + (1.0 - lbc) * _sigmoid(proj(base + c0))
        _store_heads(logf_ref, c0, jnp.log(f))
        _store_heads(hk_ref, c0, 1.0 - f)
    base += hg_w
    for c0 in range(0, hg_w, COL_CHUNK):
        _store_heads(hv_ref, c0, proj(base + c0))
    base += hg_w
    for c0 in range(0, hg_w, COL_CHUNK):
        y = proj(base + c0)
        _store_heads(og_ref, c0, y * _sigmoid(y))
    base += hg_w
    for c0 in range(0, 2 * d, COL_CHUNK):
        gates_ref[:, c0:c0 + COL_CHUNK] = _sigmoid(proj(base + c0)).astype(act)


def _in_projection(x3, mod3, w_in, lb_param, layer, gb, rb, act, hosted=(None, None, None)):
    g, r, d = x3.shape
    n = g * r
    tm = gb * rb
    hg_w = lb_param.shape[1]
    rt = r // rb
    heads = hg_w // HG_DK
    rows = lambda width: pl.BlockSpec((tm, width), lambda gi, ri: (gi * rt + ri, 0))
    by_head = pl.BlockSpec((heads, tm, HG_DK), lambda gi, ri: (0, gi * rt + ri, 0))
    head_major = lambda dtype: jax.ShapeDtypeStruct((heads, n, HG_DK), dtype)
    out_shapes = (
        jax.ShapeDtypeStruct((n, 3 * ATTN_W), act),
        head_major(act),
        head_major(act),
        head_major(act),
        head_major(F32),
        head_major(act),
        jax.ShapeDtypeStruct((n, 2 * d), act),
    )
    rp = 2 * HEADS_PER_GROUP
    kv_rows = tuple(min(window, r) for window, dil in ATTN_GROUPS)
    kv_specs = []
    for kept in kv_rows:
        assert (r - max(kept, rb)) % rb == 0
        first_tile = (r - max(kept, rb)) // rb
        kv_specs.append(pl.BlockSpec((gb, min(kept, rb) * rp, HEAD_DIM),
                                     lambda gi, ri, first_tile=first_tile: (gi, jnp.maximum(ri - first_tile, 0), 0)))
    out_shapes += tuple(jax.ShapeDtypeStruct((g, kept * rp, HEAD_DIM), F32) for kept in kv_rows)
    return _hosting_call(
        functools.partial(_inproj_kernel, layer=layer, seq_rows=r, kv_rows=kv_rows), *hosted,
        out_shape=out_shapes,
        grid=(g // gb, rt),
        in_specs=[pl.BlockSpec((gb, rb, d), lambda gi, ri: (gi, ri, 0)),
                  pl.BlockSpec((gb, 1, mod3.shape[2]), lambda gi, ri: (gi, 0, 0)),
                  _resident(w_in.shape),
                  _resident(lb_param.shape)],
        out_specs=(rows(3 * ATTN_W), by_head, by_head, by_head, by_head, by_head, rows(2 * d), *kv_specs),
        args=(x3, mod3, w_in, lb_param),
        compiler_params=_cparams("arbitrary", "arbitrary"),
        name="in_projection",
    )


ATTN_BLOCKS_PER_ITER = 4


def _attn_prompt_kernel(q_ref, k_ref, v_ref, attn_ref, qf, kf, vf, mf, lf, af, *, head_axis, group_axis):
    s = attn_ref.shape[0]
    nblk = s // BAND
    head = pl.program_id(head_axis)
    group = pl.program_id(group_axis)
    qi = lax.broadcasted_iota(jnp.int32, (BAND, BAND), 0)
    kj = lax.broadcasted_iota(jnp.int32, (BAND, BAND), 1)
    steps_own = qi - kj
    steps_prev = steps_own + BAND
    scale = HEAD_DIM ** -0.5
    ones = jnp.ones((BAND, HEAD_DIM), BF16)
    qf[...] = q_ref[...].astype(F32)
    kf[...] = k_ref[...].astype(F32)
    vf[...] = v_ref[...].astype(F32)
    for g, (window, dil) in enumerate(ATTN_GROUPS):
        nb = nblk // dil
        slope = jnp.float32(_alibi_slope(g, HEADS_PER_GROUP - 1) * dil)
        for h in range(HEADS_PER_GROUP - 1):
            slope = jnp.where(head == h, jnp.float32(_alibi_slope(g, h) * dil), slope)
        bias_own = jnp.where(steps_own >= 0, -slope * steps_own.astype(F32), NEG)
        bias_prev = jnp.where(steps_prev <= BAND, -slope * steps_prev.astype(F32), NEG)

        def body(it, carry, g=g, dil=dil, nb=nb, bias_own=bias_own, bias_prev=bias_prev):
            blocks = []
            for u in range(ATTN_BLOCKS_PER_ITER):
                idx = it * ATTN_BLOCKS_PER_ITER + u
                r = idx // nb
                i = idx - r * nb
                start = r + i * (BAND * dil)
                start_prev = r + jnp.maximum(i - 1, 0) * (BAND * dil)
                rows = pl.ds(start, BAND, stride=dil) if dil > 1 else pl.ds(start, BAND)
                rows_prev = pl.ds(start_prev, BAND, stride=dil) if dil > 1 else pl.ds(start_prev, BAND)
                blocks.append((rows, rows_prev, i > 0))
            scores = []
            for rows, rows_prev, has_prev in blocks:
                q = qf[rows, :].astype(BF16)
                scores.append((lax.dot_general(q, kf[rows, :].astype(BF16), NT_DIMS, preferred_element_type=F32),
                               lax.dot_general(q, kf[rows_prev, :].astype(BF16), NT_DIMS,
                                               preferred_element_type=F32)))
            probs = []
            for (rows, rows_prev, has_prev), (s_own, s_prev) in zip(blocks, scores):
                s_own = s_own * scale + bias_own
                s_prev = jnp.where(has_prev, s_prev * scale + bias_prev, NEG)
                m = jnp.max(jnp.maximum(s_own, s_prev), axis=-1, keepdims=True)
                probs.append((m, jnp.exp(s_own - m).astype(BF16), jnp.exp(s_prev - m).astype(BF16)))
            results = []
            for (rows, rows_prev, has_prev), (m, p_own, p_prev) in zip(blocks, probs):
                both = (jnp.dot(p_own, jnp.concatenate([vf[rows, :].astype(BF16), ones], axis=1),
                                preferred_element_type=F32)
                        + jnp.dot(p_prev, jnp.concatenate([vf[rows_prev, :].astype(BF16), ones], axis=1),
                                  preferred_element_type=F32))
                acc, l = both[:, :HEAD_DIM], both[:, HEAD_DIM:]
                if g < N_GROUPS - 1:
                    m_old = mf[rows, :]
                    m_new = jnp.maximum(m_old, m)
                    a_old = jnp.exp(m_old - m_new)
                    a_blk = jnp.exp(m - m_new)
                    l = a_old * lf[rows, :] + a_blk * l
                    acc = a_old * af[rows, :] + a_blk * acc
                    m = m_new
                results.append((rows, m, l, acc))
            for rows, m, l, acc in results:
                mf[rows, :] = jnp.broadcast_to(m, (BAND, HEAD_DIM))
                lf[rows, :] = l
                af[rows, :] = acc
            return carry

        @pl.when(group == N_GROUPS - 1 - g)
        def _(body=body):
            lax.fori_loop(0, nblk // ATTN_BLOCKS_PER_ITER, body, 0)

    @pl.when(group == N_GROUPS - 1)
    def _():
        attn_ref[...] = (af[...] / lf[...]).astype(attn_ref.dtype)


def _attention_prompt(qkv, b, s, hosted=(None, None, None)):
    hpq = ATTN_W // HEAD_DIM
    in_specs = [pl.BlockSpec((s, HEAD_DIM), lambda bi, h, g, section=section:
                             (bi, section * hpq + (N_GROUPS - 1 - g) * HEADS_PER_GROUP + h))
                for section in range(3)]
    (attn,), dst = _hosting_call(
        functools.partial(_attn_prompt_kernel, head_axis=1, group_axis=2), *hosted,
        out_shape=(jax.ShapeDtypeStruct((b * s, GROUP_W), BF16),),
        grid=(b, HEADS_PER_GROUP, N_GROUPS),
        in_specs=in_specs,
        out_specs=(pl.BlockSpec((s, HEAD_DIM), lambda bi, h, g: (bi, h)),),
        scratch_shapes=[pltpu.VMEM((s, HEAD_DIM), F32) for _ in range(6)],
        args=[qkv] * 3,
        compiler_params=_cparams("arbitrary", "arbitrary", "arbitrary"),
        name="attn_prompt",
    )
    return attn, dst


def _attn_sample_kernel(*refs, group, dil, n_steps, length, seqs):
    first = group == 0
    last = group == N_GROUPS - 1
    q_ref, newkv_ref, cache_ref = refs[:3]
    pos = 3
    if not first:
        m_in, l_in, a_in = refs[pos:pos + 3]
        pos += 3
    pos += 1
    newc_ref = refs[pos]
    outs = refs[pos + 1:]
    t = q_ref.shape[0] // seqs
    rp = 2 * HEADS_PER_GROUP
    stretch = max(dil, t)
    nkeys = length // stretch * t
    j_c = lax.broadcasted_iota(jnp.int32, (t, nkeys), 0)
    c_c = lax.broadcasted_iota(jnp.int32, (t, nkeys), 1)
    dist_c = length + j_c - ((c_c // t) * stretch + (c_c & (t - 1)))
    ok_c = ((dist_c & (dil - 1)) == 0) & (dist_c <= n_steps * dil)
    j_n = lax.broadcasted_iota(jnp.int32, (t, t), 0)
    r_n = lax.broadcasted_iota(jnp.int32, (t, t), 1)
    dist_n = j_n - r_n
    ok_n = (dist_n >= 0) & ((dist_n & (dil - 1)) == 0) & (dist_n <= n_steps * dil)
    scale = HEAD_DIM ** -0.5
    qoff = group * GROUP_W
    cache_rows = lambda s, first_row: cache_ref[s, :, pl.ds(first_row, t, stride=rp), :].reshape(
        nkeys, HEAD_DIM).astype(BF16)
    new_rows = lambda s, first_row: newkv_ref[s, pl.ds(first_row, t, stride=rp), :].astype(BF16)
    chains = [(s, h) for s in range(seqs) for h in range(HEADS_PER_GROUP)]
    scores = []
    for s, h in chains:
        q = q_ref[s * t:(s + 1) * t, qoff + h * HEAD_DIM:qoff + (h + 1) * HEAD_DIM].astype(BF16)
        scores.append((lax.dot_general(q, cache_rows(s, h), NT_DIMS, preferred_element_type=F32),
                       lax.dot_general(q, new_rows(s, h), NT_DIMS, preferred_element_type=F32)))
    probs = []
    for (s, h), (s_c, s_n) in zip(chains, scores):
        rows, sl = slice(s * t, (s + 1) * t), slice(h * HEAD_DIM, (h + 1) * HEAD_DIM)
        slope = _alibi_slope(group, h)
        s_c = jnp.where(ok_c, s_c * scale - slope * dist_c.astype(F32), NEG)
        s_n = jnp.where(ok_n, s_n * scale - slope * dist_n.astype(F32), NEG)
        m = jnp.maximum(jnp.max(s_c, axis=-1, keepdims=True), jnp.max(s_n, axis=-1, keepdims=True))
        if not first:
            m = jnp.maximum(m, m_in[rows, sl][:, 0:1])
        p_c = jnp.exp(s_c - m)
        p_n = jnp.exp(s_n - m)
        l = jnp.sum(p_c, axis=-1, keepdims=True) + jnp.sum(p_n, axis=-1, keepdims=True)
        probs.append((m, l, p_c.astype(BF16), p_n.astype(BF16)))
    for (s, h), (m, l, p_c, p_n) in zip(chains, probs):
        rows, sl = slice(s * t, (s + 1) * t), slice(h * HEAD_DIM, (h + 1) * HEAD_DIM)
        acc = (jnp.dot(p_c, cache_rows(s, HEADS_PER_GROUP + h), preferred_element_type=F32)
               + jnp.dot(p_n, new_rows(s, HEADS_PER_GROUP + h), preferred_element_type=F32))
        if not first:
            a_old = jnp.exp(m_in[rows, sl] - m)
            l = a_old * l_in[rows, sl] + l
            acc = a_old * a_in[rows, sl] + acc
        if last:
            outs[0][rows, sl] = acc / l
        else:
            outs[0][rows, sl] = jnp.broadcast_to(m, (t, HEAD_DIM))
            outs[1][rows, sl] = jnp.broadcast_to(l, (t, HEAD_DIM))
            outs[2][rows, sl] = acc
    for s in range(seqs):
        newc_ref[s] = newkv_ref[s]


SAMPLE_SEQS_PER_STEP = (8, 4, 2)


def _attention_sample(qkv, new_kv, cache, shifted, layer, running, t, group):
    window, dil = ATTN_GROUPS[group]
    depth, bd, crow, dh = cache.shape
    rp = 2 * HEADS_PER_GROUP
    length = crow // rp
    stretch = max(dil, t)
    seqs = math.gcd(SAMPLE_SEQS_PER_STEP[group], bd)
    first = group == 0
    last = group == N_GROUPS - 1
    rows = pl.BlockSpec((seqs * t, GROUP_W), lambda bi: (bi, 0))
    stat = jax.ShapeDtypeStruct((bd * t, GROUP_W), F32)
    new_blk = pl.BlockSpec((seqs, t * rp, dh), lambda bi: (bi, 0, 0))
    in_specs = [pl.BlockSpec((seqs * t, 3 * ATTN_W), lambda bi: (bi, 0)),
                new_blk,
                pl.BlockSpec((None, seqs, length // stretch, t * rp, dh), lambda bi: (layer, bi, 0, 0, 0))]
    args = [qkv, new_kv, cache.reshape(depth, bd, length // stretch, stretch * rp, dh)]
    if not first:
        in_specs += [rows, rows, rows]
        args += list(running)
    in_specs.append(pl.BlockSpec(memory_space=pl.ANY))
    args.append(shifted)
    n_out = 1 if last else 3
    res = pl.pallas_call(
        functools.partial(_attn_sample_kernel, group=group, dil=dil, n_steps=window // dil, length=length,
                          seqs=seqs),
        out_shape=(jax.ShapeDtypeStruct(shifted.shape, shifted.dtype),) + (stat,) * n_out,
        grid=(bd // seqs,),
        in_specs=in_specs,
        out_specs=(pl.BlockSpec((seqs, t * rp, dh), lambda bi: (bi, length // t - 1, 0)),) + (rows,) * n_out,
        input_output_aliases={len(args) - 1: 0},
        compiler_params=_cparams("arbitrary"),
        name=f"attn_sample_g{group}",
    )(*args)
    return res[0], (res[1] if last else tuple(res[1:]))


def _cumsum_rows(x, row):
    d = 1
    while d < x.shape[0]:
        x = x + jnp.where(row >= d, pltpu.roll(x, d, 0), 0.0)
        d *= 2
    return x


def _hgrn_carry(q, k, v, g, st):
    c = q.shape[0]
    qg = (q * jnp.exp(g)).astype(BF16)
    o = lax.dot_general(qg, st.astype(BF16), NT_DIMS, preferred_element_type=F32)
    g_last = g[c - 1:c, :]
    kd = (k * jnp.exp(g_last - g)).astype(BF16)
    st_new = st * jnp.exp(g_last) + lax.dot_general(v.astype(BF16), kd, TN_DIMS, preferred_element_type=F32)
    return o, st_new


def _hgrn_chunks_pairwise(qs, ks, vs, lfs, states):
    c = qs[0].shape[0]
    row = lax.broadcasted_iota(jnp.int32, qs[0].shape, 0)
    gs = [_cumsum_rows(lf, row) for lf in lfs]
    outs = []
    for q, k, v, g in zip(qs, ks, vs, gs):
        o = jnp.sum(q * k, axis=-1, keepdims=True) * v
        for d in range(1, c):
            ok = row >= d
            e = jnp.exp(jnp.where(ok, g - pltpu.roll(g, d, 0), 0.0))
            a = jnp.sum(jnp.where(ok, q * pltpu.roll(k, d, 0) * e, 0.0), axis=-1, keepdims=True)
            o = o + a * pltpu.roll(v, d, 0)
        outs.append(o)
    cols = []
    for k, g in zip(ks, gs):
        g_last = g[c - 1:c, :]
        kd = k * jnp.exp(g_last - g)
        cols.append(jnp.concatenate([kd, jnp.broadcast_to(jnp.exp(g_last), kd.shape)], axis=0).T)
    carried = [jnp.dot((q * jnp.exp(g)).astype(BF16), s.astype(BF16), preferred_element_type=F32)
               for q, g, s in zip(qs, gs, states)]
    new_states = [s * col[:, c:c + 1] + jnp.dot(col[:, 0:c].astype(BF16), v.astype(BF16),
                                                preferred_element_type=F32)
                  for s, col, v in zip(states, cols, vs)]
    return [o + oc for o, oc in zip(outs, carried)], new_states


def _pair_level(c):
    ri = lax.broadcasted_iota(jnp.int32, (c, c), 0)
    ci = lax.broadcasted_iota(jnp.int32, (c, c), 1)
    x = ri ^ ci
    bits = jnp.zeros((c, c), jnp.int32)
    p = 1
    while p < c:
        bits = bits + (x >= p).astype(jnp.int32)
        p *= 2
    return jnp.where(ri > ci, bits + 1, jnp.where(ri == ci, 1, 0))


def _hgrn_chunks_blocked(qs, ks, vs, lfs, sts, g_refs, level):
    nh = len(qs)
    c = qs[0].shape[0]
    row = lax.broadcasted_iota(jnp.int32, (c, HG_DK), 0)
    sub = row & (SUBLANES - 1)
    sub8 = lax.broadcasted_iota(jnp.int32, (SUBLANES, HG_DK), 0)
    gs = []
    for lf, g_ref in zip(lfs, g_refs):
        x = lf
        for d in (1, 2, 4):
            x = x + jnp.where(sub >= d, pltpu.roll(x, d, 0), 0.0)
        g_ref[...] = x
        pieces = [x[0:SUBLANES]]
        off = None
        for j in range(1, c // SUBLANES):
            tot = jnp.broadcast_to(g_ref[j * SUBLANES - 1:j * SUBLANES, :], (SUBLANES, HG_DK))
            off = tot if off is None else off + tot
            pieces.append(x[j * SUBLANES:(j + 1) * SUBLANES] + off)
        g = jnp.concatenate(pieces, axis=0)
        g_ref[...] = g
        gs.append(g)

    def mid_rows(g_ref, n):
        half = n // 2
        bcast = lambda r: jnp.broadcast_to(g_ref[r:r + 1, :], (SUBLANES, HG_DK))
        out = []
        for j in range(c // SUBLANES):
            base = j * SUBLANES
            if n >= SUBLANES:
                out.append(bcast(base // n * n + half - 1))
            else:
                assert n == 4
                out.append(jnp.where(sub8 < 4, bcast(base + 1), bcast(base + 5)))
        return jnp.concatenate(out, axis=0)

    pair = lambda y, z: lax.dot_general(y, z, NT_DIMS, preferred_element_type=F32)
    prods = [[pair(q.astype(BF16), k.astype(BF16))] for q, k in zip(qs, ks)]
    n = 2
    while n <= c:
        second = (row & (n // 2)) != 0
        for u in range(nh):
            if n == 2:
                arg = jnp.where(second, lfs[u], 0.0)
            else:
                gmid = mid_rows(g_refs[u], n)
                arg = jnp.where(second, gs[u] - gmid, gmid - gs[u])
            y = (jnp.where(second, qs[u], ks[u]) * jnp.exp(arg)).astype(BF16)
            prods[u].append(pair(y, y))
        n *= 2
    outs = []
    for u in range(nh):
        amat = jnp.where(level == 1, prods[u][0], 0.0)
        for j, p in enumerate(prods[u][1:]):
            amat = jnp.where(level == j + 2, p, amat)
        outs.append(jnp.dot(amat.astype(BF16), vs[u].astype(BF16), preferred_element_type=F32))
    states = []
    for u in range(nh):
        o_carry, st_new = _hgrn_carry(qs[u], ks[u], vs[u], gs[u], sts[u])
        outs[u] = outs[u] + o_carry
        states.append(st_new)
    return outs, states


def _hgrn_output(o, og, nw):
    return o * lax.rsqrt(jnp.mean(o * o, axis=-1, keepdims=True) + RMS_EPS) * nw * og


HGRN_HEADS_PER_ITER = 4


HGRN_CHUNKS_PER_STEP = 2


def _hgrn_prompt_kernel(q_ref, k_ref, v_ref, lf_ref, og_ref, nw_ref, hg_ref, sfin_ref, st_ref, g_ref):
    ci = pl.program_id(1)
    heads = q_ref.shape[0]
    c = g_ref.shape[1]
    head_groups = heads // HGRN_HEADS_PER_ITER

    @pl.when(ci == 0)
    def _():
        st_ref[...] = jnp.zeros_like(st_ref)

    level = _pair_level(c)

    def head_group(it, carry):
        i = it % head_groups
        rows = pl.ds(pl.multiple_of(it // head_groups * c, c), c)
        hs = [i * HGRN_HEADS_PER_ITER + u for u in range(HGRN_HEADS_PER_ITER)]
        load = lambda ref: [ref[h, rows, :].astype(F32) for h in hs]
        outs, states = _hgrn_chunks_blocked(load(q_ref), load(k_ref), load(v_ref), load(lf_ref),
                                            [st_ref[h] for h in hs],
                                            [g_ref.at[u] for u in range(HGRN_HEADS_PER_ITER)], level)
        for h, o, st_new in zip(hs, outs, states):
            st_ref[h] = st_new
            hg_ref[h, rows, :] = _hgrn_output(o, og_ref[h, rows, :].astype(F32), nw_ref[...]).astype(hg_ref.dtype)
        return carry

    lax.fori_loop(0, q_ref.shape[1] // c * head_groups, head_group, 0)

    @pl.when(ci == pl.num_programs(1) - 1)
    def _():
        for h in range(heads):
            sfin_ref[0, h] = st_ref[h].T


def _hgrn_prompt(hq, hk, hv, logf, og, norm_w, b, s, chunk, hosted=(None, None, None)):
    heads, n, _ = hq.shape
    rows = chunk * math.gcd(HGRN_CHUNKS_PER_STEP, s // chunk)
    nc = s // rows
    blk = pl.BlockSpec((heads, rows, HG_DK), lambda bi, ci: (0, bi * nc + ci, 0))
    (hg, s_fin), dst = _hosting_call(
        _hgrn_prompt_kernel, *hosted,
        out_shape=(jax.ShapeDtypeStruct((heads, n, HG_DV), BF16),
                   jax.ShapeDtypeStruct((b, heads, HG_DK, HG_DV), F32)),
        grid=(b, nc),
        in_specs=[blk, blk, blk, blk, blk, pl.BlockSpec((1, HG_DV), lambda bi, ci: (0, 0))],
        out_specs=(blk, pl.BlockSpec((1, heads, HG_DK, HG_DV), lambda bi, ci: (bi, 0, 0, 0))),
        scratch_shapes=[pltpu.VMEM((heads, HG_DV, HG_DK), F32),
                        pltpu.VMEM((HGRN_HEADS_PER_ITER, chunk, HG_DK), F32)],
        args=(hq, hk, hv, logf, og, norm_w.reshape(1, HG_DV)),
        compiler_params=_cparams("arbitrary", "arbitrary"),
        name="hgrn_prompt",
    )
    return hg, s_fin, dst


HGRN_SAMPLE_SEQS_PER_STEP = 2


def _hgrn_sample_kernel(q_ref, k_ref, v_ref, lf_ref, og_ref, nw_ref, s_ref, hg_ref, snew_ref):
    seqs, heads = s_ref.shape[0], s_ref.shape[1]
    t = q_ref.shape[1] // seqs
    chains = [(s, h) for s in range(seqs) for h in range(heads)]
    load = lambda ref: [ref[h, s * t:(s + 1) * t, :] for s, h in chains]
    outs, new_states = _hgrn_chunks_pairwise(load(q_ref), load(k_ref), load(v_ref), load(lf_ref),
                                             [s_ref[s, h] for s, h in chains])
    for (s, h), o, st_new in zip(chains, outs, new_states):
        snew_ref[s, h] = st_new
        hg_ref[h, s * t:(s + 1) * t, :] = _hgrn_output(o, og_ref[h, s * t:(s + 1) * t, :], nw_ref[...])


def _hgrn_sample(hq, hk, hv, logf, og, norm_w, state, t):
    bd, heads = state.shape[0], state.shape[1]
    seqs = math.gcd(HGRN_SAMPLE_SEQS_PER_STEP, bd)
    blk = pl.BlockSpec((heads, seqs * t, HG_DK), lambda bi: (0, bi, 0))
    sblk = pl.BlockSpec((seqs, heads, HG_DK, HG_DV), lambda bi: (bi, 0, 0, 0))
    return pl.pallas_call(
        _hgrn_sample_kernel,
        out_shape=(jax.ShapeDtypeStruct((heads, bd * t, HG_DV), F32),
                   jax.ShapeDtypeStruct(state.shape, state.dtype)),
        grid=(bd // seqs,),
        in_specs=[blk, blk, blk, blk, blk, pl.BlockSpec((1, HG_DV), lambda bi: (0, 0)), sblk],
        out_specs=(blk, sblk),
        compiler_params=_cparams("arbitrary"),
        name="hgrn_sample",
    )(hq, hk, hv, logf, og, norm_w.reshape(1, HG_DV), state)


def _merge_kernel(x_ref, mod_ref, attn_ref, hg_ref, gates_ref, wa_ref, wb_ref, wo_ref, lng_ref, lnb_ref,
                  x1_ref, *, alpha):
    gb, rb, d = x_ref.shape
    tm = gb * rb
    hg = jnp.concatenate([hg_ref[h].astype(BF16) for h in range(hg_ref.shape[0])], axis=1)
    branch_a = jnp.dot(attn_ref[...].astype(BF16), wa_ref[...], preferred_element_type=F32)
    branch_b = jnp.dot(hg, wb_ref[...], preferred_element_type=F32)
    gate_a = gates_ref[:, 0:d].astype(F32)
    gate_b = gates_ref[:, d:2 * d].astype(F32)
    merged = gate_a * branch_a + gate_b * branch_b
    mix = jnp.dot(merged.astype(BF16), wo_ref[...], preferred_element_type=F32)
    g1 = mod_ref[:, :, 2 * d:3 * d]
    h = alpha * x_ref[...] + g1 * mix.reshape(gb, rb, d)
    x1_ref[...] = _layer_norm(h, lng_ref[...], lnb_ref[...])


def _merge(x3, mod3, attn, hg, gates, wa, wb, wo, ln_g, ln_b, gb, rb, alpha, hosted=(None, None, None)):
    g, r, d = x3.shape
    tm = gb * rb
    rt = r // rb
    row = lambda gi, ri: (gi * rt + ri, 0)
    rows = lambda a: pl.BlockSpec((tm, a.shape[1]), row)
    tile3 = pl.BlockSpec((gb, rb, d), lambda gi, ri: (gi, ri, 0))
    (x1,), dst = _hosting_call(
        functools.partial(_merge_kernel, alpha=alpha), *hosted,
        out_shape=(jax.ShapeDtypeStruct(x3.shape, F32),),
        grid=(g // gb, rt),
        in_specs=[tile3, pl.BlockSpec((gb, 1, mod3.shape[2]), lambda gi, ri: (gi, 0, 0)),
                  rows(attn), pl.BlockSpec((hg.shape[0], tm, hg.shape[2]), lambda gi, ri: (0, gi * rt + ri, 0)),
                  rows(gates),
                  _resident(wa.shape), _resident(wb.shape), _resident(wo.shape),
                  _resident((1, d)), _resident((1, d))],
        out_specs=(tile3,),
        args=(x3, mod3, attn, hg, gates, wa, wb, wo, ln_g.reshape(1, d), ln_b.reshape(1, d)),
        compiler_params=_cparams("arbitrary", "arbitrary"),
        name="merge",
    )
    return x1, dst


def _mlp_kernel(x1_ref, mod_ref, wu_ref, bu_ref, wd_ref, bd_ref, lng_ref, lnb_ref, x2_ref, *, alpha):
    gb, rb, d = x1_ref.shape
    tm = gb * rb
    dff = wu_ref.shape[1]
    x1 = x1_ref[...]
    sh = mod_ref[:, :, 3 * d:4 * d]
    sc = mod_ref[:, :, 4 * d:5 * d]
    g2 = mod_ref[:, :, 5 * d:6 * d]
    u = (x1 * (1.0 + sc) + sh).reshape(tm, d).astype(BF16)
    hidden = []
    for c0 in range(0, dff, COL_CHUNK):
        hid = jnp.dot(u, wu_ref[:, c0:c0 + COL_CHUNK], preferred_element_type=F32) + bu_ref[:, c0:c0 + COL_CHUNK]
        hidden.append(jnp.square(jnp.maximum(hid, 0.0)).astype(BF16))
    ff = jnp.dot(jnp.concatenate(hidden, axis=1), wd_ref[...], preferred_element_type=F32) + bd_ref[...]
    h = alpha * x1 + g2 * ff.reshape(gb, rb, d)
    x2_ref[...] = _layer_norm(h, lng_ref[...], lnb_ref[...])


def _mlp(x3, mod3, wu, bu, wd, bd, ln_g, ln_b, gb, rb, alpha, hosted=(None, None, None)):
    g, r, d = x3.shape
    dff = wu.shape[1]
    tile3 = pl.BlockSpec((gb, rb, d), lambda gi, ri: (gi, ri, 0))
    (x2,), dst = _hosting_call(
        functools.partial(_mlp_kernel, alpha=alpha), *hosted,
        out_shape=(jax.ShapeDtypeStruct(x3.shape, F32),),
        grid=(g // gb, r // rb),
        in_specs=[tile3, pl.BlockSpec((gb, 1, mod3.shape[2]), lambda gi, ri: (gi, 0, 0)),
                  _resident(wu.shape), _resident((1, dff)), _resident(wd.shape), _resident((1, d)),
                  _resident((1, d)), _resident((1, d))],
        out_specs=(tile3,),
        args=(x3, mod3, wu, bu.reshape(1, dff), wd, bd.reshape(1, d), ln_g.reshape(1, d), ln_b.reshape(1, d)),
        compiler_params=_cparams("arbitrary", "arbitrary"),
        name="mlp",
    )
    return x2, dst


PROMPT_ROWS = 256
SAMPLE_SEQS = 32
HGRN_CHUNK = 128


def _prompt_layer(x, mod, caches, t_new, layer, w, alpha):
    b, s, d = x.shape
    mod3 = mod.reshape(b, 1, mod.shape[1])
    gb, rb = 1, min(PROMPT_ROWS, s)
    chunk = min(HGRN_CHUNK, s)
    rp = 2 * HEADS_PER_GROUP
    bd = caches[0].shape[1]
    tile_steps = s // rb * b
    hgrn_steps = s // chunk // math.gcd(HGRN_CHUNKS_PER_STEP, s // chunk) * b

    def plan(gi, b0, nbatch, n_steps):
        rows = caches[gi].shape[2]
        return _plan_shift(layer, b0, nbatch, n_steps, t_new * rp, rows - t_new * rp)

    attn_steps = b * HEADS_PER_GROUP * N_GROUPS
    n_hgrn = bd // 4
    n_attn = (bd - n_hgrn) // 2
    big = None

    def share(b0, nbatch, n_steps):
        return (plan(2, b0, nbatch, n_steps), caches[2], big) if nbatch else (None, None, big)

    (qkv, hq, hk, hv, logf, og, gates, *kv_tails), mid = _in_projection(
        x, mod3, w["w_in"], w["lb_param"], layer, gb, rb, BF16,
        hosted=(plan(1, 0, bd, tile_steps), caches[1], None))
    hg, s_fin, big = _hgrn_prompt(hq, hk, hv, logf, og, w["hg_norm_w"], b, s, chunk,
                                  hosted=share(0, n_hgrn, hgrn_steps))
    attn, big = _attention_prompt(qkv, b, s, hosted=share(n_hgrn, n_attn, attn_steps))
    new_bufs = [kv.reshape(b, kv.shape[1] // rp, 2, HEADS_PER_GROUP, HEAD_DIM) for kv in kv_tails]
    x1, small = _merge(x, mod3, attn, hg, gates, w["w_branch_a"], w["w_branch_b"], w["w_out"],
                       w["ln1_g"], w["ln1_b"], gb, rb, alpha, hosted=(plan(0, 0, bd, tile_steps), caches[0], None))
    x2, big = _mlp(x1, mod3, w["w_up"], w["b_up"], w["w_down"], w["b_down"], w["ln2_g"], w["ln2_b"], gb, rb,
                   alpha, hosted=share(n_hgrn + n_attn, bd - n_hgrn - n_attn, tile_steps))
    return x2, new_bufs, s_fin, (small, mid, big)


def _sample_layer(x, mod, caches, shifted, states, layer, w, alpha):
    b, s, d = x.shape
    mod3 = mod.reshape(b, 1, mod.shape[1])
    gb, rb = min(SAMPLE_SEQS, b), s
    rp = 2 * HEADS_PER_GROUP
    (qkv, hq, hk, hv, logf, og, gates, *new_kv), _ = _in_projection(
        x, mod3, w["w_in"], w["lb_param"], layer, gb, rb, F32)
    new_bufs = []
    running = None
    for gi in range(N_GROUPS):
        buf, running = _attention_sample(qkv, new_kv[gi], caches[gi], shifted[gi], layer, running, s, gi)
        new_bufs.append(buf)
    hg, s_fin = _hgrn_sample(hq, hk, hv, logf, og, w["hg_norm_w"], states[layer], s)
    x1, _ = _merge(x, mod3, running, hg, gates, w["w_branch_a"], w["w_branch_b"], w["w_out"],
                   w["ln1_g"], w["ln1_b"], gb, rb, alpha)
    x2, _ = _mlp(x1, mod3, w["w_up"], w["b_up"], w["w_down"], w["b_down"], w["ln2_g"], w["ln2_b"], gb, rb, alpha)
    return x2, new_bufs, s_fin


def _stack_layers(per_layer):
    return per_layer[0][None] if len(per_layer) == 1 else jnp.stack(per_layer)


def kernel(x_prompt, x_sample, c_prompt, c_sample, cache_kv_w128, cache_kv_w512, cache_kv_w2048, state_hgrn,
           w_ada, b_ada, w_in, lb_param, hg_norm_w, w_branch_a, w_branch_b, w_out, ln1_g, ln1_b, w_up, b_up,
           w_down, b_down, ln2_g, ln2_b):
    depth = w_ada.shape[0]
    alpha = (2 * depth) ** 0.25
    caches = (cache_kv_w128, cache_kv_w512, cache_kv_w2048)
    for (window, dil), cache in zip(ATTN_GROUPS, caches):
        assert window // dil == BAND and cache.shape[2] == window
    assert x_prompt.shape[1] % (BAND * ATTN_GROUPS[-1][1]) == 0 and x_sample.shape[1] == SUBLANES
    views = tuple(c.reshape(c.shape[0], c.shape[1], c.shape[2] * c.shape[3] * c.shape[4], c.shape[5])
                  for c in caches)
    nb = c_prompt.shape[0]
    yp, ys = x_prompt, x_sample
    p_bufs, p_states, s_bufs, s_states = [], [], [], []
    for l in range(depth):
        w = dict(w_in=w_in[l].astype(BF16), lb_param=lb_param, hg_norm_w=hg_norm_w[l],
                 w_branch_a=w_branch_a[l].astype(BF16), w_branch_b=w_branch_b[l].astype(BF16),
                 w_out=w_out[l].astype(BF16), ln1_g=ln1_g[l], ln1_b=ln1_b[l],
                 w_up=w_up[l].astype(BF16), b_up=b_up[l], w_down=w_down[l].astype(BF16), b_down=b_down[l],
                 ln2_g=ln2_g[l], ln2_b=ln2_b[l])
        mod = _modulation(jnp.concatenate([c_prompt, c_sample], axis=0), w_ada[l].astype(BF16), b_ada[l])
        yp, bufs_p, st_p, shifted = _prompt_layer(yp, mod[:nb], views, x_sample.shape[1], l, w, alpha)
        ys, bufs_s, st_s = _sample_layer(ys, mod[nb:], views, shifted, state_hgrn, l, w, alpha)
        p_bufs.append(bufs_p)
        p_states.append(st_p)
        s_bufs.append([buf.reshape(c.shape[1:]) for buf, c in zip(bufs_s, caches)])
        s_states.append(st_s)
    group = lambda bufs, gi: _stack_layers([bl[gi] for bl in bufs])
    return (yp, ys, group(p_bufs, 0), group(p_bufs, 1), group(p_bufs, 2), _stack_layers(p_states),
            group(s_bufs, 0), group(s_bufs, 1), group(s_bufs, 2), _stack_layers(s_states))
```

```python
import functools
import math
from typing import NamedTuple

import jax
import jax.numpy as jnp
from jax import lax
from jax.experimental import pallas as pl
from jax.experimental.pallas import tpu as pltpu

F32 = jnp.float32
BF16 = jnp.bfloat16

ATTN_GROUPS = ((128, 1), (512, 4), (2048, 16))
N_GROUPS = len(ATTN_GROUPS)
HEADS_PER_GROUP = 4
HEAD_DIM = 128
N_ATTN_HEADS = N_GROUPS * HEADS_PER_GROUP
ATTN_W = N_ATTN_HEADS * HEAD_DIM
GROUP_W = HEADS_PER_GROUP * HEAD_DIM
BAND = 128
HG_DK = 128
HG_DV = 128
LN_EPS = 1e-5
RMS_EPS = 1e-6
NEG = -1e30

SUBLANES = 8
LANES = 128
VMEM_LIMIT_BYTES = 56 * 1024 * 1024
COL_CHUNK = 512

NT_DIMS = (((1,), (1,)), ((), ()))
TN_DIMS = (((0,), (0,)), ((), ()))


def _alibi_slope(group, head):
    return 2.0 ** (-8.0 * (group * HEADS_PER_GROUP + head + 1) / N_ATTN_HEADS)


def _cparams(*sem):
    return pltpu.CompilerParams(dimension_semantics=sem, vmem_limit_bytes=VMEM_LIMIT_BYTES)


def _resident(shape):
    nd = len(shape)
    return pl.BlockSpec(shape, lambda *_: (0,) * nd, pipeline_mode=pl.Buffered(1))


def _sigmoid(x):
    return 1.0 / (1.0 + jnp.exp(-x))


def _layer_norm(h, g, b):
    mu = jnp.mean(h, axis=-1, keepdims=True)
    hc = h - mu
    var = jnp.mean(hc * hc, axis=-1, keepdims=True)
    return hc * lax.rsqrt(var + LN_EPS) * g + b


class _Shift(NamedTuple):
    layer: int
    b0: int
    nbatch: int
    cb: int
    rsplit: int
    k: int
    drop: int
    keep: int

    @property
    def n_chunks(self):
        return self.nbatch // self.cb * self.rsplit // self.k


def _plan_shift(layer, b0, nbatch, n_steps, drop, keep):
    if nbatch >= n_steps:
        cb = -(-nbatch // n_steps)
        while nbatch % cb:
            cb += 1
        return _Shift(layer, b0, nbatch, cb, 1, 1, drop, keep)
    best = None
    for r in (1, 2, 4, 8):
        if keep % (r * SUBLANES):
            continue
        k = -(-nbatch * r // n_steps)
        if nbatch * r % k == 0 and (best is None or k * best[0] < best[1] * r):
            best = (r, k)
    return _Shift(layer, b0, nbatch, 1, best[0], best[1], drop, keep)


def _shift_copies(shift, src_ref, dst_ref, buf, in_sem, out_sem, chunk, fetch):
    rows = shift.keep // shift.rsplit
    slot = chunk % 2
    copies = []
    for j in range(shift.k):
        part = chunk * shift.k + j
        b = shift.b0 + (part // shift.rsplit) * shift.cb
        r = (part % shift.rsplit) * rows
        if fetch:
            copies.append(pltpu.make_async_copy(
                src_ref.at[shift.layer, pl.ds(b, shift.cb), pl.ds(shift.drop + r, rows), :],
                buf.at[slot, j], in_sem.at[slot, j]))
        else:
            copies.append(pltpu.make_async_copy(
                buf.at[slot, j], dst_ref.at[pl.ds(b, shift.cb), pl.ds(r, rows), :], out_sem.at[slot, j]))
    return copies


def _host_shift(kernel_fn, shift, n_in, n_out, aliased, grid):
    n_steps = math.prod(grid)
    n_chunks = shift.n_chunks
    assert n_chunks <= n_steps

    def hosted(*refs):
        src = refs[n_in]
        pos = n_in + (2 if aliased else 1)
        outs = refs[pos:pos + n_out]
        dst = refs[pos + n_out]
        scratch = refs[pos + n_out + 1:-3]
        buf, in_sem, out_sem = refs[-3:]
        copies = functools.partial(_shift_copies, shift, src, dst, buf, in_sem, out_sem)

        def start(chunk, fetch):
            for copy in copies(chunk, fetch):
                copy.start()

        def wait(chunk, fetch):
            for copy in copies(chunk, fetch):
                copy.wait()

        def advance(s):
            if isinstance(s, int):
                if 1 <= s <= n_chunks:
                    wait(s - 1, True)
                    start(s - 1, False)
                if 2 <= s <= n_chunks + 1:
                    wait(s - 2, False)
                if s < n_chunks:
                    start(s, True)
            else:
                wait(s - 1, True)
                start(s - 1, False)
                wait(s - 2, False)
                start(s, True)

        step = 0
        for axis in range(len(grid)):
            step = step * grid[axis] + pl.program_id(axis)

        pl.when((step >= 2) & (step < n_chunks))(lambda: advance(step))
        for s in sorted({0, 1, n_chunks, n_chunks + 1}):
            if s < n_steps and not 2 <= s < n_chunks:
                pl.when(step == s)(functools.partial(advance, s))

        kernel_fn(*refs[:n_in], *outs, *scratch)

        @pl.when(step == n_steps - 1)
        def _():
            for s in range(n_steps, n_chunks + 2):
                advance(s)

    return hosted


def _hosting_call(kernel_fn, shift, cache, dst, *, out_shape, grid, in_specs, out_specs, args,
                  scratch_shapes=(), **kwargs):
    if shift is None:
        res = pl.pallas_call(kernel_fn, out_shape=tuple(out_shape), grid=grid, in_specs=list(in_specs),
                             out_specs=tuple(out_specs), scratch_shapes=list(scratch_shapes), **kwargs)(*args)
        return tuple(res), dst
    n_in, n_out = len(in_specs), len(out_shape)
    aliased = dst is not None
    anywhere = pl.BlockSpec(memory_space=pl.ANY)
    res = pl.pallas_call(
        _host_shift(kernel_fn, shift, n_in, n_out, aliased, grid),
        out_shape=tuple(out_shape) + (jax.ShapeDtypeStruct(cache.shape[1:], cache.dtype),),
        grid=grid,
        in_specs=list(in_specs) + [anywhere] * (2 if aliased else 1),
        out_specs=tuple(out_specs) + (anywhere,),
        scratch_shapes=list(scratch_shapes) + [
            pltpu.VMEM((2, shift.k, shift.cb, shift.keep // shift.rsplit, cache.shape[3]), cache.dtype),
            pltpu.SemaphoreType.DMA((2, shift.k)), pltpu.SemaphoreType.DMA((2, shift.k))],
        input_output_aliases={n_in + 1: n_out} if aliased else {},
        **kwargs,
    )(*args, cache, *([dst] if aliased else []))
    return tuple(res[:-1]), res[-1]


def _mod_kernel(c_ref, w_ref, b_ref, o_ref):
    c = c_ref[...]
    a = (c * _sigmoid(c)).astype(BF16)
    o_ref[...] = jnp.dot(a, w_ref[...], preferred_element_type=F32) + b_ref[...]


def _modulation(c, w_ada, b_ada):
    n, d = c.shape
    nout = w_ada.shape[1]
    tn = 1024
    return pl.pallas_call(
        _mod_kernel,
        out_shape=jax.ShapeDtypeStruct((n, nout), F32),
        grid=(nout // tn,),
        in_specs=[pl.BlockSpec((n, d), lambda j: (0, 0)),
                  pl.BlockSpec((d, tn), lambda j: (0, j)),
                  pl.BlockSpec((1, tn), lambda j: (0, j))],
        out_specs=pl.BlockSpec((n, tn), lambda j: (0, j)),
        compiler_params=_cparams("arbitrary"),
        name="modulation",
    )(c, w_ada, b_ada.reshape(1, nout))


def _store_heads(ref, c0, y):
    for j in range(y.shape[1] // HG_DK):
        ref[c0 // HG_DK + j] = y[:, j * HG_DK:(j + 1) * HG_DK].astype(ref.dtype)


def _stage_kv(scr, kv, y):
    for h in range(HEADS_PER_GROUP):
        slot = (kv * HEADS_PER_GROUP + h) * SUBLANES
        for j in range(y.shape[0] // SUBLANES):
            scr[j * 64 + slot:j * 64 + slot + SUBLANES, :] = y[j * SUBLANES:(j + 1) * SUBLANES,
                                                               h * HEAD_DIM:(h + 1) * HEAD_DIM]


def _emit_kv(out_ref, scr, gb, rb, n):
    per_seq = n // SUBLANES

    def body(it, carry):
        s = it // per_seq
        jj = it - s * per_seq
        src = pl.multiple_of((s * (rb // SUBLANES) + (rb - n) // SUBLANES + jj) * 64, 64)
        dst = pl.multiple_of(jj * 64, 64)
        for i in range(SUBLANES):
            out_ref[s, pl.ds(dst + i * SUBLANES, SUBLANES), :] = scr[pl.ds(src + i, SUBLANES, stride=SUBLANES), :]
        return carry

    lax.fori_loop(0, gb * per_seq, body, 0)


def _inproj_kernel(x_ref, mod_ref, w_ref, lbp_ref, attn_ref, hq_ref, hk_ref, hv_ref, logf_ref, og_ref,
                   gates_ref, *refs, layer, seq_rows, kv_rows):
    kv_refs, kv_scr = refs[:N_GROUPS], refs[N_GROUPS]
    gb, rb, d = x_ref.shape
    tm = gb * rb
    hg_w = hq_ref.shape[0] * HG_DK
    act = attn_ref.dtype
    sh = mod_ref[:, :, 0:d]
    sc = mod_ref[:, :, d:2 * d]
    u = (x_ref[...] * (1.0 + sc) + sh).reshape(tm, d).astype(BF16)

    def proj(c0):
        return jnp.dot(u, w_ref[:, c0:c0 + COL_CHUNK], preferred_element_type=F32)

    assert COL_CHUNK == GROUP_W
    order = list(range(N_GROUPS)) + [s * N_GROUPS + g for g in range(N_GROUPS) for s in (1, 2)]
    for chunk in order:
        c0 = chunk * COL_CHUNK
        y = proj(c0)
        attn_ref[:, c0:c0 + COL_CHUNK] = y.astype(act)
        section, g = divmod(chunk, N_GROUPS)
        if section:
            _stage_kv(kv_scr, section - 1, y)
        if section == 2:
            first_tile = (seq_rows - max(kv_rows[g], rb)) // rb
            emit = functools.partial(_emit_kv, kv_refs[g], kv_scr, gb, rb, min(kv_rows[g], rb))
            if first_tile == 0:
                emit()
            else:
                pl.when(pl.program_id(1) >= first_tile)(emit)
    base = 3 * ATTN_W
    for c0 in range(0, hg_w, COL_CHUNK):
        y = proj(base + c0)
        _store_heads(hq_ref, c0, y * _sigmoid(y))
    lbp = lbp_ref[...]
    e = jnp.exp(lbp - jnp.max(lbp, axis=0, keepdims=True))
    lb = jnp.sum(e[0:layer + 1], axis=0, keepdims=True) / jnp.sum(e, axis=0, keepdims=True)
    base += hg_w
    for c0 in range(0, hg_w, COL_CHUNK):
        lbc = lb[:, c0:c0 + COL_CHUNK]
        f = lbc + (1.0 - lbc) * _sigmoid(proj(base + c0))
        _store_heads(logf_ref, c0, jnp.log(f))
        _store_heads(hk_ref, c0, 1.0 - f)
    base += hg_w
    for c0 in range(0, hg_w, COL_CHUNK):
        _store_heads(hv_ref, c0, proj(base + c0))
    base += hg_w
    for c0 in range(0, hg_w, COL_CHUNK):
        y = proj(base + c0)
        _store_heads(og_ref, c0, y * _sigmoid(y))
    base += hg_w
    for c0 in range(0, 2 * d, COL_CHUNK):
        gates_ref[:, c0:c0 + COL_CHUNK] = _sigmoid(proj(base + c0)).astype(act)


def _in_projection(x3, mod3, w_in, lb_param, layer, gb, rb, act, hosted=(None, None, None)):
    g, r, d = x3.shape
    n = g * r
    tm = gb * rb
    hg_w = lb_param.shape[1]
    rt = r // rb
    heads = hg_w // HG_DK
    rows = lambda width: pl.BlockSpec((tm, width), lambda gi, ri: (gi * rt + ri, 0))
    by_head = pl.BlockSpec((heads, tm, HG_DK), lambda gi, ri: (0, gi * rt + ri, 0))
    head_major = lambda dtype: jax.ShapeDtypeStruct((heads, n, HG_DK), dtype)
    out_shapes = (
        jax.ShapeDtypeStruct((n, 3 * ATTN_W), act),
        head_major(act),
        head_major(act),
        head_major(act),
        head_major(F32),
        head_major(act),
        jax.ShapeDtypeStruct((n, 2 * d), act),
    )
    rp = 2 * HEADS_PER_GROUP
    kv_rows = tuple(min(window, r) for window, dil in ATTN_GROUPS)
    kv_specs = []
    for kept in kv_rows:
        assert (r - max(kept, rb)) % rb == 0
        first_tile = (r - max(kept, rb)) // rb
        kv_specs.append(pl.BlockSpec((gb, min(kept, rb) * rp, HEAD_DIM),
                                     lambda gi, ri, first_tile=first_tile: (gi, jnp.maximum(ri - first_tile, 0), 0)))
    out_shapes += tuple(jax.ShapeDtypeStruct((g, kept * rp, HEAD_DIM), F32) for kept in kv_rows)
    return _hosting_call(
        functools.partial(_inproj_kernel, layer=layer, seq_rows=r, kv_rows=kv_rows), *hosted,
        out_shape=out_shapes,
        grid=(g // gb, rt),
        in_specs=[pl.BlockSpec((gb, rb, d), lambda gi, ri: (gi, ri, 0)),
                  pl.BlockSpec((gb, 1, mod3.shape[2]), lambda gi, ri: (gi, 0, 0)),
                  _resident(w_in.shape),
                  _resident(lb_param.shape)],
        out_specs=(rows(3 * ATTN_W), by_head, by_head, by_head, by_head, by_head, rows(2 * d), *kv_specs),
        scratch_shapes=[pltpu.VMEM((tm * rp, HEAD_DIM), F32)],
        args=(x3, mod3, w_in, lb_param),
        compiler_params=_cparams("arbitrary", "arbitrary"),
        name="in_projection",
    )


ATTN_BLOCKS_PER_ITER = 8


def _attn_prompt_kernel(q_ref, k_ref, v_ref, attn_ref, qf, kf, vf, mf, lf, af, *, head_axis, group_axis):
    s = attn_ref.shape[0]
    nblk = s // BAND
    head = pl.program_id(head_axis)
    group = pl.program_id(group_axis)
    qi = lax.broadcasted_iota(jnp.int32, (BAND, BAND), 0)
    kj = lax.broadcasted_iota(jnp.int32, (BAND, BAND), 1)
    steps_own = qi - kj
    steps_prev = steps_own + BAND
    scale = HEAD_DIM ** -0.5
    ones = jnp.ones((BAND, HEAD_DIM), BF16)
    qf[...] = q_ref[...].astype(F32)
    kf[...] = k_ref[...].astype(F32)
    vf[...] = v_ref[...].astype(F32)
    for g, (window, dil) in enumerate(ATTN_GROUPS):
        nb = nblk // dil
        slope = jnp.float32(_alibi_slope(g, HEADS_PER_GROUP - 1) * dil)
        for h in range(HEADS_PER_GROUP - 1):
            slope = jnp.where(head == h, jnp.float32(_alibi_slope(g, h) * dil), slope)
        bias_own = jnp.where(steps_own >= 0, -slope * steps_own.astype(F32), NEG)
        bias_prev = jnp.where(steps_prev <= BAND, -slope * steps_prev.astype(F32), NEG)

        def body(it, carry, g=g, dil=dil, nb=nb, bias_own=bias_own, bias_prev=bias_prev):
            span = min(nb, ATTN_BLOCKS_PER_ITER)
            blocks = []
            for u in range(ATTN_BLOCKS_PER_ITER):
                idx = it * ATTN_BLOCKS_PER_ITER + u
                r = idx // nb
                i = idx - r * nb
                start = r + i * (BAND * dil)
                rows = pl.ds(start, BAND, stride=dil) if dil > 1 else pl.ds(start, BAND)
                if u % span:
                    blocks.append((rows, "chain", None, True))
                elif span == nb:
                    blocks.append((rows, "none", None, False))
                else:
                    start_prev = r + jnp.maximum(i - 1, 0) * (BAND * dil)
                    rows_prev = pl.ds(start_prev, BAND, stride=dil) if dil > 1 else pl.ds(start_prev, BAND)
                    blocks.append((rows, "load", rows_prev, i > 0))
            pair = lambda y, z: lax.dot_general(y, z, NT_DIMS, preferred_element_type=F32)
            scores, keys = [], []
            for rows, how, rows_prev, has_prev in blocks:
                q = qf[rows, :].astype(BF16)
                keys.append(kf[rows, :].astype(BF16))
                if how == "none":
                    scores.append((pair(q, keys[-1]), None))
                else:
                    k_prev = keys[-2] if how == "chain" else kf[rows_prev, :].astype(BF16)
                    scores.append((pair(q, keys[-1]), pair(q, k_prev)))
            probs = []
            for (rows, how, rows_prev, has_prev), (s_own, s_prev) in zip(blocks, scores):
                s_own = s_own * scale + bias_own
                if s_prev is None:
                    m = jnp.max(s_own, axis=-1, keepdims=True)
                    probs.append((m, jnp.exp(s_own - m).astype(BF16), None))
                else:
                    s_prev = s_prev * scale + bias_prev
                    if how == "load":
                        s_prev = jnp.where(has_prev, s_prev, NEG)
                    m = jnp.max(jnp.maximum(s_own, s_prev), axis=-1, keepdims=True)
                    probs.append((m, jnp.exp(s_own - m).astype(BF16), jnp.exp(s_prev - m).astype(BF16)))
            results, values = [], []
            for (rows, how, rows_prev, has_prev), (m, p_own, p_prev) in zip(blocks, probs):
                values.append(jnp.concatenate([vf[rows, :].astype(BF16), ones], axis=1))
                both = jnp.dot(p_own, values[-1], preferred_element_type=F32)
                if p_prev is not None:
                    v_prev = values[-2] if how == "chain" else jnp.concatenate(
                        [vf[rows_prev, :].astype(BF16), ones], axis=1)
                    both = both + jnp.dot(p_prev, v_prev, preferred_element_type=F32)
                acc, l = both[:, :HEAD_DIM], both[:, HEAD_DIM:]
                if g < N_GROUPS - 1:
                    m_old = mf[rows, :]
                    m_new = jnp.maximum(m_old, m)
                    a_old = jnp.exp(m_old - m_new)
                    a_blk = jnp.exp(m - m_new)
                    l = a_old * lf[rows, :] + a_blk * l
                    acc = a_old * af[rows, :] + a_blk * acc
                    m = m_new
                results.append((rows, m, l, acc))
            for rows, m, l, acc in results:
                mf[rows, :] = jnp.broadcast_to(m, (BAND, HEAD_DIM))
                lf[rows, :] = l
                af[rows, :] = acc
            return carry

        @pl.when(group == N_GROUPS - 1 - g)
        def _(body=body):
            lax.fori_loop(0, nblk // ATTN_BLOCKS_PER_ITER, body, 0)

    @pl.when(group == N_GROUPS - 1)
    def _():
        attn_ref[...] = (af[...] / lf[...]).astype(attn_ref.dtype)


def _attention_prompt(qkv, b, s, hosted=(None, None, None)):
    hpq = ATTN_W // HEAD_DIM
    in_specs = [pl.BlockSpec((s, HEAD_DIM), lambda bi, h, g, section=section:
                             (bi, section * hpq + (N_GROUPS - 1 - g) * HEADS_PER_GROUP + h))
                for section in range(3)]
    (attn,), dst = _hosting_call(
        functools.partial(_attn_prompt_kernel, head_axis=1, group_axis=2), *hosted,
        out_shape=(jax.ShapeDtypeStruct((b * s, GROUP_W), BF16),),
        grid=(b, HEADS_PER_GROUP, N_GROUPS),
        in_specs=in_specs,
        out_specs=(pl.BlockSpec((s, HEAD_DIM), lambda bi, h, g: (bi, h)),),
        scratch_shapes=[pltpu.VMEM((s, HEAD_DIM), F32) for _ in range(6)],
        args=[qkv] * 3,
        compiler_params=_cparams("arbitrary", "arbitrary", "arbitrary"),
        name="attn_prompt",
    )
    return attn, dst


def _attn_sample_kernel(*refs, group, dil, n_steps, length, seqs):
    first = group == 0
    last = group == N_GROUPS - 1
    q_ref, newkv_ref, cache_ref = refs[:3]
    pos = 3
    if not first:
        m_in, l_in, a_in = refs[pos:pos + 3]
        pos += 3
    pos += 1
    newc_ref = refs[pos]
    outs = refs[pos + 1:]
    t = q_ref.shape[0] // seqs
    rp = 2 * HEADS_PER_GROUP
    stretch = max(dil, t)
    nkeys = length // stretch * t
    j_c = lax.broadcasted_iota(jnp.int32, (t, nkeys), 0)
    c_c = lax.broadcasted_iota(jnp.int32, (t, nkeys), 1)
    dist_c = length + j_c - ((c_c // t) * stretch + (c_c & (t - 1)))
    ok_c = ((dist_c & (dil - 1)) == 0) & (dist_c <= n_steps * dil)
    j_n = lax.broadcasted_iota(jnp.int32, (t, t), 0)
    r_n = lax.broadcasted_iota(jnp.int32, (t, t), 1)
    dist_n = j_n - r_n
    ok_n = (dist_n >= 0) & ((dist_n & (dil - 1)) == 0) & (dist_n <= n_steps * dil)
    scale = HEAD_DIM ** -0.5
    qoff = group * GROUP_W
    cache_rows = lambda s, first_row: cache_ref[s, :, pl.ds(first_row, t, stride=rp), :].reshape(
        nkeys, HEAD_DIM).astype(BF16)
    new_rows = lambda s, first_row: newkv_ref[s, pl.ds(first_row, t, stride=rp), :].astype(BF16)
    chains = [(s, h) for s in range(seqs) for h in range(HEADS_PER_GROUP)]
    scores = []
    for s, h in chains:
        q = q_ref[s * t:(s + 1) * t, qoff + h * HEAD_DIM:qoff + (h + 1) * HEAD_DIM].astype(BF16)
        scores.append((lax.dot_general(q, cache_rows(s, h), NT_DIMS, preferred_element_type=F32),
                       lax.dot_general(q, new_rows(s, h), NT_DIMS, preferred_element_type=F32)))
    probs = []
    for (s, h), (s_c, s_n) in zip(chains, scores):
        rows, sl = slice(s * t, (s + 1) * t), slice(h * HEAD_DIM, (h + 1) * HEAD_DIM)
        slope = _alibi_slope(group, h)
        s_c = jnp.where(ok_c, s_c * scale - slope * dist_c.astype(F32), NEG)
        s_n = jnp.where(ok_n, s_n * scale - slope * dist_n.astype(F32), NEG)
        m = jnp.maximum(jnp.max(s_c, axis=-1, keepdims=True), jnp.max(s_n, axis=-1, keepdims=True))
        if not first:
            m = jnp.maximum(m, m_in[rows, sl][:, 0:1])
        p_c = jnp.exp(s_c - m)
        p_n = jnp.exp(s_n - m)
        l = jnp.sum(p_c, axis=-1, keepdims=True) + jnp.sum(p_n, axis=-1, keepdims=True)
        probs.append((m, l, p_c.astype(BF16), p_n.astype(BF16)))
    for (s, h), (m, l, p_c, p_n) in zip(chains, probs):
        rows, sl = slice(s * t, (s + 1) * t), slice(h * HEAD_DIM, (h + 1) * HEAD_DIM)
        acc = (jnp.dot(p_c, cache_rows(s, HEADS_PER_GROUP + h), preferred_element_type=F32)
               + jnp.dot(p_n, new_rows(s, HEADS_PER_GROUP + h), preferred_element_type=F32))
        if not first:
            a_old = jnp.exp(m_in[rows, sl] - m)
            l = a_old * l_in[rows, sl] + l
            acc = a_old * a_in[rows, sl] + acc
        if last:
            outs[0][rows, sl] = acc / l
        else:
            outs[0][rows, sl] = jnp.broadcast_to(m, (t, HEAD_DIM))
            outs[1][rows, sl] = jnp.broadcast_to(l, (t, HEAD_DIM))
            outs[2][rows, sl] = acc
    for s in range(seqs):
        newc_ref[s] = newkv_ref[s]


SAMPLE_SEQS_PER_STEP = (8, 4, 2)


def _attention_sample(qkv, new_kv, cache, shifted, layer, running, t, group):
    window, dil = ATTN_GROUPS[group]
    depth, bd, crow, dh = cache.shape
    rp = 2 * HEADS_PER_GROUP
    length = crow // rp
    stretch = max(dil, t)
    seqs = math.gcd(SAMPLE_SEQS_PER_STEP[group], bd)
    first = group == 0
    last = group == N_GROUPS - 1
    rows = pl.BlockSpec((seqs * t, GROUP_W), lambda bi: (bi, 0))
    stat = jax.ShapeDtypeStruct((bd * t, GROUP_W), F32)
    new_blk = pl.BlockSpec((seqs, t * rp, dh), lambda bi: (bi, 0, 0))
    in_specs = [pl.BlockSpec((seqs * t, 3 * ATTN_W), lambda bi: (bi, 0)),
                new_blk,
                pl.BlockSpec((None, seqs, length // stretch, t * rp, dh), lambda bi: (layer, bi, 0, 0, 0))]
    args = [qkv, new_kv, cache.reshape(depth, bd, length // stretch, stretch * rp, dh)]
    if not first:
        in_specs += [rows, rows, rows]
        args += list(running)
    in_specs.append(pl.BlockSpec(memory_space=pl.ANY))
    args.append(shifted)
    n_out = 1 if last else 3
    res = pl.pallas_call(
        functools.partial(_attn_sample_kernel, group=group, dil=dil, n_steps=window // dil, length=length,
                          seqs=seqs),
        out_shape=(jax.ShapeDtypeStruct(shifted.shape, shifted.dtype),) + (stat,) * n_out,
        grid=(bd // seqs,),
        in_specs=in_specs,
        out_specs=(pl.BlockSpec((seqs, t * rp, dh), lambda bi: (bi, length // t - 1, 0)),) + (rows,) * n_out,
        input_output_aliases={len(args) - 1: 0},
        compiler_params=_cparams("arbitrary"),
        name=f"attn_sample_g{group}",
    )(*args)
    return res[0], (res[1] if last else tuple(res[1:]))


def _cumsum_rows(x, row):
    d = 1
    while d < x.shape[0]:
        x = x + jnp.where(row >= d, pltpu.roll(x, d, 0), 0.0)
        d *= 2
    return x


def _hgrn_carry(q, k, v, g, st):
    c = q.shape[0]
    qg = (q * jnp.exp(g)).astype(BF16)
    o = lax.dot_general(qg, st.astype(BF16), NT_DIMS, preferred_element_type=F32)
    g_last = g[c - 1:c, :]
    kd = (k * jnp.exp(g_last - g)).astype(BF16)
    st_new = st * jnp.exp(g_last) + lax.dot_general(v.astype(BF16), kd, TN_DIMS, preferred_element_type=F32)
    return o, st_new


def _hgrn_chunks_pairwise(qs, ks, vs, lfs, states):
    c = qs[0].shape[0]
    row = lax.broadcasted_iota(jnp.int32, qs[0].shape, 0)
    gs = [_cumsum_rows(lf, row) for lf in lfs]
    outs = []
    for q, k, v, g in zip(qs, ks, vs, gs):
        o = jnp.sum(q * k, axis=-1, keepdims=True) * v
        for d in range(1, c):
            ok = row >= d
            e = jnp.exp(jnp.where(ok, g - pltpu.roll(g, d, 0), 0.0))
            a = jnp.sum(jnp.where(ok, q * pltpu.roll(k, d, 0) * e, 0.0), axis=-1, keepdims=True)
            o = o + a * pltpu.roll(v, d, 0)
        outs.append(o)
    cols = []
    for k, g in zip(ks, gs):
        g_last = g[c - 1:c, :]
        kd = k * jnp.exp(g_last - g)
        cols.append(jnp.concatenate([kd, jnp.broadcast_to(jnp.exp(g_last), kd.shape)], axis=0).T)
    carried = [jnp.dot((q * jnp.exp(g)).astype(BF16), s.astype(BF16), preferred_element_type=F32)
               for q, g, s in zip(qs, gs, states)]
    new_states = [s * col[:, c:c + 1] + jnp.dot(col[:, 0:c].astype(BF16), v.astype(BF16),
                                                preferred_element_type=F32)
                  for s, col, v in zip(states, cols, vs)]
    return [o + oc for o, oc in zip(outs, carried)], new_states


def _pair_level(c):
    ri = lax.broadcasted_iota(jnp.int32, (c, c), 0)
    ci = lax.broadcasted_iota(jnp.int32, (c, c), 1)
    x = ri ^ ci
    bits = jnp.zeros((c, c), jnp.int32)
    p = 1
    while p < c:
        bits = bits + (x >= p).astype(jnp.int32)
        p *= 2
    return jnp.where(ri > ci, bits + 1, jnp.where(ri == ci, 1, 0))


def _hgrn_chunks_blocked(qs, ks, vs, lfs, sts, g_refs, level):
    nh = len(qs)
    c = qs[0].shape[0]
    row = lax.broadcasted_iota(jnp.int32, (c, HG_DK), 0)
    sub = row & (SUBLANES - 1)
    sub8 = lax.broadcasted_iota(jnp.int32, (SUBLANES, HG_DK), 0)
    gs = []
    for lf, g_ref in zip(lfs, g_refs):
        x = lf
        for d in (1, 2, 4):
            x = x + jnp.where(sub >= d, pltpu.roll(x, d, 0), 0.0)
        g_ref[...] = x
        pieces = [x[0:SUBLANES]]
        off = None
        for j in range(1, c // SUBLANES):
            tot = jnp.broadcast_to(g_ref[j * SUBLANES - 1:j * SUBLANES, :], (SUBLANES, HG_DK))
            off = tot if off is None else off + tot
            pieces.append(x[j * SUBLANES:(j + 1) * SUBLANES] + off)
        g = jnp.concatenate(pieces, axis=0)
        g_ref[...] = g
        gs.append(g)

    def mid_rows(g_ref, n):
        half = n // 2
        bcast = lambda r: jnp.broadcast_to(g_ref[r:r + 1, :], (SUBLANES, HG_DK))
        out = []
        for j in range(c // SUBLANES):
            base = j * SUBLANES
            if n >= SUBLANES:
                out.append(bcast(base // n * n + half - 1))
            else:
                assert n == 4
                out.append(jnp.where(sub8 < 4, bcast(base + 1), bcast(base + 5)))
        return jnp.concatenate(out, axis=0)

    pair = lambda y, z: lax.dot_general(y, z, NT_DIMS, preferred_element_type=F32)
    prods = [[pair(q.astype(BF16), k.astype(BF16))] for q, k in zip(qs, ks)]
    n = 2
    while n <= c:
        second = (row & (n // 2)) != 0
        for u in range(nh):
            if n == 2:
                arg = jnp.where(second, lfs[u], 0.0)
            else:
                gmid = mid_rows(g_refs[u], n)
                arg = jnp.where(second, gs[u] - gmid, gmid - gs[u])
            y = (jnp.where(second, qs[u], ks[u]) * jnp.exp(arg)).astype(BF16)
            prods[u].append(pair(y, y))
        n *= 2
    outs = []
    for u in range(nh):
        amat = jnp.where(level == 1, prods[u][0], 0.0)
        for j, p in enumerate(prods[u][1:]):
            amat = jnp.where(level == j + 2, p, amat)
        outs.append(jnp.dot(amat.astype(BF16), vs[u].astype(BF16), preferred_element_type=F32))
    states = []
    for u in range(nh):
        o_carry, st_new = _hgrn_carry(qs[u], ks[u], vs[u], gs[u], sts[u])
        outs[u] = outs[u] + o_carry
        states.append(st_new)
    return outs, states


def _hgrn_output(o, og, nw):
    return o * lax.rsqrt(jnp.mean(o * o, axis=-1, keepdims=True) + RMS_EPS) * nw * og


HGRN_HEADS_PER_ITER = 4


HGRN_CHUNKS_PER_STEP = 2


def _hgrn_prompt_kernel(q_ref, k_ref, v_ref, lf_ref, og_ref, nw_ref, hg_ref, sfin_ref, st_ref, g_ref):
    ci = pl.program_id(1)
    heads = q_ref.shape[0]
    c = g_ref.shape[1]
    head_groups = heads // HGRN_HEADS_PER_ITER

    @pl.when(ci == 0)
    def _():
        st_ref[...] = jnp.zeros_like(st_ref)

    level = _pair_level(c)

    def head_group(it, carry):
        i = it % head_groups
        rows = pl.ds(pl.multiple_of(it // head_groups * c, c), c)
        hs = [i * HGRN_HEADS_PER_ITER + u for u in range(HGRN_HEADS_PER_ITER)]
        load = lambda ref: [ref[h, rows, :].astype(F32) for h in hs]
        outs, states = _hgrn_chunks_blocked(load(q_ref), load(k_ref), load(v_ref), load(lf_ref),
                                            [st_ref[h] for h in hs],
                                            [g_ref.at[u] for u in range(HGRN_HEADS_PER_ITER)], level)
        for h, o, st_new in zip(hs, outs, states):
            st_ref[h] = st_new
            hg_ref[h, rows, :] = _hgrn_output(o, og_ref[h, rows, :].astype(F32), nw_ref[...]).astype(hg_ref.dtype)
        return carry

    lax.fori_loop(0, q_ref.shape[1] // c * head_groups, head_group, 0)

    @pl.when(ci == pl.num_programs(1) - 1)
    def _():
        for h in range(heads):
            sfin_ref[0, h] = st_ref[h].T


def _hgrn_prompt(hq, hk, hv, logf, og, norm_w, b, s, chunk, hosted=(None, None, None)):
    heads, n, _ = hq.shape
    rows = chunk * math.gcd(HGRN_CHUNKS_PER_STEP, s // chunk)
    nc = s // rows
    blk = pl.BlockSpec((heads, rows, HG_DK), lambda bi, ci: (0, bi * nc + ci, 0))
    (hg, s_fin), dst = _hosting_call(
        _hgrn_prompt_kernel, *hosted,
        out_shape=(jax.ShapeDtypeStruct((heads, n, HG_DV), BF16),
                   jax.ShapeDtypeStruct((b, heads, HG_DK, HG_DV), F32)),
        grid=(b, nc),
        in_specs=[blk, blk, blk, blk, blk, pl.BlockSpec((1, HG_DV), lambda bi, ci: (0, 0))],
        out_specs=(blk, pl.BlockSpec((1, heads, HG_DK, HG_DV), lambda bi, ci: (bi, 0, 0, 0))),
        scratch_shapes=[pltpu.VMEM((heads, HG_DV, HG_DK), F32),
                        pltpu.VMEM((HGRN_HEADS_PER_ITER, chunk, HG_DK), F32)],
        args=(hq, hk, hv, logf, og, norm_w.reshape(1, HG_DV)),
        compiler_params=_cparams("arbitrary", "arbitrary"),
        name="hgrn_prompt",
    )
    return hg, s_fin, dst


HGRN_SAMPLE_SEQS_PER_STEP = 2


def _hgrn_sample_kernel(q_ref, k_ref, v_ref, lf_ref, og_ref, nw_ref, s_ref, hg_ref, snew_ref):
    seqs, heads = s_ref.shape[0], s_ref.shape[1]
    t = q_ref.shape[1] // seqs
    chains = [(s, h) for s in range(seqs) for h in range(heads)]
    load = lambda ref: [ref[h, s * t:(s + 1) * t, :] for s, h in chains]
    outs, new_states = _hgrn_chunks_pairwise(load(q_ref), load(k_ref), load(v_ref), load(lf_ref),
                                             [s_ref[s, h] for s, h in chains])
    for (s, h), o, st_new in zip(chains, outs, new_states):
        snew_ref[s, h] = st_new
        hg_ref[h, s * t:(s + 1) * t, :] = _hgrn_output(o, og_ref[h, s * t:(s + 1) * t, :], nw_ref[...])


def _hgrn_sample(hq, hk, hv, logf, og, norm_w, state, t):
    bd, heads = state.shape[0], state.shape[1]
    seqs = math.gcd(HGRN_SAMPLE_SEQS_PER_STEP, bd)
    blk = pl.BlockSpec((heads, seqs * t, HG_DK), lambda bi: (0, bi, 0))
    sblk = pl.BlockSpec((seqs, heads, HG_DK, HG_DV), lambda bi: (bi, 0, 0, 0))
    return pl.pallas_call(
        _hgrn_sample_kernel,
        out_shape=(jax.ShapeDtypeStruct((heads, bd * t, HG_DV), F32),
                   jax.ShapeDtypeStruct(state.shape, state.dtype)),
        grid=(bd // seqs,),
        in_specs=[blk, blk, blk, blk, blk, pl.BlockSpec((1, HG_DV), lambda bi: (0, 0)), sblk],
        out_specs=(blk, sblk),
        compiler_params=_cparams("arbitrary"),
        name="hgrn_sample",
    )(hq, hk, hv, logf, og, norm_w.reshape(1, HG_DV), state)


def _merge_kernel(x_ref, mod_ref, attn_ref, hg_ref, gates_ref, wa_ref, wb_ref, wo_ref, lng_ref, lnb_ref,
                  x1_ref, *, alpha):
    gb, rb, d = x_ref.shape
    tm = gb * rb
    hg = jnp.concatenate([hg_ref[h].astype(BF16) for h in range(hg_ref.shape[0])], axis=1)
    branch_a = jnp.dot(attn_ref[...].astype(BF16), wa_ref[...], preferred_element_type=F32)
    branch_b = jnp.dot(hg, wb_ref[...], preferred_element_type=F32)
    gate_a = gates_ref[:, 0:d].astype(F32)
    gate_b = gates_ref[:, d:2 * d].astype(F32)
    merged = gate_a * branch_a + gate_b * branch_b
    mix = jnp.dot(merged.astype(BF16), wo_ref[...], preferred_element_type=F32)
    g1 = mod_ref[:, :, 2 * d:3 * d]
    h = alpha * x_ref[...] + g1 * mix.reshape(gb, rb, d)
    x1_ref[...] = _layer_norm(h, lng_ref[...], lnb_ref[...])


def _merge(x3, mod3, attn, hg, gates, wa, wb, wo, ln_g, ln_b, gb, rb, alpha, hosted=(None, None, None)):
    g, r, d = x3.shape
    tm = gb * rb
    rt = r // rb
    row = lambda gi, ri: (gi * rt + ri, 0)
    rows = lambda a: pl.BlockSpec((tm, a.shape[1]), row)
    tile3 = pl.BlockSpec((gb, rb, d), lambda gi, ri: (gi, ri, 0))
    (x1,), dst = _hosting_call(
        functools.partial(_merge_kernel, alpha=alpha), *hosted,
        out_shape=(jax.ShapeDtypeStruct(x3.shape, F32),),
        grid=(g // gb, rt),
        in_specs=[tile3, pl.BlockSpec((gb, 1, mod3.shape[2]), lambda gi, ri: (gi, 0, 0)),
                  rows(attn), pl.BlockSpec((hg.shape[0], tm, hg.shape[2]), lambda gi, ri: (0, gi * rt + ri, 0)),
                  rows(gates),
                  _resident(wa.shape), _resident(wb.shape), _resident(wo.shape),
                  _resident((1, d)), _resident((1, d))],
        out_specs=(tile3,),
        args=(x3, mod3, attn, hg, gates, wa, wb, wo, ln_g.reshape(1, d), ln_b.reshape(1, d)),
        compiler_params=_cparams("arbitrary", "arbitrary"),
        name="merge",
    )
    return x1, dst


def _mlp_kernel(x1_ref, mod_ref, wu_ref, bu_ref, wd_ref, bd_ref, lng_ref, lnb_ref, x2_ref, *, alpha):
    gb, rb, d = x1_ref.shape
    tm = gb * rb
    dff = wu_ref.shape[1]
    x1 = x1_ref[...]
    sh = mod_ref[:, :, 3 * d:4 * d]
    sc = mod_ref[:, :, 4 * d:5 * d]
    g2 = mod_ref[:, :, 5 * d:6 * d]
    u = (x1 * (1.0 + sc) + sh).reshape(tm, d).astype(BF16)
    hidden = []
    for c0 in range(0, dff, COL_CHUNK):
        hid = jnp.dot(u, wu_ref[:, c0:c0 + COL_CHUNK], preferred_element_type=F32) + bu_ref[:, c0:c0 + COL_CHUNK]
        hidden.append(jnp.square(jnp.maximum(hid, 0.0)).astype(BF16))
    ff = jnp.dot(jnp.concatenate(hidden, axis=1), wd_ref[...], preferred_element_type=F32) + bd_ref[...]
    h = alpha * x1 + g2 * ff.reshape(gb, rb, d)
    x2_ref[...] = _layer_norm(h, lng_ref[...], lnb_ref[...])


def _mlp(x3, mod3, wu, bu, wd, bd, ln_g, ln_b, gb, rb, alpha, hosted=(None, None, None)):
    g, r, d = x3.shape
    dff = wu.shape[1]
    tile3 = pl.BlockSpec((gb, rb, d), lambda gi, ri: (gi, ri, 0))
    (x2,), dst = _hosting_call(
        functools.partial(_mlp_kernel, alpha=alpha), *hosted,
        out_shape=(jax.ShapeDtypeStruct(x3.shape, F32),),
        grid=(g // gb, r // rb),
        in_specs=[tile3, pl.BlockSpec((gb, 1, mod3.shape[2]), lambda gi, ri: (gi, 0, 0)),
                  _resident(wu.shape), _resident((1, dff)), _resident(wd.shape), _resident((1, d)),
                  _resident((1, d)), _resident((1, d))],
        out_specs=(tile3,),
        args=(x3, mod3, wu, bu.reshape(1, dff), wd, bd.reshape(1, d), ln_g.reshape(1, d), ln_b.reshape(1, d)),
        compiler_params=_cparams("arbitrary", "arbitrary"),
        name="mlp",
    )
    return x2, dst


PROMPT_ROWS = 256
SAMPLE_SEQS = 32
HGRN_CHUNK = 128


def _prompt_layer(x, mod, caches, t_new, layer, w, alpha):
    b, s, d = x.shape
    mod3 = mod.reshape(b, 1, mod.shape[1])
    gb, rb = 1, min(PROMPT_ROWS, s)
    chunk = min(HGRN_CHUNK, s)
    rp = 2 * HEADS_PER_GROUP
    bd = caches[0].shape[1]
    tile_steps = s // rb * b
    hgrn_steps = s // chunk // math.gcd(HGRN_CHUNKS_PER_STEP, s // chunk) * b

    def plan(gi, b0, nbatch, n_steps):
        rows = caches[gi].shape[2]
        return _plan_shift(layer, b0, nbatch, n_steps, t_new * rp, rows - t_new * rp)

    attn_steps = b * HEADS_PER_GROUP * N_GROUPS
    n_hgrn = bd // 4
    n_attn = (bd - n_hgrn) // 2
    big = None

    def share(b0, nbatch, n_steps):
        return (plan(2, b0, nbatch, n_steps), caches[2], big) if nbatch else (None, None, big)

    (qkv, hq, hk, hv, logf, og, gates, *kv_tails), mid = _in_projection(
        x, mod3, w["w_in"], w["lb_param"], layer, gb, rb, BF16,
        hosted=(plan(1, 0, bd, tile_steps), caches[1], None))
    hg, s_fin, big = _hgrn_prompt(hq, hk, hv, logf, og, w["hg_norm_w"], b, s, chunk,
                                  hosted=share(0, n_hgrn, hgrn_steps))
    attn, big = _attention_prompt(qkv, b, s, hosted=share(n_hgrn, n_attn, attn_steps))
    new_bufs = [kv.reshape(b, kv.shape[1] // rp, 2, HEADS_PER_GROUP, HEAD_DIM) for kv in kv_tails]
    x1, small = _merge(x, mod3, attn, hg, gates, w["w_branch_a"], w["w_branch_b"], w["w_out"],
                       w["ln1_g"], w["ln1_b"], gb, rb, alpha, hosted=(plan(0, 0, bd, tile_steps), caches[0], None))
    x2, big = _mlp(x1, mod3, w["w_up"], w["b_up"], w["w_down"], w["b_down"], w["ln2_g"], w["ln2_b"], gb, rb,
                   alpha, hosted=share(n_hgrn + n_attn, bd - n_hgrn - n_attn, tile_steps))
    return x2, new_bufs, s_fin, (small, mid, big)


def _sample_layer(x, mod, caches, shifted, states, layer, w, alpha):
    b, s, d = x.shape
    mod3 = mod.reshape(b, 1, mod.shape[1])
    gb, rb = min(SAMPLE_SEQS, b), s
    rp = 2 * HEADS_PER_GROUP
    (qkv, hq, hk, hv, logf, og, gates, *new_kv), _ = _in_projection(
        x, mod3, w["w_in"], w["lb_param"], layer, gb, rb, F32)
    new_bufs = []
    running = None
    for gi in range(N_GROUPS):
        buf, running = _attention_sample(qkv, new_kv[gi], caches[gi], shifted[gi], layer, running, s, gi)
        new_bufs.append(buf)
    hg, s_fin = _hgrn_sample(hq, hk, hv, logf, og, w["hg_norm_w"], states[layer], s)
    x1, _ = _merge(x, mod3, running, hg, gates, w["w_branch_a"], w["w_branch_b"], w["w_out"],
                   w["ln1_g"], w["ln1_b"], gb, rb, alpha)
    x2, _ = _mlp(x1, mod3, w["w_up"], w["b_up"], w["w_down"], w["b_down"], w["ln2_g"], w["ln2_b"], gb, rb, alpha)
    return x2, new_bufs, s_fin


def _stack_layers(per_layer):
    return per_layer[0][None] if len(per_layer) == 1 else jnp.stack(per_layer)


def kernel(x_prompt, x_sample, c_prompt, c_sample, cache_kv_w128, cache_kv_w512, cache_kv_w2048, state_hgrn,
           w_ada, b_ada, w_in, lb_param, hg_norm_w, w_branch_a, w_branch_b, w_out, ln1_g, ln1_b, w_up, b_up,
           w_down, b_down, ln2_g, ln2_b):
    depth = w_ada.shape[0]
    alpha = (2 * depth) ** 0.25
    caches = (cache_kv_w128, cache_kv_w512, cache_kv_w2048)
    for (window, dil), cache in zip(ATTN_GROUPS, caches):
        assert window // dil == BAND and cache.shape[2] == window
    assert x_prompt.shape[1] % (BAND * ATTN_GROUPS[-1][1]) == 0 and x_sample.shape[1] == SUBLANES
    views = tuple(c.reshape(c.shape[0], c.shape[1], c.shape[2] * c.shape[3] * c.shape[4], c.shape[5])
                  for c in caches)
    nb = c_prompt.shape[0]
    yp, ys = x_prompt, x_sample
    p_bufs, p_states, s_bufs, s_states = [], [], [], []
    for l in range(depth):
        w = dict(w_in=w_in[l].astype(BF16), lb_param=lb_param, hg_norm_w=hg_norm_w[l],
                 w_branch_a=w_branch_a[l].astype(BF16), w_branch_b=w_branch_b[l].astype(BF16),
                 w_out=w_out[l].astype(BF16), ln1_g=ln1_g[l], ln1_b=ln1_b[l],
                 w_up=w_up[l].astype(BF16), b_up=b_up[l], w_down=w_down[l].astype(BF16), b_down=b_down[l],
                 ln2_g=ln2_g[l], ln2_b=ln2_b[l])
        mod = _modulation(jnp.concatenate([c_prompt, c_sample], axis=0), w_ada[l].astype(BF16), b_ada[l])
        yp, bufs_p, st_p, shifted = _prompt_layer(yp, mod[:nb], views, x_sample.shape[1], l, w, alpha)
        ys, bufs_s, st_s = _sample_layer(ys, mod[nb:], views, shifted, state_hgrn, l, w, alpha)
        p_bufs.append(bufs_p)
        p_states.append(st_p)
        s_bufs.append([buf.reshape(c.shape[1:]) for buf, c in zip(bufs_s, caches)])
        s_states.append(st_s)
    group = lambda bufs, gi: _stack_layers([bl[gi] for bl in bufs])
    return (yp, ys, group(p_bufs, 0), group(p_bufs, 1), group(p_bufs, 2), _stack_layers(p_states),
            group(s_bufs, 0), group(s_bufs, 1), group(s_bufs, 2), _stack_layers(s_states))
```

```python
import functools
import math
from typing import NamedTuple

import jax
import jax.numpy as jnp
from jax import lax
from jax.experimental import pallas as pl
from jax.experimental.pallas import tpu as pltpu

F32 = jnp.float32
BF16 = jnp.bfloat16

ATTN_GROUPS = ((128, 1), (512, 4), (2048, 16))
N_GROUPS = len(ATTN_GROUPS)
HEADS_PER_GROUP = 4
HEAD_DIM = 128
N_ATTN_HEADS = N_GROUPS * HEADS_PER_GROUP
ATTN_W = N_ATTN_HEADS * HEAD_DIM
GROUP_W = HEADS_PER_GROUP * HEAD_DIM
BAND = 128
HG_DK = 128
HG_DV = 128
LN_EPS = 1e-5
RMS_EPS = 1e-6
NEG = -1e30

SUBLANES = 8
LANES = 128
VMEM_LIMIT_BYTES = 56 * 1024 * 1024
COL_CHUNK = 512

NT_DIMS = (((1,), (1,)), ((), ()))
TN_DIMS = (((0,), (0,)), ((), ()))


def _alibi_slope(group, head):
    return 2.0 ** (-8.0 * (group * HEADS_PER_GROUP + head + 1) / N_ATTN_HEADS)


def _cparams(*sem):
    return pltpu.CompilerParams(dimension_semantics=sem, vmem_limit_bytes=VMEM_LIMIT_BYTES)


def _resident(shape):
    nd = len(shape)
    return pl.BlockSpec(shape, lambda *_: (0,) * nd, pipeline_mode=pl.Buffered(1))


def _sigmoid(x):
    return 1.0 / (1.0 + jnp.exp(-x))


def _layer_norm(h, g, b):
    mu = jnp.mean(h, axis=-1, keepdims=True)
    hc = h - mu
    var = jnp.mean(hc * hc, axis=-1, keepdims=True)
    return hc * lax.rsqrt(var + LN_EPS) * g + b


class _Shift(NamedTuple):
    layer: int
    b0: int
    nbatch: int
    cb: int
    rsplit: int
    k: int
    drop: int
    keep: int

    @property
    def n_chunks(self):
        return self.nbatch // self.cb * self.rsplit // self.k


def _plan_shift(layer, b0, nbatch, n_steps, drop, keep):
    if nbatch >= n_steps:
        cb = -(-nbatch // n_steps)
        while nbatch % cb:
            cb += 1
        return _Shift(layer, b0, nbatch, cb, 1, 1, drop, keep)
    best = None
    for r in (1, 2, 4, 8):
        if keep % (r * SUBLANES):
            continue
        k = -(-nbatch * r // n_steps)
        if nbatch * r % k == 0 and (best is None or k * best[0] < best[1] * r):
            best = (r, k)
    return _Shift(layer, b0, nbatch, 1, best[0], best[1], drop, keep)


def _shift_copies(shift, src_ref, dst_ref, buf, in_sem, out_sem, chunk, fetch):
    rows = shift.keep // shift.rsplit
    slot = chunk % 2
    copies = []
    for j in range(shift.k):
        part = chunk * shift.k + j
        b = shift.b0 + (part // shift.rsplit) * shift.cb
        r = (part % shift.rsplit) * rows
        if fetch:
            copies.append(pltpu.make_async_copy(
                src_ref.at[shift.layer, pl.ds(b, shift.cb), pl.ds(shift.drop + r, rows), :],
                buf.at[slot, j], in_sem.at[slot, j]))
        else:
            copies.append(pltpu.make_async_copy(
                buf.at[slot, j], dst_ref.at[pl.ds(b, shift.cb), pl.ds(r, rows), :], out_sem.at[slot, j]))
    return copies


def _host_shift(kernel_fn, shift, n_in, n_out, aliased, grid):
    n_steps = math.prod(grid)
    n_chunks = shift.n_chunks
    assert n_chunks <= n_steps

    def hosted(*refs):
        src = refs[n_in]
        pos = n_in + (2 if aliased else 1)
        outs = refs[pos:pos + n_out]
        dst = refs[pos + n_out]
        scratch = refs[pos + n_out + 1:-3]
        buf, in_sem, out_sem = refs[-3:]
        copies = functools.partial(_shift_copies, shift, src, dst, buf, in_sem, out_sem)

        def start(chunk, fetch):
            for copy in copies(chunk, fetch):
                copy.start()

        def wait(chunk, fetch):
            for copy in copies(chunk, fetch):
                copy.wait()

        def advance(s):
            if isinstance(s, int):
                if 1 <= s <= n_chunks:
                    wait(s - 1, True)
                    start(s - 1, False)
                if 2 <= s <= n_chunks + 1:
                    wait(s - 2, False)
                if s < n_chunks:
                    start(s, True)
            else:
                wait(s - 1, True)
                start(s - 1, False)
                wait(s - 2, False)
                start(s, True)

        step = 0
        for axis in range(len(grid)):
            step = step * grid[axis] + pl.program_id(axis)

        pl.when((step >= 2) & (step < n_chunks))(lambda: advance(step))
        for s in sorted({0, 1, n_chunks, n_chunks + 1}):
            if s < n_steps and not 2 <= s < n_chunks:
                pl.when(step == s)(functools.partial(advance, s))

        kernel_fn(*refs[:n_in], *outs, *scratch)

        @pl.when(step == n_steps - 1)
        def _():
            for s in range(n_steps, n_chunks + 2):
                advance(s)

    return hosted


def _hosting_call(kernel_fn, shift, cache, dst, *, out_shape, grid, in_specs, out_specs, args,
                  scratch_shapes=(), **kwargs):
    if shift is None:
        res = pl.pallas_call(kernel_fn, out_shape=tuple(out_shape), grid=grid, in_specs=list(in_specs),
                             out_specs=tuple(out_specs), scratch_shapes=list(scratch_shapes), **kwargs)(*args)
        return tuple(res), dst
    n_in, n_out = len(in_specs), len(out_shape)
    aliased = dst is not None
    anywhere = pl.BlockSpec(memory_space=pl.ANY)
    res = pl.pallas_call(
        _host_shift(kernel_fn, shift, n_in, n_out, aliased, grid),
        out_shape=tuple(out_shape) + (jax.ShapeDtypeStruct(cache.shape[1:], cache.dtype),),
        grid=grid,
        in_specs=list(in_specs) + [anywhere] * (2 if aliased else 1),
        out_specs=tuple(out_specs) + (anywhere,),
        scratch_shapes=list(scratch_shapes) + [
            pltpu.VMEM((2, shift.k, shift.cb, shift.keep // shift.rsplit, cache.shape[3]), cache.dtype),
            pltpu.SemaphoreType.DMA((2, shift.k)), pltpu.SemaphoreType.DMA((2, shift.k))],
        input_output_aliases={n_in + 1: n_out} if aliased else {},
        **kwargs,
    )(*args, cache, *([dst] if aliased else []))
    return tuple(res[:-1]), res[-1]


def _mod_kernel(c_ref, w_ref, b_ref, o_ref):
    c = c_ref[...]
    a = (c * _sigmoid(c)).astype(BF16)
    o_ref[...] = jnp.dot(a, w_ref[...].astype(BF16), preferred_element_type=F32) + b_ref[...]


def _modulation(c, w_ada, b_ada):
    n, d = c.shape
    nout = w_ada.shape[1]
    tn = 1024
    return pl.pallas_call(
        _mod_kernel,
        out_shape=jax.ShapeDtypeStruct((n, nout), F32),
        grid=(nout // tn,),
        in_specs=[pl.BlockSpec((n, d), lambda j: (0, 0)),
                  pl.BlockSpec((d, tn), lambda j: (0, j)),
                  pl.BlockSpec((1, tn), lambda j: (0, j))],
        out_specs=pl.BlockSpec((n, tn), lambda j: (0, j)),
        compiler_params=_cparams("arbitrary"),
        name="modulation",
    )(c, w_ada, b_ada.reshape(1, nout))


def _store_heads(ref, c0, y):
    for j in range(y.shape[1] // HG_DK):
        ref[c0 // HG_DK + j] = y[:, j * HG_DK:(j + 1) * HG_DK].astype(ref.dtype)


def _stage_kv(scr, kv, y):
    for h in range(HEADS_PER_GROUP):
        slot = (kv * HEADS_PER_GROUP + h) * SUBLANES
        for j in range(y.shape[0] // SUBLANES):
            scr[j * 64 + slot:j * 64 + slot + SUBLANES, :] = y[j * SUBLANES:(j + 1) * SUBLANES,
                                                               h * HEAD_DIM:(h + 1) * HEAD_DIM]


def _emit_kv(out_ref, scr, gb, rb, n):
    per_seq = n // SUBLANES
    for s in range(gb):
        for jj in range(per_seq):
            src = (s * (rb // SUBLANES) + (rb - n) // SUBLANES + jj) * 64
            for i in range(SUBLANES):
                out_ref[s, jj * 64 + i * SUBLANES:jj * 64 + (i + 1) * SUBLANES, :] = (
                    scr[pl.ds(src + i, SUBLANES, stride=SUBLANES), :])


def _inproj_kernel(x_ref, mod_ref, w_ref, lbp_ref, attn_ref, hq_ref, hk_ref, hv_ref, logf_ref, og_ref,
                   gates_ref, *refs, layer, kv_rows):
    kv_refs, kv_scr = refs[:N_GROUPS], refs[N_GROUPS]
    gb, rb, d = x_ref.shape
    tm = gb * rb
    hg_w = hq_ref.shape[0] * HG_DK
    act = attn_ref.dtype
    sh = mod_ref[:, :, 0:d]
    sc = mod_ref[:, :, d:2 * d]
    u = (x_ref[...] * (1.0 + sc) + sh).reshape(tm, d).astype(BF16)

    def proj(c0):
        return jnp.dot(u, w_ref[:, c0:c0 + COL_CHUNK], preferred_element_type=F32)

    assert COL_CHUNK == GROUP_W
    order = list(range(N_GROUPS)) + [s * N_GROUPS + g for g in range(N_GROUPS) for s in (1, 2)]
    for chunk in order:
        c0 = chunk * COL_CHUNK
        y = proj(c0)
        attn_ref[:, c0:c0 + COL_CHUNK] = y.astype(act)
        section, g = divmod(chunk, N_GROUPS)
        if section:
            _stage_kv(kv_scr, section - 1, y)
        if section == 2:
            _emit_kv(kv_refs[g], kv_scr, gb, rb, min(kv_rows[g], rb))
    base = 3 * ATTN_W
    for c0 in range(0, hg_w, COL_CHUNK):
        y = proj(base + c0)
        _store_heads(hq_ref, c0, y * _sigmoid(y))
    lbp = lbp_ref[...]
    e = jnp.exp(lbp - jnp.max(lbp, axis=0, keepdims=True))
    lb = jnp.sum(e[0:layer + 1], axis=0, keepdims=True) / jnp.sum(e, axis=0, keepdims=True)
    base += hg_w
    for c0 in range(0, hg_w, COL_CHUNK):
        lbc = lb[:, c0:c0 + COL_CHUNK]
        f = lbc + (1.0 - lbc) * _sigmoid(proj(base + c0))
        _store_heads(logf_ref, c0, jnp.log(f))
        _store_heads(hk_ref, c0, 1.0 - f)
    base += hg_w
    for c0 in range(0, hg_w, COL_CHUNK):
        _store_heads(hv_ref, c0, proj(base + c0))
    base += hg_w
    for c0 in range(0, hg_w, COL_CHUNK):
        y = proj(base + c0)
        _store_heads(og_ref, c0, y * _sigmoid(y))
    base += hg_w
    for c0 in range(0, 2 * d, COL_CHUNK):
        gates_ref[:, c0:c0 + COL_CHUNK] = _sigmoid(proj(base + c0)).astype(act)


def _in_projection(x3, mod3, w_in, lb_param, layer, gb, rb, act, hosted=(None, None, None)):
    g, r, d = x3.shape
    n = g * r
    tm = gb * rb
    hg_w = lb_param.shape[1]
    rt = r // rb
    heads = hg_w // HG_DK
    rows = lambda width: pl.BlockSpec((tm, width), lambda gi, ri: (gi * rt + ri, 0))
    by_head = pl.BlockSpec((heads, tm, HG_DK), lambda gi, ri: (0, gi * rt + ri, 0))
    head_major = lambda dtype: jax.ShapeDtypeStruct((heads, n, HG_DK), dtype)
    out_shapes = (
        jax.ShapeDtypeStruct((n, 3 * ATTN_W), act),
        head_major(act),
        head_major(act),
        head_major(act),
        head_major(F32),
        head_major(act),
        jax.ShapeDtypeStruct((n, 2 * d), act),
    )
    rp = 2 * HEADS_PER_GROUP
    kv_rows = tuple(min(window, r) for window, dil in ATTN_GROUPS)
    kv_specs = []
    for kept in kv_rows:
        assert (r - max(kept, rb)) % rb == 0
        first_tile = (r - max(kept, rb)) // rb
        kv_specs.append(pl.BlockSpec((gb, min(kept, rb) * rp, HEAD_DIM),
                                     lambda gi, ri, first_tile=first_tile: (gi, jnp.maximum(ri - first_tile, 0), 0)))
    out_shapes += tuple(jax.ShapeDtypeStruct((g, kept * rp, HEAD_DIM), F32) for kept in kv_rows)
    return _hosting_call(
        functools.partial(_inproj_kernel, layer=layer, kv_rows=kv_rows), *hosted,
        out_shape=out_shapes,
        grid=(g // gb, rt),
        in_specs=[pl.BlockSpec((gb, rb, d), lambda gi, ri: (gi, ri, 0)),
                  pl.BlockSpec((gb, 1, mod3.shape[2]), lambda gi, ri: (gi, 0, 0)),
                  _resident(w_in.shape),
                  _resident(lb_param.shape)],
        out_specs=(rows(3 * ATTN_W), by_head, by_head, by_head, by_head, by_head, rows(2 * d), *kv_specs),
        scratch_shapes=[pltpu.VMEM((tm * rp, HEAD_DIM), F32)],
        args=(x3, mod3, w_in, lb_param),
        compiler_params=_cparams("arbitrary", "arbitrary"),
        name="in_projection",
    )


ATTN_BLOCKS_PER_ITER = 8


def _attn_prompt_kernel(q_ref, k_ref, v_ref, attn_ref, qf, kf, vf, mf, lf, af, *, head_axis, group_axis):
    s = attn_ref.shape[0]
    nblk = s // BAND
    head = pl.program_id(head_axis)
    group = pl.program_id(group_axis)
    qi = lax.broadcasted_iota(jnp.int32, (BAND, BAND), 0)
    kj = lax.broadcasted_iota(jnp.int32, (BAND, BAND), 1)
    steps_own = qi - kj
    steps_prev = steps_own + BAND
    scale = HEAD_DIM ** -0.5
    ones = jnp.ones((BAND, HEAD_DIM), BF16)
    qf[...] = q_ref[...].astype(F32)
    kf[...] = k_ref[...].astype(F32)
    vf[...] = v_ref[...].astype(F32)
    for g, (window, dil) in enumerate(ATTN_GROUPS):
        nb = nblk // dil
        slope = jnp.float32(_alibi_slope(g, HEADS_PER_GROUP - 1) * dil)
        for h in range(HEADS_PER_GROUP - 1):
            slope = jnp.where(head == h, jnp.float32(_alibi_slope(g, h) * dil), slope)
        bias_own = jnp.where(steps_own >= 0, -slope * steps_own.astype(F32), NEG)
        bias_prev = jnp.where(steps_prev <= BAND, -slope * steps_prev.astype(F32), NEG)

        def body(it, carry, g=g, dil=dil, nb=nb, bias_own=bias_own, bias_prev=bias_prev):
            span = min(nb, ATTN_BLOCKS_PER_ITER)
            blocks = []
            for u in range(ATTN_BLOCKS_PER_ITER):
                idx = it * ATTN_BLOCKS_PER_ITER + u
                r = idx // nb
                i = idx - r * nb
                start = r + i * (BAND * dil)
                rows = pl.ds(start, BAND, stride=dil) if dil > 1 else pl.ds(start, BAND)
                if u % span:
                    blocks.append((rows, "chain", None, True))
                elif span == nb:
                    blocks.append((rows, "none", None, False))
                else:
                    start_prev = r + jnp.maximum(i - 1, 0) * (BAND * dil)
                    rows_prev = pl.ds(start_prev, BAND, stride=dil) if dil > 1 else pl.ds(start_prev, BAND)
                    blocks.append((rows, "load", rows_prev, i > 0))
            pair = lambda y, z: lax.dot_general(y, z, NT_DIMS, preferred_element_type=F32)
            scores, keys = [], []
            for rows, how, rows_prev, has_prev in blocks:
                q = qf[rows, :].astype(BF16)
                keys.append(kf[rows, :].astype(BF16))
                if how == "none":
                    scores.append((pair(q, keys[-1]), None))
                else:
                    k_prev = keys[-2] if how == "chain" else kf[rows_prev, :].astype(BF16)
                    scores.append((pair(q, keys[-1]), pair(q, k_prev)))
            probs = []
            for (rows, how, rows_prev, has_prev), (s_own, s_prev) in zip(blocks, scores):
                s_own = s_own * scale + bias_own
                if s_prev is None:
                    m = jnp.max(s_own, axis=-1, keepdims=True)
                    probs.append((m, jnp.exp(s_own - m).astype(BF16), None))
                else:
                    s_prev = s_prev * scale + bias_prev
                    if how == "load":
                        s_prev = jnp.where(has_prev, s_prev, NEG)
                    m = jnp.max(jnp.maximum(s_own, s_prev), axis=-1, keepdims=True)
                    probs.append((m, jnp.exp(s_own - m).astype(BF16), jnp.exp(s_prev - m).astype(BF16)))
            results, values = [], []
            for (rows, how, rows_prev, has_prev), (m, p_own, p_prev) in zip(blocks, probs):
                values.append(jnp.concatenate([vf[rows, :].astype(BF16), ones], axis=1))
                both = jnp.dot(p_own, values[-1], preferred_element_type=F32)
                if p_prev is not None:
                    v_prev = values[-2] if how == "chain" else jnp.concatenate(
                        [vf[rows_prev, :].astype(BF16), ones], axis=1)
                    both = both + jnp.dot(p_prev, v_prev, preferred_element_type=F32)
                acc, l = both[:, :HEAD_DIM], both[:, HEAD_DIM:]
                if g < N_GROUPS - 1:
                    m_old = mf[rows, :]
                    m_new = jnp.maximum(m_old, m)
                    a_old = jnp.exp(m_old - m_new)
                    a_blk = jnp.exp(m - m_new)
                    l = a_old * lf[rows, :] + a_blk * l
                    acc = a_old * af[rows, :] + a_blk * acc
                    m = m_new
                results.append((rows, m, l, acc))
            for rows, m, l, acc in results:
                mf[rows, :] = jnp.broadcast_to(m, (BAND, HEAD_DIM))
                lf[rows, :] = l
                af[rows, :] = acc
            return carry

        @pl.when(group == N_GROUPS - 1 - g)
        def _(body=body):
            lax.fori_loop(0, nblk // ATTN_BLOCKS_PER_ITER, body, 0)

    @pl.when(group == N_GROUPS - 1)
    def _():
        attn_ref[...] = (af[...] / lf[...]).astype(attn_ref.dtype)


def _attention_prompt(qkv, b, s, hosted=(None, None, None)):
    hpq = ATTN_W // HEAD_DIM
    in_specs = [pl.BlockSpec((s, HEAD_DIM), lambda bi, h, g, section=section:
                             (bi, section * hpq + (N_GROUPS - 1 - g) * HEADS_PER_GROUP + h))
                for section in range(3)]
    (attn,), dst = _hosting_call(
        functools.partial(_attn_prompt_kernel, head_axis=1, group_axis=2), *hosted,
        out_shape=(jax.ShapeDtypeStruct((b * s, GROUP_W), BF16),),
        grid=(b, HEADS_PER_GROUP, N_GROUPS),
        in_specs=in_specs,
        out_specs=(pl.BlockSpec((s, HEAD_DIM), lambda bi, h, g: (bi, h)),),
        scratch_shapes=[pltpu.VMEM((s, HEAD_DIM), F32) for _ in range(6)],
        args=[qkv] * 3,
        compiler_params=_cparams("arbitrary", "arbitrary", "arbitrary"),
        name="attn_prompt",
    )
    return attn, dst


def _attn_sample_kernel(*refs, group, dil, n_steps, length, seqs):
    first = group == 0
    last = group == N_GROUPS - 1
    q_ref, newkv_ref, cache_ref = refs[:3]
    pos = 3
    if not first:
        m_in, l_in, a_in = refs[pos:pos + 3]
        pos += 3
    pos += 1
    newc_ref = refs[pos]
    outs = refs[pos + 1:]
    t = q_ref.shape[0] // seqs
    rp = 2 * HEADS_PER_GROUP
    stretch = max(dil, t)
    nkeys = length // stretch * t
    j_c = lax.broadcasted_iota(jnp.int32, (t, nkeys), 0)
    c_c = lax.broadcasted_iota(jnp.int32, (t, nkeys), 1)
    dist_c = length + j_c - ((c_c // t) * stretch + (c_c & (t - 1)))
    ok_c = ((dist_c & (dil - 1)) == 0) & (dist_c <= n_steps * dil)
    j_n = lax.broadcasted_iota(jnp.int32, (t, t), 0)
    r_n = lax.broadcasted_iota(jnp.int32, (t, t), 1)
    dist_n = j_n - r_n
    ok_n = (dist_n >= 0) & ((dist_n & (dil - 1)) == 0) & (dist_n <= n_steps * dil)
    scale = HEAD_DIM ** -0.5
    qoff = group * GROUP_W
    cache_rows = lambda s, first_row: cache_ref[s, :, pl.ds(first_row, t, stride=rp), :].reshape(
        nkeys, HEAD_DIM).astype(BF16)
    new_rows = lambda s, first_row: newkv_ref[s, pl.ds(first_row, t, stride=rp), :].astype(BF16)
    chains = [(s, h) for s in range(seqs) for h in range(HEADS_PER_GROUP)]
    scores = []
    for s, h in chains:
        q = q_ref[s * t:(s + 1) * t, qoff + h * HEAD_DIM:qoff + (h + 1) * HEAD_DIM].astype(BF16)
        scores.append((lax.dot_general(q, cache_rows(s, h), NT_DIMS, preferred_element_type=F32),
                       lax.dot_general(q, new_rows(s, h), NT_DIMS, preferred_element_type=F32)))
    probs = []
    for (s, h), (s_c, s_n) in zip(chains, scores):
        rows, sl = slice(s * t, (s + 1) * t), slice(h * HEAD_DIM, (h + 1) * HEAD_DIM)
        slope = _alibi_slope(group, h)
        s_c = jnp.where(ok_c, s_c * scale - slope * dist_c.astype(F32), NEG)
        s_n = jnp.where(ok_n, s_n * scale - slope * dist_n.astype(F32), NEG)
        m = jnp.maximum(jnp.max(s_c, axis=-1, keepdims=True), jnp.max(s_n, axis=-1, keepdims=True))
        if not first:
            m = jnp.maximum(m, m_in[rows, sl][:, 0:1])
        p_c = jnp.exp(s_c - m)
        p_n = jnp.exp(s_n - m)
        l = jnp.sum(p_c, axis=-1, keepdims=True) + jnp.sum(p_n, axis=-1, keepdims=True)
        probs.append((m, l, p_c.astype(BF16), p_n.astype(BF16)))
    for (s, h), (m, l, p_c, p_n) in zip(chains, probs):
        rows, sl = slice(s * t, (s + 1) * t), slice(h * HEAD_DIM, (h + 1) * HEAD_DIM)
        acc = (jnp.dot(p_c, cache_rows(s, HEADS_PER_GROUP + h), preferred_element_type=F32)
               + jnp.dot(p_n, new_rows(s, HEADS_PER_GROUP + h), preferred_element_type=F32))
        if not first:
            a_old = jnp.exp(m_in[rows, sl] - m)
            l = a_old * l_in[rows, sl] + l
            acc = a_old * a_in[rows, sl] + acc
        if last:
            outs[0][rows, sl] = acc / l
        else:
            outs[0][rows, sl] = jnp.broadcast_to(m, (t, HEAD_DIM))
            outs[1][rows, sl] = jnp.broadcast_to(l, (t, HEAD_DIM))
            outs[2][rows, sl] = acc
    for s in range(seqs):
        newc_ref[s] = newkv_ref[s]


SAMPLE_SEQS_PER_STEP = (8, 4, 2)


def _attention_sample(qkv, new_kv, cache, shifted, layer, running, t, group):
    window, dil = ATTN_GROUPS[group]
    depth, bd, crow, dh = cache.shape
    rp = 2 * HEADS_PER_GROUP
    length = crow // rp
    stretch = max(dil, t)
    seqs = math.gcd(SAMPLE_SEQS_PER_STEP[group], bd)
    first = group == 0
    last = group == N_GROUPS - 1
    rows = pl.BlockSpec((seqs * t, GROUP_W), lambda bi: (bi, 0))
    stat = jax.ShapeDtypeStruct((bd * t, GROUP_W), F32)
    new_blk = pl.BlockSpec((seqs, t * rp, dh), lambda bi: (bi, 0, 0))
    in_specs = [pl.BlockSpec((seqs * t, 3 * ATTN_W), lambda bi: (bi, 0)),
                new_blk,
                pl.BlockSpec((None, seqs, length // stretch, t * rp, dh), lambda bi: (layer, bi, 0, 0, 0))]
    args = [qkv, new_kv, cache.reshape(depth, bd, length // stretch, stretch * rp, dh)]
    if not first:
        in_specs += [rows, rows, rows]
        args += list(running)
    in_specs.append(pl.BlockSpec(memory_space=pl.ANY))
    args.append(shifted)
    n_out = 1 if last else 3
    res = pl.pallas_call(
        functools.partial(_attn_sample_kernel, group=group, dil=dil, n_steps=window // dil, length=length,
                          seqs=seqs),
        out_shape=(jax.ShapeDtypeStruct(shifted.shape, shifted.dtype),) + (stat,) * n_out,
        grid=(bd // seqs,),
        in_specs=in_specs,
        out_specs=(pl.BlockSpec((seqs, t * rp, dh), lambda bi: (bi, length // t - 1, 0)),) + (rows,) * n_out,
        input_output_aliases={len(args) - 1: 0},
        compiler_params=_cparams("arbitrary"),
        name=f"attn_sample_g{group}",
    )(*args)
    return res[0], (res[1] if last else tuple(res[1:]))


def _cumsum_rows(x, row):
    d = 1
    while d < x.shape[0]:
        x = x + jnp.where(row >= d, pltpu.roll(x, d, 0), 0.0)
        d *= 2
    return x


def _hgrn_carry(q, k, v, g, st):
    c = q.shape[0]
    qg = (q * jnp.exp(g)).astype(BF16)
    o = lax.dot_general(qg, st.astype(BF16), NT_DIMS, preferred_element_type=F32)
    g_last = g[c - 1:c, :]
    kd = (k * jnp.exp(g_last - g)).astype(BF16)
    st_new = st * jnp.exp(g_last) + lax.dot_general(v.astype(BF16), kd, TN_DIMS, preferred_element_type=F32)
    return o, st_new


def _hgrn_chunks_pairwise(qs, ks, vs, lfs, states):
    c = qs[0].shape[0]
    row = lax.broadcasted_iota(jnp.int32, qs[0].shape, 0)
    gs = [_cumsum_rows(lf, row) for lf in lfs]
    outs = []
    for q, k, v, g in zip(qs, ks, vs, gs):
        o = jnp.sum(q * k, axis=-1, keepdims=True) * v
        for d in range(1, c):
            ok = row >= d
            e = jnp.exp(jnp.where(ok, g - pltpu.roll(g, d, 0), 0.0))
            a = jnp.sum(jnp.where(ok, q * pltpu.roll(k, d, 0) * e, 0.0), axis=-1, keepdims=True)
            o = o + a * pltpu.roll(v, d, 0)
        outs.append(o)
    cols = []
    for k, g in zip(ks, gs):
        g_last = g[c - 1:c, :]
        kd = k * jnp.exp(g_last - g)
        cols.append(jnp.concatenate([kd, jnp.broadcast_to(jnp.exp(g_last), kd.shape)], axis=0).T)
    carried = [jnp.dot((q * jnp.exp(g)).astype(BF16), s.astype(BF16), preferred_element_type=F32)
               for q, g, s in zip(qs, gs, states)]
    new_states = [s * col[:, c:c + 1] + jnp.dot(col[:, 0:c].astype(BF16), v.astype(BF16),
                                                preferred_element_type=F32)
                  for s, col, v in zip(states, cols, vs)]
    return [o + oc for o, oc in zip(outs, carried)], new_states


def _pair_level(c):
    ri = lax.broadcasted_iota(jnp.int32, (c, c), 0)
    ci = lax.broadcasted_iota(jnp.int32, (c, c), 1)
    x = ri ^ ci
    bits = jnp.zeros((c, c), jnp.int32)
    p = 1
    while p < c:
        bits = bits + (x >= p).astype(jnp.int32)
        p *= 2
    return jnp.where(ri > ci, bits + 1, jnp.where(ri == ci, 1, 0))


def _hgrn_chunks_blocked(qs, ks, vs, lfs, sts, g_refs, level):
    nh = len(qs)
    c = qs[0].shape[0]
    row = lax.broadcasted_iota(jnp.int32, (c, HG_DK), 0)
    sub = row & (SUBLANES - 1)
    sub8 = lax.broadcasted_iota(jnp.int32, (SUBLANES, HG_DK), 0)
    gs = []
    for lf, g_ref in zip(lfs, g_refs):
        x = lf
        for d in (1, 2, 4):
            x = x + jnp.where(sub >= d, pltpu.roll(x, d, 0), 0.0)
        g_ref[...] = x
        pieces = [x[0:SUBLANES]]
        off = None
        for j in range(1, c // SUBLANES):
            tot = jnp.broadcast_to(g_ref[j * SUBLANES - 1:j * SUBLANES, :], (SUBLANES, HG_DK))
            off = tot if off is None else off + tot
            pieces.append(x[j * SUBLANES:(j + 1) * SUBLANES] + off)
        g = jnp.concatenate(pieces, axis=0)
        g_ref[...] = g
        gs.append(g)

    def mid_rows(g_ref, n):
        half = n // 2
        bcast = lambda r: jnp.broadcast_to(g_ref[r:r + 1, :], (SUBLANES, HG_DK))
        out = []
        for j in range(c // SUBLANES):
            base = j * SUBLANES
            if n >= SUBLANES:
                out.append(bcast(base // n * n + half - 1))
            else:
                assert n == 4
                out.append(jnp.where(sub8 < 4, bcast(base + 1), bcast(base + 5)))
        return jnp.concatenate(out, axis=0)

    pair = lambda y, z: lax.dot_general(y, z, NT_DIMS, preferred_element_type=F32)
    prods = [[pair(q.astype(BF16), k.astype(BF16))] for q, k in zip(qs, ks)]
    n = 2
    while n <= c:
        second = (row & (n // 2)) != 0
        for u in range(nh):
            if n == 2:
                arg = jnp.where(second, lfs[u], 0.0)
            else:
                gmid = mid_rows(g_refs[u], n)
                arg = jnp.where(second, gs[u] - gmid, gmid - gs[u])
            y = (jnp.where(second, qs[u], ks[u]) * jnp.exp(arg)).astype(BF16)
            prods[u].append(pair(y, y))
        n *= 2
    outs = []
    for u in range(nh):
        amat = jnp.where(level == 1, prods[u][0], 0.0)
        for j, p in enumerate(prods[u][1:]):
            amat = jnp.where(level == j + 2, p, amat)
        outs.append(jnp.dot(amat.astype(BF16), vs[u].astype(BF16), preferred_element_type=F32))
    states = []
    for u in range(nh):
        o_carry, st_new = _hgrn_carry(qs[u], ks[u], vs[u], gs[u], sts[u])
        outs[u] = outs[u] + o_carry
        states.append(st_new)
    return outs, states


def _hgrn_output(o, og, nw):
    return o * lax.rsqrt(jnp.mean(o * o, axis=-1, keepdims=True) + RMS_EPS) * nw * og


HGRN_HEADS_PER_ITER = 4


HGRN_CHUNKS_PER_STEP = 2


def _hgrn_prompt_kernel(q_ref, k_ref, v_ref, lf_ref, og_ref, nw_ref, hg_ref, sfin_ref, st_ref, g_ref):
    ci = pl.program_id(1)
    heads = q_ref.shape[0]
    c = g_ref.shape[1]
    head_groups = heads // HGRN_HEADS_PER_ITER

    @pl.when(ci == 0)
    def _():
        st_ref[...] = jnp.zeros_like(st_ref)

    level = _pair_level(c)

    def head_group(it, carry):
        i = it % head_groups
        rows = pl.ds(pl.multiple_of(it // head_groups * c, c), c)
        hs = [i * HGRN_HEADS_PER_ITER + u for u in range(HGRN_HEADS_PER_ITER)]
        load = lambda ref: [ref[h, rows, :].astype(F32) for h in hs]
        outs, states = _hgrn_chunks_blocked(load(q_ref), load(k_ref), load(v_ref), load(lf_ref),
                                            [st_ref[h] for h in hs],
                                            [g_ref.at[u] for u in range(HGRN_HEADS_PER_ITER)], level)
        for h, o, st_new in zip(hs, outs, states):
            st_ref[h] = st_new
            hg_ref[h, rows, :] = _hgrn_output(o, og_ref[h, rows, :].astype(F32), nw_ref[...]).astype(hg_ref.dtype)
        return carry

    lax.fori_loop(0, q_ref.shape[1] // c * head_groups, head_group, 0)

    @pl.when(ci == pl.num_programs(1) - 1)
    def _():
        for h in range(heads):
            sfin_ref[0, h] = st_ref[h].T


def _hgrn_prompt(hq, hk, hv, logf, og, norm_w, b, s, chunk, hosted=(None, None, None)):
    heads, n, _ = hq.shape
    rows = chunk * math.gcd(HGRN_CHUNKS_PER_STEP, s // chunk)
    nc = s // rows
    blk = pl.BlockSpec((heads, rows, HG_DK), lambda bi, ci: (0, bi * nc + ci, 0))
    (hg, s_fin), dst = _hosting_call(
        _hgrn_prompt_kernel, *hosted,
        out_shape=(jax.ShapeDtypeStruct((heads, n, HG_DV), BF16),
                   jax.ShapeDtypeStruct((b, heads, HG_DK, HG_DV), F32)),
        grid=(b, nc),
        in_specs=[blk, blk, blk, blk, blk, pl.BlockSpec((1, HG_DV), lambda bi, ci: (0, 0))],
        out_specs=(blk, pl.BlockSpec((1, heads, HG_DK, HG_DV), lambda bi, ci: (bi, 0, 0, 0))),
        scratch_shapes=[pltpu.VMEM((heads, HG_DV, HG_DK), F32),
                        pltpu.VMEM((HGRN_HEADS_PER_ITER, chunk, HG_DK), F32)],
        args=(hq, hk, hv, logf, og, norm_w.reshape(1, HG_DV)),
        compiler_params=_cparams("arbitrary", "arbitrary"),
        name="hgrn_prompt",
    )
    return hg, s_fin, dst


HGRN_SAMPLE_SEQS_PER_STEP = 2


def _hgrn_sample_kernel(q_ref, k_ref, v_ref, lf_ref, og_ref, nw_ref, s_ref, hg_ref, snew_ref):
    seqs, heads = s_ref.shape[0], s_ref.shape[1]
    t = q_ref.shape[1] // seqs
    chains = [(s, h) for s in range(seqs) for h in range(heads)]
    load = lambda ref: [ref[h, s * t:(s + 1) * t, :] for s, h in chains]
    outs, new_states = _hgrn_chunks_pairwise(load(q_ref), load(k_ref), load(v_ref), load(lf_ref),
                                             [s_ref[s, h] for s, h in chains])
    for (s, h), o, st_new in zip(chains, outs, new_states):
        snew_ref[s, h] = st_new
        hg_ref[h, s * t:(s + 1) * t, :] = _hgrn_output(o, og_ref[h, s * t:(s + 1) * t, :], nw_ref[...])


def _hgrn_sample(hq, hk, hv, logf, og, norm_w, state, t):
    bd, heads = state.shape[0], state.shape[1]
    seqs = math.gcd(HGRN_SAMPLE_SEQS_PER_STEP, bd)
    blk = pl.BlockSpec((heads, seqs * t, HG_DK), lambda bi: (0, bi, 0))
    sblk = pl.BlockSpec((seqs, heads, HG_DK, HG_DV), lambda bi: (bi, 0, 0, 0))
    return pl.pallas_call(
        _hgrn_sample_kernel,
        out_shape=(jax.ShapeDtypeStruct((heads, bd * t, HG_DV), F32),
                   jax.ShapeDtypeStruct(state.shape, state.dtype)),
        grid=(bd // seqs,),
        in_specs=[blk, blk, blk, blk, blk, pl.BlockSpec((1, HG_DV), lambda bi: (0, 0)), sblk],
        out_specs=(blk, sblk),
        compiler_params=_cparams("arbitrary"),
        name="hgrn_sample",
    )(hq, hk, hv, logf, og, norm_w.reshape(1, HG_DV), state)


def _merge_kernel(x_ref, mod_ref, attn_ref, hg_ref, gates_ref, wa_ref, wb_ref, wo_ref, lng_ref, lnb_ref,
                  x1_ref, *, alpha):
    gb, rb, d = x_ref.shape
    tm = gb * rb
    hg = jnp.concatenate([hg_ref[h].astype(BF16) for h in range(hg_ref.shape[0])], axis=1)
    branch_a = jnp.dot(attn_ref[...].astype(BF16), wa_ref[...], preferred_element_type=F32)
    branch_b = jnp.dot(hg, wb_ref[...], preferred_element_type=F32)
    gate_a = gates_ref[:, 0:d].astype(F32)
    gate_b = gates_ref[:, d:2 * d].astype(F32)
    merged = gate_a * branch_a + gate_b * branch_b
    mix = jnp.dot(merged.astype(BF16), wo_ref[...], preferred_element_type=F32)
    g1 = mod_ref[:, :, 2 * d:3 * d]
    h = alpha * x_ref[...] + g1 * mix.reshape(gb, rb, d)
    x1_ref[...] = _layer_norm(h, lng_ref[...], lnb_ref[...])


def _merge(x3, mod3, attn, hg, gates, wa, wb, wo, ln_g, ln_b, gb, rb, alpha, hosted=(None, None, None)):
    g, r, d = x3.shape
    tm = gb * rb
    rt = r // rb
    row = lambda gi, ri: (gi * rt + ri, 0)
    rows = lambda a: pl.BlockSpec((tm, a.shape[1]), row)
    tile3 = pl.BlockSpec((gb, rb, d), lambda gi, ri: (gi, ri, 0))
    (x1,), dst = _hosting_call(
        functools.partial(_merge_kernel, alpha=alpha), *hosted,
        out_shape=(jax.ShapeDtypeStruct(x3.shape, F32),),
        grid=(g // gb, rt),
        in_specs=[tile3, pl.BlockSpec((gb, 1, mod3.shape[2]), lambda gi, ri: (gi, 0, 0)),
                  rows(attn), pl.BlockSpec((hg.shape[0], tm, hg.shape[2]), lambda gi, ri: (0, gi * rt + ri, 0)),
                  rows(gates),
                  _resident(wa.shape), _resident(wb.shape), _resident(wo.shape),
                  _resident((1, d)), _resident((1, d))],
        out_specs=(tile3,),
        args=(x3, mod3, attn, hg, gates, wa, wb, wo, ln_g.reshape(1, d), ln_b.reshape(1, d)),
        compiler_params=_cparams("arbitrary", "arbitrary"),
        name="merge",
    )
    return x1, dst


def _mlp_kernel(x1_ref, mod_ref, wu_ref, bu_ref, wd_ref, bd_ref, lng_ref, lnb_ref, x2_ref, *, alpha):
    gb, rb, d = x1_ref.shape
    tm = gb * rb
    dff = wu_ref.shape[1]
    x1 = x1_ref[...]
    sh = mod_ref[:, :, 3 * d:4 * d]
    sc = mod_ref[:, :, 4 * d:5 * d]
    g2 = mod_ref[:, :, 5 * d:6 * d]
    u = (x1 * (1.0 + sc) + sh).reshape(tm, d).astype(BF16)
    hidden = []
    for c0 in range(0, dff, COL_CHUNK):
        hid = jnp.dot(u, wu_ref[:, c0:c0 + COL_CHUNK], preferred_element_type=F32) + bu_ref[:, c0:c0 + COL_CHUNK]
        hidden.append(jnp.square(jnp.maximum(hid, 0.0)).astype(BF16))
    ff = jnp.dot(jnp.concatenate(hidden, axis=1), wd_ref[...], preferred_element_type=F32) + bd_ref[...]
    h = alpha * x1 + g2 * ff.reshape(gb, rb, d)
    x2_ref[...] = _layer_norm(h, lng_ref[...], lnb_ref[...])


def _mlp(x3, mod3, wu, bu, wd, bd, ln_g, ln_b, gb, rb, alpha, hosted=(None, None, None)):
    g, r, d = x3.shape
    dff = wu.shape[1]
    tile3 = pl.BlockSpec((gb, rb, d), lambda gi, ri: (gi, ri, 0))
    (x2,), dst = _hosting_call(
        functools.partial(_mlp_kernel, alpha=alpha), *hosted,
        out_shape=(jax.ShapeDtypeStruct(x3.shape, F32),),
        grid=(g // gb, r // rb),
        in_specs=[tile3, pl.BlockSpec((gb, 1, mod3.shape[2]), lambda gi, ri: (gi, 0, 0)),
                  _resident(wu.shape), _resident((1, dff)), _resident(wd.shape), _resident((1, d)),
                  _resident((1, d)), _resident((1, d))],
        out_specs=(tile3,),
        args=(x3, mod3, wu, bu.reshape(1, dff), wd, bd.reshape(1, d), ln_g.reshape(1, d), ln_b.reshape(1, d)),
        compiler_params=_cparams("arbitrary", "arbitrary"),
        name="mlp",
    )
    return x2, dst


PROMPT_ROWS = 256
SAMPLE_SEQS = 32
HGRN_CHUNK = 128


def _prompt_layer(x, mod, caches, t_new, layer, w, alpha):
    b, s, d = x.shape
    mod3 = mod.reshape(b, 1, mod.shape[1])
    gb, rb = 1, min(PROMPT_ROWS, s)
    chunk = min(HGRN_CHUNK, s)
    rp = 2 * HEADS_PER_GROUP
    bd = caches[0].shape[1]
    tile_steps = s // rb * b
    hgrn_steps = s // chunk // math.gcd(HGRN_CHUNKS_PER_STEP, s // chunk) * b

    def plan(gi, b0, nbatch, n_steps):
        rows = caches[gi].shape[2]
        return _plan_shift(layer, b0, nbatch, n_steps, t_new * rp, rows - t_new * rp)

    attn_steps = b * HEADS_PER_GROUP * N_GROUPS
    n_hgrn = bd // 4
    n_attn = (bd - n_hgrn) // 2
    big = None

    def share(b0, nbatch, n_steps):
        return (plan(2, b0, nbatch, n_steps), caches[2], big) if nbatch else (None, None, big)

    (qkv, hq, hk, hv, logf, og, gates, *kv_tails), mid = _in_projection(
        x, mod3, w["w_in"], w["lb_param"], layer, gb, rb, BF16,
        hosted=(plan(1, 0, bd, tile_steps), caches[1], None))
    hg, s_fin, big = _hgrn_prompt(hq, hk, hv, logf, og, w["hg_norm_w"], b, s, chunk,
                                  hosted=share(0, n_hgrn, hgrn_steps))
    attn, big = _attention_prompt(qkv, b, s, hosted=share(n_hgrn, n_attn, attn_steps))
    new_bufs = [kv.reshape(b, kv.shape[1] // rp, 2, HEADS_PER_GROUP, HEAD_DIM) for kv in kv_tails]
    x1, small = _merge(x, mod3, attn, hg, gates, w["w_branch_a"], w["w_branch_b"], w["w_out"],
                       w["ln1_g"], w["ln1_b"], gb, rb, alpha, hosted=(plan(0, 0, bd, tile_steps), caches[0], None))
    x2, big = _mlp(x1, mod3, w["w_up"], w["b_up"], w["w_down"], w["b_down"], w["ln2_g"], w["ln2_b"], gb, rb,
                   alpha, hosted=share(n_hgrn + n_attn, bd - n_hgrn - n_attn, tile_steps))
    return x2, new_bufs, s_fin, (small, mid, big)


def _sample_layer(x, mod, caches, shifted, states, layer, w, alpha):
    b, s, d = x.shape
    mod3 = mod.reshape(b, 1, mod.shape[1])
    gb, rb = min(SAMPLE_SEQS, b), s
    rp = 2 * HEADS_PER_GROUP
    (qkv, hq, hk, hv, logf, og, gates, *new_kv), _ = _in_projection(
        x, mod3, w["w_in"], w["lb_param"], layer, gb, rb, F32)
    new_bufs = []
    running = None
    for gi in range(N_GROUPS):
        buf, running = _attention_sample(qkv, new_kv[gi], caches[gi], shifted[gi], layer, running, s, gi)
        new_bufs.append(buf)
    hg, s_fin = _hgrn_sample(hq, hk, hv, logf, og, w["hg_norm_w"], states[layer], s)
    x1, _ = _merge(x, mod3, running, hg, gates, w["w_branch_a"], w["w_branch_b"], w["w_out"],
                   w["ln1_g"], w["ln1_b"], gb, rb, alpha)
    x2, _ = _mlp(x1, mod3, w["w_up"], w["b_up"], w["w_down"], w["b_down"], w["ln2_g"], w["ln2_b"], gb, rb, alpha)
    return x2, new_bufs, s_fin


def _stack_layers(per_layer):
    return per_layer[0][None] if len(per_layer) == 1 else jnp.stack(per_layer)


def kernel(x_prompt, x_sample, c_prompt, c_sample, cache_kv_w128, cache_kv_w512, cache_kv_w2048, state_hgrn,
           w_ada, b_ada, w_in, lb_param, hg_norm_w, w_branch_a, w_branch_b, w_out, ln1_g, ln1_b, w_up, b_up,
           w_down, b_down, ln2_g, ln2_b):
    depth = w_ada.shape[0]
    alpha = (2 * depth) ** 0.25
    caches = (cache_kv_w128, cache_kv_w512, cache_kv_w2048)
    for (window, dil), cache in zip(ATTN_GROUPS, caches):
        assert window // dil == BAND and cache.shape[2] == window
    assert x_prompt.shape[1] % (BAND * ATTN_GROUPS[-1][1]) == 0 and x_sample.shape[1] == SUBLANES
    views = tuple(c.reshape(c.shape[0], c.shape[1], c.shape[2] * c.shape[3] * c.shape[4], c.shape[5])
                  for c in caches)
    nb = c_prompt.shape[0]
    yp, ys = x_prompt, x_sample
    p_bufs, p_states, s_bufs, s_states = [], [], [], []
    for l in range(depth):
        w = dict(w_in=w_in[l].astype(BF16), lb_param=lb_param, hg_norm_w=hg_norm_w[l],
                 w_branch_a=w_branch_a[l].astype(BF16), w_branch_b=w_branch_b[l].astype(BF16),
                 w_out=w_out[l].astype(BF16), ln1_g=ln1_g[l], ln1_b=ln1_b[l],
                 w_up=w_up[l].astype(BF16), b_up=b_up[l], w_down=w_down[l].astype(BF16), b_down=b_down[l],
                 ln2_g=ln2_g[l], ln2_b=ln2_b[l])
        mod = _modulation(jnp.concatenate([c_prompt, c_sample], axis=0), w_ada[l], b_ada[l])
        yp, bufs_p, st_p, shifted = _prompt_layer(yp, mod[:nb], views, x_sample.shape[1], l, w, alpha)
        ys, bufs_s, st_s = _sample_layer(ys, mod[nb:], views, shifted, state_hgrn, l, w, alpha)
        p_bufs.append(bufs_p)
        p_states.append(st_p)
        s_bufs.append([buf.reshape(c.shape[1:]) for buf, c in zip(bufs_s, caches)])
        s_states.append(st_s)
    group = lambda bufs, gi: _stack_layers([bl[gi] for bl in bufs])
    return (yp, ys, group(p_bufs, 0), group(p_bufs, 1), group(p_bufs, 2), _stack_layers(p_states),
            group(s_bufs, 0), group(s_bufs, 1), group(s_bufs, 2), _stack_layers(s_states))
```

```python
import functools
import math
from typing import NamedTuple

import jax
import jax.numpy as jnp
from jax import lax
from jax.experimental import pallas as pl
from jax.experimental.pallas import tpu as pltpu

F32 = jnp.float32
BF16 = jnp.bfloat16

ATTN_GROUPS = ((128, 1), (512, 4), (2048, 16))
N_GROUPS = len(ATTN_GROUPS)
HEADS_PER_GROUP = 4
HEAD_DIM = 128
N_ATTN_HEADS = N_GROUPS * HEADS_PER_GROUP
ATTN_W = N_ATTN_HEADS * HEAD_DIM
GROUP_W = HEADS_PER_GROUP * HEAD_DIM
BAND = 128
HG_DK = 128
HG_DV = 128
LN_EPS = 1e-5
RMS_EPS = 1e-6
NEG = -1e30

SUBLANES = 8
LANES = 128
VMEM_LIMIT_BYTES = 56 * 1024 * 1024
COL_CHUNK = 512

NT_DIMS = (((1,), (1,)), ((), ()))
TN_DIMS = (((0,), (0,)), ((), ()))


def _alibi_slope(group, head):
    return 2.0 ** (-8.0 * (group * HEADS_PER_GROUP + head + 1) / N_ATTN_HEADS)


def _cparams(*sem):
    return pltpu.CompilerParams(dimension_semantics=sem, vmem_limit_bytes=VMEM_LIMIT_BYTES)


def _resident(shape):
    nd = len(shape)
    return pl.BlockSpec(shape, lambda *_: (0,) * nd, pipeline_mode=pl.Buffered(1))


def _sigmoid(x):
    return 1.0 / (1.0 + jnp.exp(-x))


def _layer_norm(h, g, b):
    mu = jnp.mean(h, axis=-1, keepdims=True)
    hc = h - mu
    var = jnp.mean(hc * hc, axis=-1, keepdims=True)
    return hc * lax.rsqrt(var + LN_EPS) * g + b


class _Shift(NamedTuple):
    layer: int
    b0: int
    nbatch: int
    cb: int
    rsplit: int
    k: int
    drop: int
    keep: int

    @property
    def n_chunks(self):
        return self.nbatch // self.cb * self.rsplit // self.k


def _plan_shift(layer, b0, nbatch, n_steps, drop, keep):
    if nbatch >= n_steps:
        cb = -(-nbatch // n_steps)
        while nbatch % cb:
            cb += 1
        return _Shift(layer, b0, nbatch, cb, 1, 1, drop, keep)
    best = None
    for r in (1, 2, 4, 8):
        if keep % (r * SUBLANES):
            continue
        k = -(-nbatch * r // n_steps)
        if nbatch * r % k == 0 and (best is None or k * best[0] < best[1] * r):
            best = (r, k)
    return _Shift(layer, b0, nbatch, 1, best[0], best[1], drop, keep)


def _shift_copies(shift, src_ref, dst_ref, buf, in_sem, out_sem, chunk, fetch):
    rows = shift.keep // shift.rsplit
    slot = chunk % 2
    copies = []
    for j in range(shift.k):
        part = chunk * shift.k + j
        b = shift.b0 + (part // shift.rsplit) * shift.cb
        r = (part % shift.rsplit) * rows
        if fetch:
            copies.append(pltpu.make_async_copy(
                src_ref.at[shift.layer, pl.ds(b, shift.cb), pl.ds(shift.drop + r, rows), :],
                buf.at[slot, j], in_sem.at[slot, j]))
        else:
            copies.append(pltpu.make_async_copy(
                buf.at[slot, j], dst_ref.at[pl.ds(b, shift.cb), pl.ds(r, rows), :], out_sem.at[slot, j]))
    return copies


def _host_shift(kernel_fn, shift, n_in, n_out, aliased, grid):
    n_steps = math.prod(grid)
    n_chunks = shift.n_chunks
    assert n_chunks <= n_steps

    def hosted(*refs):
        src = refs[n_in]
        pos = n_in + (2 if aliased else 1)
        outs = refs[pos:pos + n_out]
        dst = refs[pos + n_out]
        scratch = refs[pos + n_out + 1:-3]
        buf, in_sem, out_sem = refs[-3:]
        copies = functools.partial(_shift_copies, shift, src, dst, buf, in_sem, out_sem)

        def start(chunk, fetch):
            for copy in copies(chunk, fetch):
                copy.start()

        def wait(chunk, fetch):
            for copy in copies(chunk, fetch):
                copy.wait()

        def advance(s):
            if isinstance(s, int):
                if 1 <= s <= n_chunks:
                    wait(s - 1, True)
                    start(s - 1, False)
                if 2 <= s <= n_chunks + 1:
                    wait(s - 2, False)
                if s < n_chunks:
                    start(s, True)
            else:
                wait(s - 1, True)
                start(s - 1, False)
                wait(s - 2, False)
                start(s, True)

        step = 0
        for axis in range(len(grid)):
            step = step * grid[axis] + pl.program_id(axis)

        pl.when((step >= 2) & (step < n_chunks))(lambda: advance(step))
        for s in sorted({0, 1, n_chunks, n_chunks + 1}):
            if s < n_steps and not 2 <= s < n_chunks:
                pl.when(step == s)(functools.partial(advance, s))

        kernel_fn(*refs[:n_in], *outs, *scratch)

        @pl.when(step == n_steps - 1)
        def _():
            for s in range(n_steps, n_chunks + 2):
                advance(s)

    return hosted


def _hosting_call(kernel_fn, shift, cache, dst, *, out_shape, grid, in_specs, out_specs, args,
                  scratch_shapes=(), **kwargs):
    if shift is None:
        res = pl.pallas_call(kernel_fn, out_shape=tuple(out_shape), grid=grid, in_specs=list(in_specs),
                             out_specs=tuple(out_specs), scratch_shapes=list(scratch_shapes), **kwargs)(*args)
        return tuple(res), dst
    n_in, n_out = len(in_specs), len(out_shape)
    aliased = dst is not None
    anywhere = pl.BlockSpec(memory_space=pl.ANY)
    res = pl.pallas_call(
        _host_shift(kernel_fn, shift, n_in, n_out, aliased, grid),
        out_shape=tuple(out_shape) + (jax.ShapeDtypeStruct(cache.shape[1:], cache.dtype),),
        grid=grid,
        in_specs=list(in_specs) + [anywhere] * (2 if aliased else 1),
        out_specs=tuple(out_specs) + (anywhere,),
        scratch_shapes=list(scratch_shapes) + [
            pltpu.VMEM((2, shift.k, shift.cb, shift.keep // shift.rsplit, cache.shape[3]), cache.dtype),
            pltpu.SemaphoreType.DMA((2, shift.k)), pltpu.SemaphoreType.DMA((2, shift.k))],
        input_output_aliases={n_in + 1: n_out} if aliased else {},
        **kwargs,
    )(*args, cache, *([dst] if aliased else []))
    return tuple(res[:-1]), res[-1]


def _mod_kernel(c_ref, w_ref, b_ref, o_ref):
    c = c_ref[...]
    a = (c * _sigmoid(c)).astype(BF16)
    o_ref[...] = jnp.dot(a, w_ref[...].astype(BF16), preferred_element_type=F32) + b_ref[...]


def _modulation(c, w_ada, b_ada):
    n, d = c.shape
    nout = w_ada.shape[1]
    tn = 1024
    return pl.pallas_call(
        _mod_kernel,
        out_shape=jax.ShapeDtypeStruct((n, nout), F32),
        grid=(nout // tn,),
        in_specs=[pl.BlockSpec((n, d), lambda j: (0, 0)),
                  pl.BlockSpec((d, tn), lambda j: (0, j)),
                  pl.BlockSpec((1, tn), lambda j: (0, j))],
        out_specs=pl.BlockSpec((n, tn), lambda j: (0, j)),
        compiler_params=_cparams("arbitrary"),
        name="modulation",
    )(c, w_ada, b_ada.reshape(1, nout))


def _store_heads(ref, c0, y):
    for j in range(y.shape[1] // HG_DK):
        ref[c0 // HG_DK + j] = y[:, j * HG_DK:(j + 1) * HG_DK].astype(ref.dtype)


def _stage_kv(scr, kv, y):
    for h in range(HEADS_PER_GROUP):
        slot = (kv * HEADS_PER_GROUP + h) * SUBLANES
        for j in range(y.shape[0] // SUBLANES):
            scr[j * 64 + slot:j * 64 + slot + SUBLANES, :] = y[j * SUBLANES:(j + 1) * SUBLANES,
                                                               h * HEAD_DIM:(h + 1) * HEAD_DIM]


def _emit_kv(out_ref, scr, gb, rb, n):
    per_seq = n // SUBLANES
    for s in range(gb):
        for jj in range(per_seq):
            src = (s * (rb // SUBLANES) + (rb - n) // SUBLANES + jj) * 64
            for i in range(SUBLANES):
                out_ref[s, jj * 64 + i * SUBLANES:jj * 64 + (i + 1) * SUBLANES, :] = (
                    scr[pl.ds(src + i, SUBLANES, stride=SUBLANES), :])


def _inproj_kernel(x_ref, mod_ref, w_ref, lbp_ref, attn_ref, hq_ref, hk_ref, hv_ref, logf_ref, og_ref,
                   gates_ref, *refs, layer, kv_rows):
    kv_refs, kv_scr = refs[:N_GROUPS], refs[N_GROUPS]
    gb, rb, d = x_ref.shape
    tm = gb * rb
    hg_w = hq_ref.shape[0] * HG_DK
    act = attn_ref.dtype
    sh = mod_ref[:, :, 0:d]
    sc = mod_ref[:, :, d:2 * d]
    u = (x_ref[...] * (1.0 + sc) + sh).reshape(tm, d).astype(BF16)

    def proj(c0):
        return jnp.dot(u, w_ref[:, c0:c0 + COL_CHUNK], preferred_element_type=F32)

    assert COL_CHUNK == GROUP_W
    order = list(range(N_GROUPS)) + [s * N_GROUPS + g for g in range(N_GROUPS) for s in (1, 2)]
    for chunk in order:
        c0 = chunk * COL_CHUNK
        y = proj(c0)
        attn_ref[:, c0:c0 + COL_CHUNK] = y.astype(act)
        section, g = divmod(chunk, N_GROUPS)
        if section:
            _stage_kv(kv_scr, section - 1, y)
        if section == 2:
            _emit_kv(kv_refs[g], kv_scr, gb, rb, min(kv_rows[g], rb))
    base = 3 * ATTN_W
    for c0 in range(0, hg_w, COL_CHUNK):
        y = proj(base + c0)
        _store_heads(hq_ref, c0, y * _sigmoid(y))
    lbp = lbp_ref[...]
    e = jnp.exp(lbp - jnp.max(lbp, axis=0, keepdims=True))
    lb = jnp.sum(e[0:layer + 1], axis=0, keepdims=True) / jnp.sum(e, axis=0, keepdims=True)
    base += hg_w
    for c0 in range(0, hg_w, COL_CHUNK):
        lbc = lb[:, c0:c0 + COL_CHUNK]
        f = lbc + (1.0 - lbc) * _sigmoid(proj(base + c0))
        _store_heads(logf_ref, c0, jnp.log(f))
        _store_heads(hk_ref, c0, 1.0 - f)
    base += hg_w
    for c0 in range(0, hg_w, COL_CHUNK):
        _store_heads(hv_ref, c0, proj(base + c0))
    base += hg_w
    for c0 in range(0, hg_w, COL_CHUNK):
        y = proj(base + c0)
        _store_heads(og_ref, c0, y * _sigmoid(y))
    base += hg_w
    for c0 in range(0, 2 * d, COL_CHUNK):
        gates_ref[:, c0:c0 + COL_CHUNK] = _sigmoid(proj(base + c0)).astype(act)


def _in_projection(x3, mod3, w_in, lb_param, layer, gb, rb, act, hosted=(None, None, None)):
    g, r, d = x3.shape
    n = g * r
    tm = gb * rb
    hg_w = lb_param.shape[1]
    rt = r // rb
    heads = hg_w // HG_DK
    rows = lambda width: pl.BlockSpec((tm, width), lambda gi, ri: (gi * rt + ri, 0))
    by_head = pl.BlockSpec((heads, tm, HG_DK), lambda gi, ri: (0, gi * rt + ri, 0))
    head_major = lambda dtype: jax.ShapeDtypeStruct((heads, n, HG_DK), dtype)
    out_shapes = (
        jax.ShapeDtypeStruct((n, 3 * ATTN_W), act),
        head_major(act),
        head_major(act),
        head_major(act),
        head_major(F32),
        head_major(act),
        jax.ShapeDtypeStruct((n, 2 * d), act),
    )
    rp = 2 * HEADS_PER_GROUP
    kv_rows = tuple(min(window, r) for window, dil in ATTN_GROUPS)
    kv_specs = []
    for kept in kv_rows:
        assert (r - max(kept, rb)) % rb == 0
        first_tile = (r - max(kept, rb)) // rb
        kv_specs.append(pl.BlockSpec((gb, min(kept, rb) * rp, HEAD_DIM),
                                     lambda gi, ri, first_tile=first_tile: (gi, jnp.maximum(ri - first_tile, 0), 0)))
    out_shapes += tuple(jax.ShapeDtypeStruct((g, kept * rp, HEAD_DIM), F32) for kept in kv_rows)
    return _hosting_call(
        functools.partial(_inproj_kernel, layer=layer, kv_rows=kv_rows), *hosted,
        out_shape=out_shapes,
        grid=(g // gb, rt),
        in_specs=[pl.BlockSpec((gb, rb, d), lambda gi, ri: (gi, ri, 0)),
                  pl.BlockSpec((gb, 1, mod3.shape[2]), lambda gi, ri: (gi, 0, 0)),
                  _resident(w_in.shape),
                  _resident(lb_param.shape)],
        out_specs=(rows(3 * ATTN_W), by_head, by_head, by_head, by_head, by_head, rows(2 * d), *kv_specs),
        scratch_shapes=[pltpu.VMEM((tm * rp, HEAD_DIM), F32)],
        args=(x3, mod3, w_in, lb_param),
        compiler_params=_cparams("arbitrary", "arbitrary"),
        name="in_projection",
    )


ATTN_BLOCKS_PER_ITER = 8


def _attn_prompt_kernel(q_ref, k_ref, v_ref, attn_ref, qf, kf, vf, mf, lf, af, *, head_axis, group_axis):
    s = attn_ref.shape[0]
    nblk = s // BAND
    head = pl.program_id(head_axis)
    group = pl.program_id(group_axis)
    qi = lax.broadcasted_iota(jnp.int32, (BAND, BAND), 0)
    kj = lax.broadcasted_iota(jnp.int32, (BAND, BAND), 1)
    steps_own = qi - kj
    steps_prev = steps_own + BAND
    scale = HEAD_DIM ** -0.5
    ones = jnp.ones((BAND, HEAD_DIM), BF16)
    qf[...] = q_ref[...].astype(F32)
    kf[...] = k_ref[...].astype(F32)
    vf[...] = v_ref[...].astype(F32)
    for g, (window, dil) in enumerate(ATTN_GROUPS):
        nb = nblk // dil
        slope = jnp.float32(_alibi_slope(g, HEADS_PER_GROUP - 1) * dil)
        for h in range(HEADS_PER_GROUP - 1):
            slope = jnp.where(head == h, jnp.float32(_alibi_slope(g, h) * dil), slope)
        bias_own = jnp.where(steps_own >= 0, -slope * steps_own.astype(F32), NEG)
        bias_prev = jnp.where(steps_prev <= BAND, -slope * steps_prev.astype(F32), NEG)

        def body(it, carry, g=g, dil=dil, nb=nb, bias_own=bias_own, bias_prev=bias_prev):
            span = min(nb, ATTN_BLOCKS_PER_ITER)
            blocks = []
            for u in range(ATTN_BLOCKS_PER_ITER):
                idx = it * ATTN_BLOCKS_PER_ITER + u
                r = idx // nb
                i = idx - r * nb
                start = r + i * (BAND * dil)
                rows = pl.ds(start, BAND, stride=dil) if dil > 1 else pl.ds(start, BAND)
                if u % span:
                    blocks.append((rows, "chain", None, True))
                elif span == nb:
                    blocks.append((rows, "none", None, False))
                else:
                    start_prev = r + jnp.maximum(i - 1, 0) * (BAND * dil)
                    rows_prev = pl.ds(start_prev, BAND, stride=dil) if dil > 1 else pl.ds(start_prev, BAND)
                    blocks.append((rows, "load", rows_prev, i > 0))
            pair = lambda y, z: lax.dot_general(y, z, NT_DIMS, preferred_element_type=F32)
            scores, keys = [], []
            for rows, how, rows_prev, has_prev in blocks:
                q = qf[rows, :].astype(BF16)
                keys.append(kf[rows, :].astype(BF16))
                if how == "none":
                    scores.append((pair(q, keys[-1]), None))
                else:
                    k_prev = keys[-2] if how == "chain" else kf[rows_prev, :].astype(BF16)
                    scores.append((pair(q, keys[-1]), pair(q, k_prev)))
            probs = []
            for (rows, how, rows_prev, has_prev), (s_own, s_prev) in zip(blocks, scores):
                s_own = s_own * scale + bias_own
                if s_prev is None:
                    m = jnp.max(s_own, axis=-1, keepdims=True)
                    probs.append((m, jnp.exp(s_own - m).astype(BF16), None))
                else:
                    s_prev = s_prev * scale + bias_prev
                    if how == "load":
                        s_prev = jnp.where(has_prev, s_prev, NEG)
                    m = jnp.max(jnp.maximum(s_own, s_prev), axis=-1, keepdims=True)
                    probs.append((m, jnp.exp(s_own - m).astype(BF16), jnp.exp(s_prev - m).astype(BF16)))
            results, values = [], []
            for (rows, how, rows_prev, has_prev), (m, p_own, p_prev) in zip(blocks, probs):
                values.append(jnp.concatenate([vf[rows, :].astype(BF16), ones], axis=1))
                both = jnp.dot(p_own, values[-1], preferred_element_type=F32)
                if p_prev is not None:
                    v_prev = values[-2] if how == "chain" else jnp.concatenate(
                        [vf[rows_prev, :].astype(BF16), ones], axis=1)
                    both = both + jnp.dot(p_prev, v_prev, preferred_element_type=F32)
                acc, l = both[:, :HEAD_DIM], both[:, HEAD_DIM:]
                if g < N_GROUPS - 1:
                    m_old = mf[rows, :]
                    m_new = jnp.maximum(m_old, m)
                    a_old = jnp.exp(m_old - m_new)
                    a_blk = jnp.exp(m - m_new)
                    l = a_old * lf[rows, :] + a_blk * l
                    acc = a_old * af[rows, :] + a_blk * acc
                    m = m_new
                results.append((rows, m, l, acc))
            for rows, m, l, acc in results:
                mf[rows, :] = jnp.broadcast_to(m, (BAND, HEAD_DIM))
                lf[rows, :] = l
                af[rows, :] = acc
            return carry

        @pl.when(group == N_GROUPS - 1 - g)
        def _(body=body):
            lax.fori_loop(0, nblk // ATTN_BLOCKS_PER_ITER, body, 0)

    @pl.when(group == N_GROUPS - 1)
    def _():
        attn_ref[...] = (af[...] / lf[...]).astype(attn_ref.dtype)


def _attention_prompt(qkv, b, s, hosted=(None, None, None)):
    hpq = ATTN_W // HEAD_DIM
    in_specs = [pl.BlockSpec((s, HEAD_DIM), lambda bi, h, g, section=section:
                             (bi, section * hpq + (N_GROUPS - 1 - g) * HEADS_PER_GROUP + h))
                for section in range(3)]
    (attn,), dst = _hosting_call(
        functools.partial(_attn_prompt_kernel, head_axis=1, group_axis=2), *hosted,
        out_shape=(jax.ShapeDtypeStruct((b * s, GROUP_W), BF16),),
        grid=(b, HEADS_PER_GROUP, N_GROUPS),
        in_specs=in_specs,
        out_specs=(pl.BlockSpec((s, HEAD_DIM), lambda bi, h, g: (bi, h)),),
        scratch_shapes=[pltpu.VMEM((s, HEAD_DIM), F32) for _ in range(6)],
        args=[qkv] * 3,
        compiler_params=_cparams("arbitrary", "arbitrary", "arbitrary"),
        name="attn_prompt",
    )
    return attn, dst


def _attn_sample_kernel(*refs, group, dil, n_steps, length, seqs):
    first = group == 0
    last = group == N_GROUPS - 1
    q_ref, newkv_ref, cache_ref = refs[:3]
    pos = 3
    if not first:
        m_in, l_in, a_in = refs[pos:pos + 3]
        pos += 3
    pos += 1
    newc_ref = refs[pos]
    outs = refs[pos + 1:]
    t = q_ref.shape[0] // seqs
    rp = 2 * HEADS_PER_GROUP
    stretch = max(dil, t)
    nkeys = length // stretch * t
    j_c = lax.broadcasted_iota(jnp.int32, (t, nkeys), 0)
    c_c = lax.broadcasted_iota(jnp.int32, (t, nkeys), 1)
    dist_c = length + j_c - ((c_c // t) * stretch + (c_c & (t - 1)))
    ok_c = ((dist_c & (dil - 1)) == 0) & (dist_c <= n_steps * dil)
    j_n = lax.broadcasted_iota(jnp.int32, (t, t), 0)
    r_n = lax.broadcasted_iota(jnp.int32, (t, t), 1)
    dist_n = j_n - r_n
    ok_n = (dist_n >= 0) & ((dist_n & (dil - 1)) == 0) & (dist_n <= n_steps * dil)
    scale = HEAD_DIM ** -0.5
    qoff = group * GROUP_W
    cache_rows = lambda s, first_row: cache_ref[s, :, pl.ds(first_row, t, stride=rp), :].reshape(
        nkeys, HEAD_DIM).astype(BF16)
    new_rows = lambda s, first_row: newkv_ref[s, pl.ds(first_row, t, stride=rp), :].astype(BF16)
    chains = [(s, h) for s in range(seqs) for h in range(HEADS_PER_GROUP)]
    scores = []
    for s, h in chains:
        q = q_ref[s * t:(s + 1) * t, qoff + h * HEAD_DIM:qoff + (h + 1) * HEAD_DIM].astype(BF16)
        scores.append((lax.dot_general(q, cache_rows(s, h), NT_DIMS, preferred_element_type=F32),
                       lax.dot_general(q, new_rows(s, h), NT_DIMS, preferred_element_type=F32)))
    probs = []
    for (s, h), (s_c, s_n) in zip(chains, scores):
        rows, sl = slice(s * t, (s + 1) * t), slice(h * HEAD_DIM, (h + 1) * HEAD_DIM)
        slope = _alibi_slope(group, h)
        s_c = jnp.where(ok_c, s_c * scale - slope * dist_c.astype(F32), NEG)
        s_n = jnp.where(ok_n, s_n * scale - slope * dist_n.astype(F32), NEG)
        m = jnp.maximum(jnp.max(s_c, axis=-1, keepdims=True), jnp.max(s_n, axis=-1, keepdims=True))
        if not first:
            m = jnp.maximum(m, m_in[rows, sl][:, 0:1])
        p_c = jnp.exp(s_c - m)
        p_n = jnp.exp(s_n - m)
        l = jnp.sum(p_c, axis=-1, keepdims=True) + jnp.sum(p_n, axis=-1, keepdims=True)
        probs.append((m, l, p_c.astype(BF16), p_n.astype(BF16)))
    for (s, h), (m, l, p_c, p_n) in zip(chains, probs):
        rows, sl = slice(s * t, (s + 1) * t), slice(h * HEAD_DIM, (h + 1) * HEAD_DIM)
        acc = (jnp.dot(p_c, cache_rows(s, HEADS_PER_GROUP + h), preferred_element_type=F32)
               + jnp.dot(p_n, new_rows(s, HEADS_PER_GROUP + h), preferred_element_type=F32))
        if not first:
            a_old = jnp.exp(m_in[rows, sl] - m)
            l = a_old * l_in[rows, sl] + l
            acc = a_old * a_in[rows, sl] + acc
        if last:
            outs[0][rows, sl] = acc / l
        else:
            outs[0][rows, sl] = jnp.broadcast_to(m, (t, HEAD_DIM))
            outs[1][rows, sl] = jnp.broadcast_to(l, (t, HEAD_DIM))
            outs[2][rows, sl] = acc
    for s in range(seqs):
        newc_ref[s] = newkv_ref[s]


SAMPLE_SEQS_PER_STEP = (8, 4, 2)


def _attention_sample(qkv, new_kv, cache, shifted, layer, running, t, group):
    window, dil = ATTN_GROUPS[group]
    depth, bd, crow, dh = cache.shape
    rp = 2 * HEADS_PER_GROUP
    length = crow // rp
    stretch = max(dil, t)
    seqs = math.gcd(SAMPLE_SEQS_PER_STEP[group], bd)
    first = group == 0
    last = group == N_GROUPS - 1
    rows = pl.BlockSpec((seqs * t, GROUP_W), lambda bi: (bi, 0))
    stat = jax.ShapeDtypeStruct((bd * t, GROUP_W), F32)
    new_blk = pl.BlockSpec((seqs, t * rp, dh), lambda bi: (bi, 0, 0))
    in_specs = [pl.BlockSpec((seqs * t, 3 * ATTN_W), lambda bi: (bi, 0)),
                new_blk,
                pl.BlockSpec((None, seqs, length // stretch, t * rp, dh), lambda bi: (layer, bi, 0, 0, 0))]
    args = [qkv, new_kv, cache.reshape(depth, bd, length // stretch, stretch * rp, dh)]
    if not first:
        in_specs += [rows, rows, rows]
        args += list(running)
    in_specs.append(pl.BlockSpec(memory_space=pl.ANY))
    args.append(shifted)
    n_out = 1 if last else 3
    res = pl.pallas_call(
        functools.partial(_attn_sample_kernel, group=group, dil=dil, n_steps=window // dil, length=length,
                          seqs=seqs),
        out_shape=(jax.ShapeDtypeStruct(shifted.shape, shifted.dtype),) + (stat,) * n_out,
        grid=(bd // seqs,),
        in_specs=in_specs,
        out_specs=(pl.BlockSpec((seqs, t * rp, dh), lambda bi: (bi, length // t - 1, 0)),) + (rows,) * n_out,
        input_output_aliases={len(args) - 1: 0},
        compiler_params=_cparams("arbitrary"),
        name=f"attn_sample_g{group}",
    )(*args)
    return res[0], (res[1] if last else tuple(res[1:]))


def _cumsum_rows(x, row):
    d = 1
    while d < x.shape[0]:
        x = x + jnp.where(row >= d, pltpu.roll(x, d, 0), 0.0)
        d *= 2
    return x


def _hgrn_carry(q, k, v, g, st):
    c = q.shape[0]
    qg = (q * jnp.exp(g)).astype(BF16)
    o = lax.dot_general(qg, st.astype(BF16), NT_DIMS, preferred_element_type=F32)
    g_last = g[c - 1:c, :]
    kd = (k * jnp.exp(g_last - g)).astype(BF16)
    st_new = st * jnp.exp(g_last) + lax.dot_general(v.astype(BF16), kd, TN_DIMS, preferred_element_type=F32)
    return o, st_new


def _hgrn_chunks_pairwise(qs, ks, vs, lfs, states):
    c = qs[0].shape[0]
    row = lax.broadcasted_iota(jnp.int32, qs[0].shape, 0)
    gs = [_cumsum_rows(lf, row) for lf in lfs]
    outs = []
    for q, k, v, g in zip(qs, ks, vs, gs):
        o = jnp.sum(q * k, axis=-1, keepdims=True) * v
        for d in range(1, c):
            ok = row >= d
            e = jnp.exp(jnp.where(ok, g - pltpu.roll(g, d, 0), 0.0))
            a = jnp.sum(jnp.where(ok, q * pltpu.roll(k, d, 0) * e, 0.0), axis=-1, keepdims=True)
            o = o + a * pltpu.roll(v, d, 0)
        outs.append(o)
    cols = []
    for k, g in zip(ks, gs):
        g_last = g[c - 1:c, :]
        kd = k * jnp.exp(g_last - g)
        cols.append(jnp.concatenate([kd, jnp.broadcast_to(jnp.exp(g_last), kd.shape)], axis=0).T)
    carried = [jnp.dot((q * jnp.exp(g)).astype(BF16), s.astype(BF16), preferred_element_type=F32)
               for q, g, s in zip(qs, gs, states)]
    new_states = [s * col[:, c:c + 1] + jnp.dot(col[:, 0:c].astype(BF16), v.astype(BF16),
                                                preferred_element_type=F32)
                  for s, col, v in zip(states, cols, vs)]
    return [o + oc for o, oc in zip(outs, carried)], new_states


def _pair_level(c):
    ri = lax.broadcasted_iota(jnp.int32, (c, c), 0)
    ci = lax.broadcasted_iota(jnp.int32, (c, c), 1)
    x = ri ^ ci
    bits = jnp.zeros((c, c), jnp.int32)
    p = 1
    while p < c:
        bits = bits + (x >= p).astype(jnp.int32)
        p *= 2
    return jnp.where(ri > ci, bits + 1, jnp.where(ri == ci, 1, 0))


def _hgrn_chunks_blocked(qs, ks, vs, lfs, sts, g_refs, level):
    nh = len(qs)
    c = qs[0].shape[0]
    row = lax.broadcasted_iota(jnp.int32, (c, HG_DK), 0)
    sub = row & (SUBLANES - 1)
    sub8 = lax.broadcasted_iota(jnp.int32, (SUBLANES, HG_DK), 0)
    gs = []
    for lf, g_ref in zip(lfs, g_refs):
        x = lf
        for d in (1, 2, 4):
            x = x + jnp.where(sub >= d, pltpu.roll(x, d, 0), 0.0)
        g_ref[...] = x
        pieces = [x[0:SUBLANES]]
        off = None
        for j in range(1, c // SUBLANES):
            tot = jnp.broadcast_to(g_ref[j * SUBLANES - 1:j * SUBLANES, :], (SUBLANES, HG_DK))
            off = tot if off is None else off + tot
            pieces.append(x[j * SUBLANES:(j + 1) * SUBLANES] + off)
        g = jnp.concatenate(pieces, axis=0)
        g_ref[...] = g
        gs.append(g)

    def mid_rows(g_ref, n):
        half = n // 2
        bcast = lambda r: jnp.broadcast_to(g_ref[r:r + 1, :], (SUBLANES, HG_DK))
        out = []
        for j in range(c // SUBLANES):
            base = j * SUBLANES
            if n >= SUBLANES:
                out.append(bcast(base // n * n + half - 1))
            else:
                assert n == 4
                out.append(jnp.where(sub8 < 4, bcast(base + 1), bcast(base + 5)))
        return jnp.concatenate(out, axis=0)

    pair = lambda y, z: lax.dot_general(y, z, NT_DIMS, preferred_element_type=F32)
    prods = [[pair(q.astype(BF16), k.astype(BF16))] for q, k in zip(qs, ks)]
    n = 2
    while n <= c:
        second = (row & (n // 2)) != 0
        for u in range(nh):
            if n == 2:
                arg = jnp.where(second, lfs[u], 0.0)
            else:
                gmid = mid_rows(g_refs[u], n)
                arg = jnp.where(second, gs[u] - gmid, gmid - gs[u])
            y = (jnp.where(second, qs[u], ks[u]) * jnp.exp(arg)).astype(BF16)
            prods[u].append(pair(y, y))
        n *= 2
    outs = []
    for u in range(nh):
        amat = jnp.where(level == 1, prods[u][0], 0.0)
        for j, p in enumerate(prods[u][1:]):
            amat = jnp.where(level == j + 2, p, amat)
        outs.append(jnp.dot(amat.astype(BF16), vs[u].astype(BF16), preferred_element_type=F32))
    states = []
    for u in range(nh):
        o_carry, st_new = _hgrn_carry(qs[u], ks[u], vs[u], gs[u], sts[u])
        outs[u] = outs[u] + o_carry
        states.append(st_new)
    return outs, states


def _hgrn_output(o, og, nw):
    return o * lax.rsqrt(jnp.mean(o * o, axis=-1, keepdims=True) + RMS_EPS) * nw * og


HGRN_HEADS_PER_ITER = 4


HGRN_CHUNKS_PER_STEP = 4


def _hgrn_prompt_kernel(q_ref, k_ref, v_ref, lf_ref, og_ref, nw_ref, hg_ref, sfin_ref, st_ref, g_ref):
    ci = pl.program_id(1)
    heads = q_ref.shape[0]
    c = g_ref.shape[1]
    head_groups = heads // HGRN_HEADS_PER_ITER

    @pl.when(ci == 0)
    def _():
        st_ref[...] = jnp.zeros_like(st_ref)

    level = _pair_level(c)

    def head_group(it, carry):
        i = it % head_groups
        rows = pl.ds(pl.multiple_of(it // head_groups * c, c), c)
        hs = [i * HGRN_HEADS_PER_ITER + u for u in range(HGRN_HEADS_PER_ITER)]
        load = lambda ref: [ref[h, rows, :].astype(F32) for h in hs]
        outs, states = _hgrn_chunks_blocked(load(q_ref), load(k_ref), load(v_ref), load(lf_ref),
                                            [st_ref[h] for h in hs],
                                            [g_ref.at[u] for u in range(HGRN_HEADS_PER_ITER)], level)
        for h, o, st_new in zip(hs, outs, states):
            st_ref[h] = st_new
            hg_ref[h, rows, :] = _hgrn_output(o, og_ref[h, rows, :].astype(F32), nw_ref[...]).astype(hg_ref.dtype)
        return carry

    lax.fori_loop(0, q_ref.shape[1] // c * head_groups, head_group, 0)

    @pl.when(ci == pl.num_programs(1) - 1)
    def _():
        for h in range(heads):
            sfin_ref[0, h] = st_ref[h].T


def _hgrn_prompt(hq, hk, hv, logf, og, norm_w, b, s, chunk, hosted=(None, None, None)):
    heads, n, _ = hq.shape
    rows = chunk * math.gcd(HGRN_CHUNKS_PER_STEP, s // chunk)
    nc = s // rows
    blk = pl.BlockSpec((heads, rows, HG_DK), lambda bi, ci: (0, bi * nc + ci, 0))
    (hg, s_fin), dst = _hosting_call(
        _hgrn_prompt_kernel, *hosted,
        out_shape=(jax.ShapeDtypeStruct((heads, n, HG_DV), BF16),
                   jax.ShapeDtypeStruct((b, heads, HG_DK, HG_DV), F32)),
        grid=(b, nc),
        in_specs=[blk, blk, blk, blk, blk, pl.BlockSpec((1, HG_DV), lambda bi, ci: (0, 0))],
        out_specs=(blk, pl.BlockSpec((1, heads, HG_DK, HG_DV), lambda bi, ci: (bi, 0, 0, 0))),
        scratch_shapes=[pltpu.VMEM((heads, HG_DV, HG_DK), F32),
                        pltpu.VMEM((HGRN_HEADS_PER_ITER, chunk, HG_DK), F32)],
        args=(hq, hk, hv, logf, og, norm_w.reshape(1, HG_DV)),
        compiler_params=_cparams("arbitrary", "arbitrary"),
        name="hgrn_prompt",
    )
    return hg, s_fin, dst


HGRN_SAMPLE_SEQS_PER_STEP = 2


def _hgrn_sample_kernel(q_ref, k_ref, v_ref, lf_ref, og_ref, nw_ref, s_ref, hg_ref, snew_ref):
    seqs, heads = s_ref.shape[0], s_ref.shape[1]
    t = q_ref.shape[1] // seqs
    chains = [(s, h) for s in range(seqs) for h in range(heads)]
    load = lambda ref: [ref[h, s * t:(s + 1) * t, :] for s, h in chains]
    outs, new_states = _hgrn_chunks_pairwise(load(q_ref), load(k_ref), load(v_ref), load(lf_ref),
                                             [s_ref[s, h] for s, h in chains])
    for (s, h), o, st_new in zip(chains, outs, new_states):
        snew_ref[s, h] = st_new
        hg_ref[h, s * t:(s + 1) * t, :] = _hgrn_output(o, og_ref[h, s * t:(s + 1) * t, :], nw_ref[...])


def _hgrn_sample(hq, hk, hv, logf, og, norm_w, state, t):
    bd, heads = state.shape[0], state.shape[1]
    seqs = math.gcd(HGRN_SAMPLE_SEQS_PER_STEP, bd)
    blk = pl.BlockSpec((heads, seqs * t, HG_DK), lambda bi: (0, bi, 0))
    sblk = pl.BlockSpec((seqs, heads, HG_DK, HG_DV), lambda bi: (bi, 0, 0, 0))
    return pl.pallas_call(
        _hgrn_sample_kernel,
        out_shape=(jax.ShapeDtypeStruct((heads, bd * t, HG_DV), F32),
                   jax.ShapeDtypeStruct(state.shape, state.dtype)),
        grid=(bd // seqs,),
        in_specs=[blk, blk, blk, blk, blk, pl.BlockSpec((1, HG_DV), lambda bi: (0, 0)), sblk],
        out_specs=(blk, sblk),
        compiler_params=_cparams("arbitrary"),
        name="hgrn_sample",
    )(hq, hk, hv, logf, og, norm_w.reshape(1, HG_DV), state)


def _merge_kernel(x_ref, mod_ref, attn_ref, hg_ref, gates_ref, wa_ref, wb_ref, wo_ref, lng_ref, lnb_ref,
                  x1_ref, *, alpha):
    gb, rb, d = x_ref.shape
    tm = gb * rb
    hg = jnp.concatenate([hg_ref[h].astype(BF16) for h in range(hg_ref.shape[0])], axis=1)
    branch_a = jnp.dot(attn_ref[...].astype(BF16), wa_ref[...], preferred_element_type=F32)
    branch_b = jnp.dot(hg, wb_ref[...], preferred_element_type=F32)
    gate_a = gates_ref[:, 0:d].astype(F32)
    gate_b = gates_ref[:, d:2 * d].astype(F32)
    merged = gate_a * branch_a + gate_b * branch_b
    mix = jnp.dot(merged.astype(BF16), wo_ref[...], preferred_element_type=F32)
    g1 = mod_ref[:, :, 2 * d:3 * d]
    h = alpha * x_ref[...] + g1 * mix.reshape(gb, rb, d)
    x1_ref[...] = _layer_norm(h, lng_ref[...], lnb_ref[...])


def _merge(x3, mod3, attn, hg, gates, wa, wb, wo, ln_g, ln_b, gb, rb, alpha, hosted=(None, None, None)):
    g, r, d = x3.shape
    tm = gb * rb
    rt = r // rb
    row = lambda gi, ri: (gi * rt + ri, 0)
    rows = lambda a: pl.BlockSpec((tm, a.shape[1]), row)
    tile3 = pl.BlockSpec((gb, rb, d), lambda gi, ri: (gi, ri, 0))
    (x1,), dst = _hosting_call(
        functools.partial(_merge_kernel, alpha=alpha), *hosted,
        out_shape=(jax.ShapeDtypeStruct(x3.shape, F32),),
        grid=(g // gb, rt),
        in_specs=[tile3, pl.BlockSpec((gb, 1, mod3.shape[2]), lambda gi, ri: (gi, 0, 0)),
                  rows(attn), pl.BlockSpec((hg.shape[0], tm, hg.shape[2]), lambda gi, ri: (0, gi * rt + ri, 0)),
                  rows(gates),
                  _resident(wa.shape), _resident(wb.shape), _resident(wo.shape),
                  _resident((1, d)), _resident((1, d))],
        out_specs=(tile3,),
        args=(x3, mod3, attn, hg, gates, wa, wb, wo, ln_g.reshape(1, d), ln_b.reshape(1, d)),
        compiler_params=_cparams("arbitrary", "arbitrary"),
        name="merge",
    )
    return x1, dst


def _mlp_kernel(x1_ref, mod_ref, wu_ref, bu_ref, wd_ref, bd_ref, lng_ref, lnb_ref, x2_ref, *, alpha):
    gb, rb, d = x1_ref.shape
    tm = gb * rb
    dff = wu_ref.shape[1]
    x1 = x1_ref[...]
    sh = mod_ref[:, :, 3 * d:4 * d]
    sc = mod_ref[:, :, 4 * d:5 * d]
    g2 = mod_ref[:, :, 5 * d:6 * d]
    u = (x1 * (1.0 + sc) + sh).reshape(tm, d).astype(BF16)
    hidden = []
    for c0 in range(0, dff, COL_CHUNK):
        hid = jnp.dot(u, wu_ref[:, c0:c0 + COL_CHUNK], preferred_element_type=F32) + bu_ref[:, c0:c0 + COL_CHUNK]
        hidden.append(jnp.square(jnp.maximum(hid, 0.0)).astype(BF16))
    ff = jnp.dot(jnp.concatenate(hidden, axis=1), wd_ref[...], preferred_element_type=F32) + bd_ref[...]
    h = alpha * x1 + g2 * ff.reshape(gb, rb, d)
    x2_ref[...] = _layer_norm(h, lng_ref[...], lnb_ref[...])


def _mlp(x3, mod3, wu, bu, wd, bd, ln_g, ln_b, gb, rb, alpha, hosted=(None, None, None)):
    g, r, d = x3.shape
    dff = wu.shape[1]
    tile3 = pl.BlockSpec((gb, rb, d), lambda gi, ri: (gi, ri, 0))
    (x2,), dst = _hosting_call(
        functools.partial(_mlp_kernel, alpha=alpha), *hosted,
        out_shape=(jax.ShapeDtypeStruct(x3.shape, F32),),
        grid=(g // gb, r // rb),
        in_specs=[tile3, pl.BlockSpec((gb, 1, mod3.shape[2]), lambda gi, ri: (gi, 0, 0)),
                  _resident(wu.shape), _resident((1, dff)), _resident(wd.shape), _resident((1, d)),
                  _resident((1, d)), _resident((1, d))],
        out_specs=(tile3,),
        args=(x3, mod3, wu, bu.reshape(1, dff), wd, bd.reshape(1, d), ln_g.reshape(1, d), ln_b.reshape(1, d)),
        compiler_params=_cparams("arbitrary", "arbitrary"),
        name="mlp",
    )
    return x2, dst


PROMPT_ROWS = 256
MERGE_ROWS = 512
SAMPLE_SEQS = 32
HGRN_CHUNK = 128


def _prompt_layer(x, mod, caches, t_new, layer, w, alpha):
    b, s, d = x.shape
    mod3 = mod.reshape(b, 1, mod.shape[1])
    gb, rb = 1, min(PROMPT_ROWS, s)
    chunk = min(HGRN_CHUNK, s)
    rp = 2 * HEADS_PER_GROUP
    bd = caches[0].shape[1]
    tile_steps = s // rb * b
    hgrn_steps = s // chunk // math.gcd(HGRN_CHUNKS_PER_STEP, s // chunk) * b

    def plan(gi, b0, nbatch, n_steps):
        rows = caches[gi].shape[2]
        return _plan_shift(layer, b0, nbatch, n_steps, t_new * rp, rows - t_new * rp)

    attn_steps = b * HEADS_PER_GROUP * N_GROUPS
    n_hgrn = bd // 4
    n_attn = (bd - n_hgrn) // 2
    big = None

    def share(b0, nbatch, n_steps):
        return (plan(2, b0, nbatch, n_steps), caches[2], big) if nbatch else (None, None, big)

    (qkv, hq, hk, hv, logf, og, gates, *kv_tails), mid = _in_projection(
        x, mod3, w["w_in"], w["lb_param"], layer, gb, rb, BF16,
        hosted=(plan(1, 0, bd, tile_steps), caches[1], None))
    hg, s_fin, big = _hgrn_prompt(hq, hk, hv, logf, og, w["hg_norm_w"], b, s, chunk,
                                  hosted=share(0, n_hgrn, hgrn_steps))
    attn, big = _attention_prompt(qkv, b, s, hosted=share(n_hgrn, n_attn, attn_steps))
    new_bufs = [kv.reshape(b, kv.shape[1] // rp, 2, HEADS_PER_GROUP, HEAD_DIM) for kv in kv_tails]
    rb_merge = min(MERGE_ROWS, s)
    x1, small = _merge(x, mod3, attn, hg, gates, w["w_branch_a"], w["w_branch_b"], w["w_out"],
                       w["ln1_g"], w["ln1_b"], gb, rb_merge, alpha,
                       hosted=(plan(0, 0, bd, s // rb_merge * b), caches[0], None))
    x2, big = _mlp(x1, mod3, w["w_up"], w["b_up"], w["w_down"], w["b_down"], w["ln2_g"], w["ln2_b"], gb, rb,
                   alpha, hosted=share(n_hgrn + n_attn, bd - n_hgrn - n_attn, tile_steps))
    return x2, new_bufs, s_fin, (small, mid, big)


def _sample_layer(x, mod, caches, shifted, states, layer, w, alpha):
    b, s, d = x.shape
    mod3 = mod.reshape(b, 1, mod.shape[1])
    gb, rb = min(SAMPLE_SEQS, b), s
    rp = 2 * HEADS_PER_GROUP
    (qkv, hq, hk, hv, logf, og, gates, *new_kv), _ = _in_projection(
        x, mod3, w["w_in"], w["lb_param"], layer, gb, rb, F32)
    new_bufs = []
    running = None
    for gi in range(N_GROUPS):
        buf, running = _attention_sample(qkv, new_kv[gi], caches[gi], shifted[gi], layer, running, s, gi)
        new_bufs.append(buf)
    hg, s_fin = _hgrn_sample(hq, hk, hv, logf, og, w["hg_norm_w"], states[layer], s)
    x1, _ = _merge(x, mod3, running, hg, gates, w["w_branch_a"], w["w_branch_b"], w["w_out"],
                   w["ln1_g"], w["ln1_b"], gb, rb, alpha)
    x2, _ = _mlp(x1, mod3, w["w_up"], w["b_up"], w["w_down"], w["b_down"], w["ln2_g"], w["ln2_b"], gb, rb, alpha)
    return x2, new_bufs, s_fin


def _stack_layers(per_layer):
    return per_layer[0][None] if len(per_layer) == 1 else jnp.stack(per_layer)


def kernel(x_prompt, x_sample, c_prompt, c_sample, cache_kv_w128, cache_kv_w512, cache_kv_w2048, state_hgrn,
           w_ada, b_ada, w_in, lb_param, hg_norm_w, w_branch_a, w_branch_b, w_out, ln1_g, ln1_b, w_up, b_up,
           w_down, b_down, ln2_g, ln2_b):
    depth = w_ada.shape[0]
    alpha = (2 * depth) ** 0.25
    caches = (cache_kv_w128, cache_kv_w512, cache_kv_w2048)
    for (window, dil), cache in zip(ATTN_GROUPS, caches):
        assert window // dil == BAND and cache.shape[2] == window
    assert x_prompt.shape[1] % (BAND * ATTN_GROUPS[-1][1]) == 0 and x_sample.shape[1] == SUBLANES
    views = tuple(c.reshape(c.shape[0], c.shape[1], c.shape[2] * c.shape[3] * c.shape[4], c.shape[5])
                  for c in caches)
    nb = c_prompt.shape[0]
    yp, ys = x_prompt, x_sample
    p_bufs, p_states, s_bufs, s_states = [], [], [], []
    for l in range(depth):
        w = dict(w_in=w_in[l].astype(BF16), lb_param=lb_param, hg_norm_w=hg_norm_w[l],
                 w_branch_a=w_branch_a[l].astype(BF16), w_branch_b=w_branch_b[l].astype(BF16),
                 w_out=w_out[l].astype(BF16), ln1_g=ln1_g[l], ln1_b=ln1_b[l],
                 w_up=w_up[l].astype(BF16), b_up=b_up[l], w_down=w_down[l].astype(BF16), b_down=b_down[l],
                 ln2_g=ln2_g[l], ln2_b=ln2_b[l])
        mod = _modulation(jnp.concatenate([c_prompt, c_sample], axis=0), w_ada[l], b_ada[l])
        yp, bufs_p, st_p, shifted = _prompt_layer(yp, mod[:nb], views, x_sample.shape[1], l, w, alpha)
        ys, bufs_s, st_s = _sample_layer(ys, mod[nb:], views, shifted, state_hgrn, l, w, alpha)
        p_bufs.append(bufs_p)
        p_states.append(st_p)
        s_bufs.append([buf.reshape(c.shape[1:]) for buf, c in zip(bufs_s, caches)])
        s_states.append(st_s)
    group = lambda bufs, gi: _stack_layers([bl[gi] for bl in bufs])
    return (yp, ys, group(p_bufs, 0), group(p_bufs, 1), group(p_bufs, 2), _stack_layers(p_states),
            group(s_bufs, 0), group(s_bufs, 1), group(s_bufs, 2), _stack_layers(s_states))
```

```python
import functools
import math
from typing import NamedTuple

import jax
import jax.numpy as jnp
from jax import lax
from jax.experimental import pallas as pl
from jax.experimental.pallas import tpu as pltpu

F32 = jnp.float32
BF16 = jnp.bfloat16

ATTN_GROUPS = ((128, 1), (512, 4), (2048, 16))
N_GROUPS = len(ATTN_GROUPS)
HEADS_PER_GROUP = 4
HEAD_DIM = 128
N_ATTN_HEADS = N_GROUPS * HEADS_PER_GROUP
ATTN_W = N_ATTN_HEADS * HEAD_DIM
GROUP_W = HEADS_PER_GROUP * HEAD_DIM
BAND = 128
HG_DK = 128
HG_DV = 128
LN_EPS = 1e-5
RMS_EPS = 1e-6
NEG = -1e30

SUBLANES = 8
LANES = 128
VMEM_LIMIT_BYTES = 56 * 1024 * 1024
COL_CHUNK = 512

NT_DIMS = (((1,), (1,)), ((), ()))
TN_DIMS = (((0,), (0,)), ((), ()))


def _alibi_slope(group, head):
    return 2.0 ** (-8.0 * (group * HEADS_PER_GROUP + head + 1) / N_ATTN_HEADS)


def _cparams(*sem):
    return pltpu.CompilerParams(dimension_semantics=sem, vmem_limit_bytes=VMEM_LIMIT_BYTES)


def _resident(shape):
    nd = len(shape)
    return pl.BlockSpec(shape, lambda *_: (0,) * nd, pipeline_mode=pl.Buffered(1))


def _sigmoid(x):
    return 1.0 / (1.0 + jnp.exp(-x))


def _layer_norm(h, g, b):
    mu = jnp.mean(h, axis=-1, keepdims=True)
    hc = h - mu
    var = jnp.mean(hc * hc, axis=-1, keepdims=True)
    return hc * lax.rsqrt(var + LN_EPS) * g + b


class _Shift(NamedTuple):
    layer: int
    b0: int
    nbatch: int
    cb: int
    rsplit: int
    k: int
    drop: int
    keep: int

    @property
    def n_chunks(self):
        return self.nbatch // self.cb * self.rsplit // self.k


def _plan_shift(layer, b0, nbatch, n_steps, drop, keep):
    if nbatch >= n_steps:
        cb = -(-nbatch // n_steps)
        while nbatch % cb:
            cb += 1
        return _Shift(layer, b0, nbatch, cb, 1, 1, drop, keep)
    best = None
    for r in (1, 2, 4, 8):
        if keep % (r * SUBLANES):
            continue
        k = -(-nbatch * r // n_steps)
        if nbatch * r % k == 0 and (best is None or k * best[0] < best[1] * r):
            best = (r, k)
    return _Shift(layer, b0, nbatch, 1, best[0], best[1], drop, keep)


def _shift_copies(shift, src_ref, dst_ref, buf, in_sem, out_sem, chunk, fetch):
    rows = shift.keep // shift.rsplit
    slot = chunk % 2
    copies = []
    for j in range(shift.k):
        part = chunk * shift.k + j
        b = shift.b0 + (part // shift.rsplit) * shift.cb
        r = (part % shift.rsplit) * rows
        if fetch:
            copies.append(pltpu.make_async_copy(
                src_ref.at[shift.layer, pl.ds(b, shift.cb), pl.ds(shift.drop + r, rows), :],
                buf.at[slot, j], in_sem.at[slot, j]))
        else:
            copies.append(pltpu.make_async_copy(
                buf.at[slot, j], dst_ref.at[pl.ds(b, shift.cb), pl.ds(r, rows), :], out_sem.at[slot, j]))
    return copies


def _host_shift(kernel_fn, shift, n_in, n_out, aliased, grid):
    n_steps = math.prod(grid)
    n_chunks = shift.n_chunks
    assert n_chunks <= n_steps

    def hosted(*refs):
        src = refs[n_in]
        pos = n_in + (2 if aliased else 1)
        outs = refs[pos:pos + n_out]
        dst = refs[pos + n_out]
        scratch = refs[pos + n_out + 1:-3]
        buf, in_sem, out_sem = refs[-3:]
        copies = functools.partial(_shift_copies, shift, src, dst, buf, in_sem, out_sem)

        def start(chunk, fetch):
            for copy in copies(chunk, fetch):
                copy.start()

        def wait(chunk, fetch):
            for copy in copies(chunk, fetch):
                copy.wait()

        def advance(s):
            if isinstance(s, int):
                if 1 <= s <= n_chunks:
                    wait(s - 1, True)
                    start(s - 1, False)
                if 2 <= s <= n_chunks + 1:
                    wait(s - 2, False)
                if s < n_chunks:
                    start(s, True)
            else:
                wait(s - 1, True)
                start(s - 1, False)
                wait(s - 2, False)
                start(s, True)

        step = 0
        for axis in range(len(grid)):
            step = step * grid[axis] + pl.program_id(axis)

        pl.when((step >= 2) & (step < n_chunks))(lambda: advance(step))
        for s in sorted({0, 1, n_chunks, n_chunks + 1}):
            if s < n_steps and not 2 <= s < n_chunks:
                pl.when(step == s)(functools.partial(advance, s))

        kernel_fn(*refs[:n_in], *outs, *scratch)

        @pl.when(step == n_steps - 1)
        def _():
            for s in range(n_steps, n_chunks + 2):
                advance(s)

    return hosted


def _hosting_call(kernel_fn, shift, cache, dst, *, out_shape, grid, in_specs, out_specs, args,
                  scratch_shapes=(), **kwargs):
    if shift is None:
        res = pl.pallas_call(kernel_fn, out_shape=tuple(out_shape), grid=grid, in_specs=list(in_specs),
                             out_specs=tuple(out_specs), scratch_shapes=list(scratch_shapes), **kwargs)(*args)
        return tuple(res), dst
    n_in, n_out = len(in_specs), len(out_shape)
    aliased = dst is not None
    anywhere = pl.BlockSpec(memory_space=pl.ANY)
    res = pl.pallas_call(
        _host_shift(kernel_fn, shift, n_in, n_out, aliased, grid),
        out_shape=tuple(out_shape) + (jax.ShapeDtypeStruct(cache.shape[1:], cache.dtype),),
        grid=grid,
        in_specs=list(in_specs) + [anywhere] * (2 if aliased else 1),
        out_specs=tuple(out_specs) + (anywhere,),
        scratch_shapes=list(scratch_shapes) + [
            pltpu.VMEM((2, shift.k, shift.cb, shift.keep // shift.rsplit, cache.shape[3]), cache.dtype),
            pltpu.SemaphoreType.DMA((2, shift.k)), pltpu.SemaphoreType.DMA((2, shift.k))],
        input_output_aliases={n_in + 1: n_out} if aliased else {},
        **kwargs,
    )(*args, cache, *([dst] if aliased else []))
    return tuple(res[:-1]), res[-1]


def _mod_kernel(c_ref, w_ref, b_ref, o_ref):
    c = c_ref[...]
    a = (c * _sigmoid(c)).astype(BF16)
    o_ref[...] = jnp.dot(a, w_ref[...].astype(BF16), preferred_element_type=F32) + b_ref[...]


def _modulation(c, w_ada, b_ada):
    n, d = c.shape
    nout = w_ada.shape[1]
    tn = 1024
    return pl.pallas_call(
        _mod_kernel,
        out_shape=jax.ShapeDtypeStruct((n, nout), F32),
        grid=(nout // tn,),
        in_specs=[pl.BlockSpec((n, d), lambda j: (0, 0)),
                  pl.BlockSpec((d, tn), lambda j: (0, j)),
                  pl.BlockSpec((1, tn), lambda j: (0, j))],
        out_specs=pl.BlockSpec((n, tn), lambda j: (0, j)),
        compiler_params=_cparams("arbitrary"),
        name="modulation",
    )(c, w_ada, b_ada.reshape(1, nout))


def _store_heads(ref, c0, y):
    for j in range(y.shape[1] // HG_DK):
        ref[c0 // HG_DK + j] = y[:, j * HG_DK:(j + 1) * HG_DK].astype(ref.dtype)


def _stage_kv(scr, kv, y):
    for h in range(HEADS_PER_GROUP):
        slot = (kv * HEADS_PER_GROUP + h) * SUBLANES
        for j in range(y.shape[0] // SUBLANES):
            scr[j * 64 + slot:j * 64 + slot + SUBLANES, :] = y[j * SUBLANES:(j + 1) * SUBLANES,
                                                               h * HEAD_DIM:(h + 1) * HEAD_DIM]


def _emit_kv(out_ref, scr, gb, rb, n):
    per_seq = n // SUBLANES
    for s in range(gb):
        for jj in range(per_seq):
            src = (s * (rb // SUBLANES) + (rb - n) // SUBLANES + jj) * 64
            for i in range(SUBLANES):
                out_ref[s, jj * 64 + i * SUBLANES:jj * 64 + (i + 1) * SUBLANES, :] = (
                    scr[pl.ds(src + i, SUBLANES, stride=SUBLANES), :])


def _inproj_kernel(x_ref, mod_ref, w_ref, lbp_ref, attn_ref, hq_ref, hk_ref, hv_ref, logf_ref, og_ref,
                   gates_ref, *refs, layer, kv_rows):
    kv_refs, kv_scr = refs[:N_GROUPS], refs[N_GROUPS]
    gb, rb, d = x_ref.shape
    tm = gb * rb
    hg_w = hq_ref.shape[0] * HG_DK
    act = attn_ref.dtype
    sh = mod_ref[:, :, 0:d]
    sc = mod_ref[:, :, d:2 * d]
    u = (x_ref[...] * (1.0 + sc) + sh).reshape(tm, d).astype(BF16)

    def proj(c0):
        return jnp.dot(u, w_ref[:, c0:c0 + COL_CHUNK], preferred_element_type=F32)

    assert COL_CHUNK == GROUP_W
    order = list(range(N_GROUPS)) + [s * N_GROUPS + g for g in range(N_GROUPS) for s in (1, 2)]
    for chunk in order:
        c0 = chunk * COL_CHUNK
        y = proj(c0)
        attn_ref[:, c0:c0 + COL_CHUNK] = y.astype(act)
        section, g = divmod(chunk, N_GROUPS)
        if section:
            _stage_kv(kv_scr, section - 1, y)
        if section == 2:
            _emit_kv(kv_refs[g], kv_scr, gb, rb, min(kv_rows[g], rb))
    base = 3 * ATTN_W
    for c0 in range(0, hg_w, COL_CHUNK):
        y = proj(base + c0)
        _store_heads(hq_ref, c0, y * _sigmoid(y))
    lbp = lbp_ref[...]
    e = jnp.exp(lbp - jnp.max(lbp, axis=0, keepdims=True))
    lb = jnp.sum(e[0:layer + 1], axis=0, keepdims=True) / jnp.sum(e, axis=0, keepdims=True)
    base += hg_w
    for c0 in range(0, hg_w, COL_CHUNK):
        lbc = lb[:, c0:c0 + COL_CHUNK]
        f = lbc + (1.0 - lbc) * _sigmoid(proj(base + c0))
        _store_heads(logf_ref, c0, jnp.log(f))
        _store_heads(hk_ref, c0, 1.0 - f)
    base += hg_w
    for c0 in range(0, hg_w, COL_CHUNK):
        _store_heads(hv_ref, c0, proj(base + c0))
    base += hg_w
    for c0 in range(0, hg_w, COL_CHUNK):
        y = proj(base + c0)
        _store_heads(og_ref, c0, y * _sigmoid(y))
    base += hg_w
    for c0 in range(0, 2 * d, COL_CHUNK):
        gates_ref[:, c0:c0 + COL_CHUNK] = _sigmoid(proj(base + c0)).astype(act)


def _in_projection(x3, mod3, w_in, lb_param, layer, gb, rb, act, hosted=(None, None, None)):
    g, r, d = x3.shape
    n = g * r
    tm = gb * rb
    hg_w = lb_param.shape[1]
    rt = r // rb
    heads = hg_w // HG_DK
    rows = lambda width: pl.BlockSpec((tm, width), lambda gi, ri: (gi * rt + ri, 0))
    by_head = pl.BlockSpec((heads, tm, HG_DK), lambda gi, ri: (0, gi * rt + ri, 0))
    head_major = lambda dtype: jax.ShapeDtypeStruct((heads, n, HG_DK), dtype)
    out_shapes = (
        jax.ShapeDtypeStruct((n, 3 * ATTN_W), act),
        head_major(act),
        head_major(act),
        head_major(act),
        head_major(F32),
        head_major(act),
        jax.ShapeDtypeStruct((n, 2 * d), act),
    )
    rp = 2 * HEADS_PER_GROUP
    kv_rows = tuple(min(window, r) for window, dil in ATTN_GROUPS)
    kv_specs = []
    for kept in kv_rows:
        assert (r - max(kept, rb)) % rb == 0
        first_tile = (r - max(kept, rb)) // rb
        kv_specs.append(pl.BlockSpec((gb, min(kept, rb) * rp, HEAD_DIM),
                                     lambda gi, ri, first_tile=first_tile: (gi, jnp.maximum(ri - first_tile, 0), 0)))
    out_shapes += tuple(jax.ShapeDtypeStruct((g, kept * rp, HEAD_DIM), F32) for kept in kv_rows)
    return _hosting_call(
        functools.partial(_inproj_kernel, layer=layer, kv_rows=kv_rows), *hosted,
        out_shape=out_shapes,
        grid=(g // gb, rt),
        in_specs=[pl.BlockSpec((gb, rb, d), lambda gi, ri: (gi, ri, 0)),
                  pl.BlockSpec((gb, 1, mod3.shape[2]), lambda gi, ri: (gi, 0, 0)),
                  _resident(w_in.shape),
                  _resident(lb_param.shape)],
        out_specs=(rows(3 * ATTN_W), by_head, by_head, by_head, by_head, by_head, rows(2 * d), *kv_specs),
        scratch_shapes=[pltpu.VMEM((tm * rp, HEAD_DIM), F32)],
        args=(x3, mod3, w_in, lb_param),
        compiler_params=_cparams("arbitrary", "arbitrary"),
        name="in_projection",
    )


ATTN_BLOCKS_PER_ITER = 8


def _attn_prompt_kernel(q_ref, k_ref, v_ref, attn_ref, qf, kf, vf, mf, lf, af, *, head_axis, group_axis):
    s = attn_ref.shape[0]
    nblk = s // BAND
    head = pl.program_id(head_axis)
    group = pl.program_id(group_axis)
    qi = lax.broadcasted_iota(jnp.int32, (BAND, BAND), 0)
    kj = lax.broadcasted_iota(jnp.int32, (BAND, BAND), 1)
    steps_own = qi - kj
    steps_prev = steps_own + BAND
    scale = HEAD_DIM ** -0.5
    ones = jnp.ones((BAND, HEAD_DIM), BF16)
    qf[...] = q_ref[...].astype(F32)
    kf[...] = k_ref[...].astype(F32)
    vf[...] = v_ref[...].astype(F32)
    for g, (window, dil) in enumerate(ATTN_GROUPS):
        nb = nblk // dil
        slope = jnp.float32(_alibi_slope(g, HEADS_PER_GROUP - 1) * dil)
        for h in range(HEADS_PER_GROUP - 1):
            slope = jnp.where(head == h, jnp.float32(_alibi_slope(g, h) * dil), slope)
        bias_own = jnp.where(steps_own >= 0, -slope * steps_own.astype(F32), NEG)
        bias_prev = jnp.where(steps_prev <= BAND, -slope * steps_prev.astype(F32), NEG)

        def body(it, carry, g=g, dil=dil, nb=nb, bias_own=bias_own, bias_prev=bias_prev):
            span = min(nb, ATTN_BLOCKS_PER_ITER)
            blocks = []
            for u in range(ATTN_BLOCKS_PER_ITER):
                idx = it * ATTN_BLOCKS_PER_ITER + u
                r = idx // nb
                i = idx - r * nb
                start = r + i * (BAND * dil)
                rows = pl.ds(start, BAND, stride=dil) if dil > 1 else pl.ds(start, BAND)
                if u % span:
                    blocks.append((rows, "chain", None, True))
                elif span == nb:
                    blocks.append((rows, "none", None, False))
                else:
                    start_prev = r + jnp.maximum(i - 1, 0) * (BAND * dil)
                    rows_prev = pl.ds(start_prev, BAND, stride=dil) if dil > 1 else pl.ds(start_prev, BAND)
                    blocks.append((rows, "load", rows_prev, i > 0))
            pair = lambda y, z: lax.dot_general(y, z, NT_DIMS, preferred_element_type=F32)
            scores, keys = [], []
            for rows, how, rows_prev, has_prev in blocks:
                q = qf[rows, :].astype(BF16)
                keys.append(kf[rows, :].astype(BF16))
                if how == "none":
                    scores.append((pair(q, keys[-1]), None))
                else:
                    k_prev = keys[-2] if how == "chain" else kf[rows_prev, :].astype(BF16)
                    scores.append((pair(q, keys[-1]), pair(q, k_prev)))
            probs = []
            for (rows, how, rows_prev, has_prev), (s_own, s_prev) in zip(blocks, scores):
                s_own = s_own * scale + bias_own
                if s_prev is None:
                    m = jnp.max(s_own, axis=-1, keepdims=True)
                    probs.append((m, jnp.exp(s_own - m).astype(BF16), None))
                else:
                    s_prev = s_prev * scale + bias_prev
                    if how == "load":
                        s_prev = jnp.where(has_prev, s_prev, NEG)
                    m = jnp.max(jnp.maximum(s_own, s_prev), axis=-1, keepdims=True)
                    probs.append((m, jnp.exp(s_own - m).astype(BF16), jnp.exp(s_prev - m).astype(BF16)))
            results, values = [], []
            for (rows, how, rows_prev, has_prev), (m, p_own, p_prev) in zip(blocks, probs):
                values.append(jnp.concatenate([vf[rows, :].astype(BF16), ones], axis=1))
                both = jnp.dot(p_own, values[-1], preferred_element_type=F32)
                if p_prev is not None:
                    v_prev = values[-2] if how == "chain" else jnp.concatenate(
                        [vf[rows_prev, :].astype(BF16), ones], axis=1)
                    both = both + jnp.dot(p_prev, v_prev, preferred_element_type=F32)
                acc, l = both[:, :HEAD_DIM], both[:, HEAD_DIM:]
                if g < N_GROUPS - 1:
                    m_old = mf[rows, :]
                    m_new = jnp.maximum(m_old, m)
                    a_old = jnp.exp(m_old - m_new)
                    a_blk = jnp.exp(m - m_new)
                    l = a_old * lf[rows, :] + a_blk * l
                    acc = a_old * af[rows, :] + a_blk * acc
                    m = m_new
                results.append((rows, m, l, acc))
            for rows, m, l, acc in results:
                mf[rows, :] = jnp.broadcast_to(m, (BAND, HEAD_DIM))
                lf[rows, :] = l
                af[rows, :] = acc
            return carry

        @pl.when(group == N_GROUPS - 1 - g)
        def _(body=body):
            lax.fori_loop(0, nblk // ATTN_BLOCKS_PER_ITER, body, 0)

    @pl.when(group == N_GROUPS - 1)
    def _():
        attn_ref[...] = (af[...] / lf[...]).astype(attn_ref.dtype)


def _attention_prompt(qkv, b, s, hosted=(None, None, None)):
    hpq = ATTN_W // HEAD_DIM
    in_specs = [pl.BlockSpec((s, HEAD_DIM), lambda bi, h, g, section=section:
                             (bi, section * hpq + (N_GROUPS - 1 - g) * HEADS_PER_GROUP + h))
                for section in range(3)]
    (attn,), dst = _hosting_call(
        functools.partial(_attn_prompt_kernel, head_axis=1, group_axis=2), *hosted,
        out_shape=(jax.ShapeDtypeStruct((b * s, GROUP_W), BF16),),
        grid=(b, HEADS_PER_GROUP, N_GROUPS),
        in_specs=in_specs,
        out_specs=(pl.BlockSpec((s, HEAD_DIM), lambda bi, h, g: (bi, h)),),
        scratch_shapes=[pltpu.VMEM((s, HEAD_DIM), F32) for _ in range(6)],
        args=[qkv] * 3,
        compiler_params=_cparams("arbitrary", "arbitrary", "arbitrary"),
        name="attn_prompt",
    )
    return attn, dst


def _attn_sample_kernel(*refs, group, dil, n_steps, length, seqs):
    first = group == 0
    last = group == N_GROUPS - 1
    q_ref, newkv_ref, cache_ref = refs[:3]
    pos = 3
    if not first:
        m_in, l_in, a_in = refs[pos:pos + 3]
        pos += 3
    pos += 1
    newc_ref = refs[pos]
    outs = refs[pos + 1:]
    t = q_ref.shape[0] // seqs
    rp = 2 * HEADS_PER_GROUP
    stretch = max(dil, t)
    nkeys = length // stretch * t
    j_c = lax.broadcasted_iota(jnp.int32, (t, nkeys), 0)
    c_c = lax.broadcasted_iota(jnp.int32, (t, nkeys), 1)
    dist_c = length + j_c - ((c_c // t) * stretch + (c_c & (t - 1)))
    ok_c = ((dist_c & (dil - 1)) == 0) & (dist_c <= n_steps * dil)
    j_n = lax.broadcasted_iota(jnp.int32, (t, t), 0)
    r_n = lax.broadcasted_iota(jnp.int32, (t, t), 1)
    dist_n = j_n - r_n
    ok_n = (dist_n >= 0) & ((dist_n & (dil - 1)) == 0) & (dist_n <= n_steps * dil)
    scale = HEAD_DIM ** -0.5
    qoff = group * GROUP_W
    cache_rows = lambda s, first_row: cache_ref[s, :, pl.ds(first_row, t, stride=rp), :].reshape(
        nkeys, HEAD_DIM).astype(BF16)
    new_rows = lambda s, first_row: newkv_ref[s, pl.ds(first_row, t, stride=rp), :].astype(BF16)
    chains = [(s, h) for s in range(seqs) for h in range(HEADS_PER_GROUP)]
    scores = []
    for s, h in chains:
        q = q_ref[s * t:(s + 1) * t, qoff + h * HEAD_DIM:qoff + (h + 1) * HEAD_DIM].astype(BF16)
        scores.append((lax.dot_general(q, cache_rows(s, h), NT_DIMS, preferred_element_type=F32),
                       lax.dot_general(q, new_rows(s, h), NT_DIMS, preferred_element_type=F32)))
    probs = []
    for (s, h), (s_c, s_n) in zip(chains, scores):
        rows, sl = slice(s * t, (s + 1) * t), slice(h * HEAD_DIM, (h + 1) * HEAD_DIM)
        slope = _alibi_slope(group, h)
        s_c = jnp.where(ok_c, s_c * scale - slope * dist_c.astype(F32), NEG)
        s_n = jnp.where(ok_n, s_n * scale - slope * dist_n.astype(F32), NEG)
        m = jnp.maximum(jnp.max(s_c, axis=-1, keepdims=True), jnp.max(s_n, axis=-1, keepdims=True))
        if not first:
            m = jnp.maximum(m, m_in[rows, sl][:, 0:1])
        p_c = jnp.exp(s_c - m)
        p_n = jnp.exp(s_n - m)
        l = jnp.sum(p_c, axis=-1, keepdims=True) + jnp.sum(p_n, axis=-1, keepdims=True)
        probs.append((m, l, p_c.astype(BF16), p_n.astype(BF16)))
    for (s, h), (m, l, p_c, p_n) in zip(chains, probs):
        rows, sl = slice(s * t, (s + 1) * t), slice(h * HEAD_DIM, (h + 1) * HEAD_DIM)
        acc = (jnp.dot(p_c, cache_rows(s, HEADS_PER_GROUP + h), preferred_element_type=F32)
               + jnp.dot(p_n, new_rows(s, HEADS_PER_GROUP + h), preferred_element_type=F32))
        if not first:
            a_old = jnp.exp(m_in[rows, sl] - m)
            l = a_old * l_in[rows, sl] + l
            acc = a_old * a_in[rows, sl] + acc
        if last:
            outs[0][rows, sl] = acc / l
        else:
            outs[0][rows, sl] = jnp.broadcast_to(m, (t, HEAD_DIM))
            outs[1][rows, sl] = jnp.broadcast_to(l, (t, HEAD_DIM))
            outs[2][rows, sl] = acc
    for s in range(seqs):
        newc_ref[s] = newkv_ref[s]


SAMPLE_SEQS_PER_STEP = (8, 4, 2)


def _attention_sample(qkv, new_kv, cache, shifted, layer, running, t, group):
    window, dil = ATTN_GROUPS[group]
    depth, bd, crow, dh = cache.shape
    rp = 2 * HEADS_PER_GROUP
    length = crow // rp
    stretch = max(dil, t)
    seqs = math.gcd(SAMPLE_SEQS_PER_STEP[group], bd)
    first = group == 0
    last = group == N_GROUPS - 1
    rows = pl.BlockSpec((seqs * t, GROUP_W), lambda bi: (bi, 0))
    stat = jax.ShapeDtypeStruct((bd * t, GROUP_W), F32)
    new_blk = pl.BlockSpec((seqs, t * rp, dh), lambda bi: (bi, 0, 0))
    in_specs = [pl.BlockSpec((seqs * t, 3 * ATTN_W), lambda bi: (bi, 0)),
                new_blk,
                pl.BlockSpec((None, seqs, length // stretch, t * rp, dh), lambda bi: (layer, bi, 0, 0, 0))]
    args = [qkv, new_kv, cache.reshape(depth, bd, length // stretch, stretch * rp, dh)]
    if not first:
        in_specs += [rows, rows, rows]
        args += list(running)
    in_specs.append(pl.BlockSpec(memory_space=pl.ANY))
    args.append(shifted)
    n_out = 1 if last else 3
    res = pl.pallas_call(
        functools.partial(_attn_sample_kernel, group=group, dil=dil, n_steps=window // dil, length=length,
                          seqs=seqs),
        out_shape=(jax.ShapeDtypeStruct(shifted.shape, shifted.dtype),) + (stat,) * n_out,
        grid=(bd // seqs,),
        in_specs=in_specs,
        out_specs=(pl.BlockSpec((seqs, t * rp, dh), lambda bi: (bi, length // t - 1, 0)),) + (rows,) * n_out,
        input_output_aliases={len(args) - 1: 0},
        compiler_params=_cparams("arbitrary"),
        name=f"attn_sample_g{group}",
    )(*args)
    return res[0], (res[1] if last else tuple(res[1:]))


def _cumsum_rows(x, row):
    d = 1
    while d < x.shape[0]:
        x = x + jnp.where(row >= d, pltpu.roll(x, d, 0), 0.0)
        d *= 2
    return x


def _hgrn_carry(q, k, v, g, st):
    c = q.shape[0]
    qg = (q * jnp.exp(g)).astype(BF16)
    o = lax.dot_general(qg, st.astype(BF16), NT_DIMS, preferred_element_type=F32)
    g_last = g[c - 1:c, :]
    kd = (k * jnp.exp(g_last - g)).astype(BF16)
    st_new = st * jnp.exp(g_last) + lax.dot_general(v.astype(BF16), kd, TN_DIMS, preferred_element_type=F32)
    return o, st_new


def _hgrn_chunks_pairwise(qs, ks, vs, lfs, states):
    c = qs[0].shape[0]
    row = lax.broadcasted_iota(jnp.int32, qs[0].shape, 0)
    gs = [_cumsum_rows(lf, row) for lf in lfs]
    outs = []
    for q, k, v, g in zip(qs, ks, vs, gs):
        o = jnp.sum(q * k, axis=-1, keepdims=True) * v
        for d in range(1, c):
            ok = row >= d
            e = jnp.exp(jnp.where(ok, g - pltpu.roll(g, d, 0), 0.0))
            a = jnp.sum(jnp.where(ok, q * pltpu.roll(k, d, 0) * e, 0.0), axis=-1, keepdims=True)
            o = o + a * pltpu.roll(v, d, 0)
        outs.append(o)
    cols = []
    for k, g in zip(ks, gs):
        g_last = g[c - 1:c, :]
        kd = k * jnp.exp(g_last - g)
        cols.append(jnp.concatenate([kd, jnp.broadcast_to(jnp.exp(g_last), kd.shape)], axis=0).T)
    carried = [jnp.dot((q * jnp.exp(g)).astype(BF16), s.astype(BF16), preferred_element_type=F32)
               for q, g, s in zip(qs, gs, states)]
    new_states = [s * col[:, c:c + 1] + jnp.dot(col[:, 0:c].astype(BF16), v.astype(BF16),
                                                preferred_element_type=F32)
                  for s, col, v in zip(states, cols, vs)]
    return [o + oc for o, oc in zip(outs, carried)], new_states


def _pair_level(c):
    ri = lax.broadcasted_iota(jnp.int32, (c, c), 0)
    ci = lax.broadcasted_iota(jnp.int32, (c, c), 1)
    x = ri ^ ci
    bits = jnp.zeros((c, c), jnp.int32)
    p = 1
    while p < c:
        bits = bits + (x >= p).astype(jnp.int32)
        p *= 2
    return jnp.where(ri > ci, bits + 1, jnp.where(ri == ci, 1, 0))


def _hgrn_chunks_blocked(qs, ks, vs, lfs, sts, g_refs, level):
    nh = len(qs)
    c = qs[0].shape[0]
    row = lax.broadcasted_iota(jnp.int32, (c, HG_DK), 0)
    sub = row & (SUBLANES - 1)
    sub8 = lax.broadcasted_iota(jnp.int32, (SUBLANES, HG_DK), 0)
    gs = []
    for lf, g_ref in zip(lfs, g_refs):
        x = lf
        for d in (1, 2, 4):
            x = x + jnp.where(sub >= d, pltpu.roll(x, d, 0), 0.0)
        g_ref[...] = x
        pieces = [x[0:SUBLANES]]
        off = None
        for j in range(1, c // SUBLANES):
            tot = jnp.broadcast_to(g_ref[j * SUBLANES - 1:j * SUBLANES, :], (SUBLANES, HG_DK))
            off = tot if off is None else off + tot
            pieces.append(x[j * SUBLANES:(j + 1) * SUBLANES] + off)
        g = jnp.concatenate(pieces, axis=0)
        g_ref[...] = g
        gs.append(g)

    def mid_rows(g_ref, n):
        half = n // 2
        bcast = lambda r: jnp.broadcast_to(g_ref[r:r + 1, :], (SUBLANES, HG_DK))
        out = []
        for j in range(c // SUBLANES):
            base = j * SUBLANES
            if n >= SUBLANES:
                out.append(bcast(base // n * n + half - 1))
            else:
                assert n == 4
                out.append(jnp.where(sub8 < 4, bcast(base + 1), bcast(base + 5)))
        return jnp.concatenate(out, axis=0)

    pair = lambda y, z: lax.dot_general(y, z, NT_DIMS, preferred_element_type=F32)
    prods = [[pair(q.astype(BF16), k.astype(BF16))] for q, k in zip(qs, ks)]
    n = 2
    while n <= c:
        second = (row & (n // 2)) != 0
        for u in range(nh):
            if n == 2:
                arg = jnp.where(second, lfs[u], 0.0)
            else:
                gmid = mid_rows(g_refs[u], n)
                arg = jnp.where(second, gs[u] - gmid, gmid - gs[u])
            y = (jnp.where(second, qs[u], ks[u]) * jnp.exp(arg)).astype(BF16)
            prods[u].append(pair(y, y))
        n *= 2
    outs = []
    for u in range(nh):
        amat = jnp.where(level == 1, prods[u][0], 0.0)
        for j, p in enumerate(prods[u][1:]):
            amat = jnp.where(level == j + 2, p, amat)
        outs.append(jnp.dot(amat.astype(BF16), vs[u].astype(BF16), preferred_element_type=F32))
    states = []
    for u in range(nh):
        o_carry, st_new = _hgrn_carry(qs[u], ks[u], vs[u], gs[u], sts[u])
        outs[u] = outs[u] + o_carry
        states.append(st_new)
    return outs, states


def _hgrn_output(o, og, nw):
    return o * lax.rsqrt(jnp.mean(o * o, axis=-1, keepdims=True) + RMS_EPS) * nw * og


HGRN_HEADS_PER_ITER = 8


HGRN_CHUNKS_PER_STEP = 4


def _hgrn_prompt_kernel(q_ref, k_ref, v_ref, lf_ref, og_ref, nw_ref, hg_ref, sfin_ref, st_ref, g_ref):
    ci = pl.program_id(1)
    heads = q_ref.shape[0]
    c = g_ref.shape[1]
    head_groups = heads // HGRN_HEADS_PER_ITER

    @pl.when(ci == 0)
    def _():
        st_ref[...] = jnp.zeros_like(st_ref)

    level = _pair_level(c)

    def head_group(it, carry):
        i = it % head_groups
        rows = pl.ds(pl.multiple_of(it // head_groups * c, c), c)
        hs = [i * HGRN_HEADS_PER_ITER + u for u in range(HGRN_HEADS_PER_ITER)]
        load = lambda ref: [ref[h, rows, :].astype(F32) for h in hs]
        outs, states = _hgrn_chunks_blocked(load(q_ref), load(k_ref), load(v_ref), load(lf_ref),
                                            [st_ref[h] for h in hs],
                                            [g_ref.at[u] for u in range(HGRN_HEADS_PER_ITER)], level)
        for h, o, st_new in zip(hs, outs, states):
            st_ref[h] = st_new
            hg_ref[h, rows, :] = _hgrn_output(o, og_ref[h, rows, :].astype(F32), nw_ref[...]).astype(hg_ref.dtype)
        return carry

    lax.fori_loop(0, q_ref.shape[1] // c * head_groups, head_group, 0)

    @pl.when(ci == pl.num_programs(1) - 1)
    def _():
        for h in range(heads):
            sfin_ref[0, h] = st_ref[h].T


def _hgrn_prompt(hq, hk, hv, logf, og, norm_w, b, s, chunk, hosted=(None, None, None)):
    heads, n, _ = hq.shape
    rows = chunk * math.gcd(HGRN_CHUNKS_PER_STEP, s // chunk)
    nc = s // rows
    blk = pl.BlockSpec((heads, rows, HG_DK), lambda bi, ci: (0, bi * nc + ci, 0))
    (hg, s_fin), dst = _hosting_call(
        _hgrn_prompt_kernel, *hosted,
        out_shape=(jax.ShapeDtypeStruct((heads, n, HG_DV), BF16),
                   jax.ShapeDtypeStruct((b, heads, HG_DK, HG_DV), F32)),
        grid=(b, nc),
        in_specs=[blk, blk, blk, blk, blk, pl.BlockSpec((1, HG_DV), lambda bi, ci: (0, 0))],
        out_specs=(blk, pl.BlockSpec((1, heads, HG_DK, HG_DV), lambda bi, ci: (bi, 0, 0, 0))),
        scratch_shapes=[pltpu.VMEM((heads, HG_DV, HG_DK), F32),
                        pltpu.VMEM((HGRN_HEADS_PER_ITER, chunk, HG_DK), F32)],
        args=(hq, hk, hv, logf, og, norm_w.reshape(1, HG_DV)),
        compiler_params=_cparams("arbitrary", "arbitrary"),
        name="hgrn_prompt",
    )
    return hg, s_fin, dst


HGRN_SAMPLE_SEQS_PER_STEP = 2


def _hgrn_sample_kernel(q_ref, k_ref, v_ref, lf_ref, og_ref, nw_ref, s_ref, hg_ref, snew_ref):
    seqs, heads = s_ref.shape[0], s_ref.shape[1]
    t = q_ref.shape[1] // seqs
    chains = [(s, h) for s in range(seqs) for h in range(heads)]
    load = lambda ref: [ref[h, s * t:(s + 1) * t, :] for s, h in chains]
    outs, new_states = _hgrn_chunks_pairwise(load(q_ref), load(k_ref), load(v_ref), load(lf_ref),
                                             [s_ref[s, h] for s, h in chains])
    for (s, h), o, st_new in zip(chains, outs, new_states):
        snew_ref[s, h] = st_new
        hg_ref[h, s * t:(s + 1) * t, :] = _hgrn_output(o, og_ref[h, s * t:(s + 1) * t, :], nw_ref[...])


def _hgrn_sample(hq, hk, hv, logf, og, norm_w, state, t):
    bd, heads = state.shape[0], state.shape[1]
    seqs = math.gcd(HGRN_SAMPLE_SEQS_PER_STEP, bd)
    blk = pl.BlockSpec((heads, seqs * t, HG_DK), lambda bi: (0, bi, 0))
    sblk = pl.BlockSpec((seqs, heads, HG_DK, HG_DV), lambda bi: (bi, 0, 0, 0))
    return pl.pallas_call(
        _hgrn_sample_kernel,
        out_shape=(jax.ShapeDtypeStruct((heads, bd * t, HG_DV), F32),
                   jax.ShapeDtypeStruct(state.shape, state.dtype)),
        grid=(bd // seqs,),
        in_specs=[blk, blk, blk, blk, blk, pl.BlockSpec((1, HG_DV), lambda bi: (0, 0)), sblk],
        out_specs=(blk, sblk),
        compiler_params=_cparams("arbitrary"),
        name="hgrn_sample",
    )(hq, hk, hv, logf, og, norm_w.reshape(1, HG_DV), state)


def _merge_kernel(x_ref, mod_ref, attn_ref, hg_ref, gates_ref, wa_ref, wb_ref, wo_ref, lng_ref, lnb_ref,
                  x1_ref, *, alpha):
    gb, rb, d = x_ref.shape
    tm = gb * rb
    hg = jnp.concatenate([hg_ref[h].astype(BF16) for h in range(hg_ref.shape[0])], axis=1)
    branch_a = jnp.dot(attn_ref[...].astype(BF16), wa_ref[...], preferred_element_type=F32)
    branch_b = jnp.dot(hg, wb_ref[...], preferred_element_type=F32)
    gate_a = gates_ref[:, 0:d].astype(F32)
    gate_b = gates_ref[:, d:2 * d].astype(F32)
    merged = gate_a * branch_a + gate_b * branch_b
    mix = jnp.dot(merged.astype(BF16), wo_ref[...], preferred_element_type=F32)
    g1 = mod_ref[:, :, 2 * d:3 * d]
    h = alpha * x_ref[...] + g1 * mix.reshape(gb, rb, d)
    x1_ref[...] = _layer_norm(h, lng_ref[...], lnb_ref[...])


def _merge(x3, mod3, attn, hg, gates, wa, wb, wo, ln_g, ln_b, gb, rb, alpha, hosted=(None, None, None)):
    g, r, d = x3.shape
    tm = gb * rb
    rt = r // rb
    row = lambda gi, ri: (gi * rt + ri, 0)
    rows = lambda a: pl.BlockSpec((tm, a.shape[1]), row)
    tile3 = pl.BlockSpec((gb, rb, d), lambda gi, ri: (gi, ri, 0))
    (x1,), dst = _hosting_call(
        functools.partial(_merge_kernel, alpha=alpha), *hosted,
        out_shape=(jax.ShapeDtypeStruct(x3.shape, F32),),
        grid=(g // gb, rt),
        in_specs=[tile3, pl.BlockSpec((gb, 1, mod3.shape[2]), lambda gi, ri: (gi, 0, 0)),
                  rows(attn), pl.BlockSpec((hg.shape[0], tm, hg.shape[2]), lambda gi, ri: (0, gi * rt + ri, 0)),
                  rows(gates),
                  _resident(wa.shape), _resident(wb.shape), _resident(wo.shape),
                  _resident((1, d)), _resident((1, d))],
        out_specs=(tile3,),
        args=(x3, mod3, attn, hg, gates, wa, wb, wo, ln_g.reshape(1, d), ln_b.reshape(1, d)),
        compiler_params=_cparams("arbitrary", "arbitrary"),
        name="merge",
    )
    return x1, dst


def _mlp_kernel(x1_ref, mod_ref, wu_ref, bu_ref, wd_ref, bd_ref, lng_ref, lnb_ref, x2_ref, *, alpha):
    gb, rb, d = x1_ref.shape
    tm = gb * rb
    dff = wu_ref.shape[1]
    x1 = x1_ref[...]
    sh = mod_ref[:, :, 3 * d:4 * d]
    sc = mod_ref[:, :, 4 * d:5 * d]
    g2 = mod_ref[:, :, 5 * d:6 * d]
    u = (x1 * (1.0 + sc) + sh).reshape(tm, d).astype(BF16)
    hidden = []
    for c0 in range(0, dff, COL_CHUNK):
        hid = jnp.dot(u, wu_ref[:, c0:c0 + COL_CHUNK], preferred_element_type=F32) + bu_ref[:, c0:c0 + COL_CHUNK]
        hidden.append(jnp.square(jnp.maximum(hid, 0.0)).astype(BF16))
    ff = jnp.dot(jnp.concatenate(hidden, axis=1), wd_ref[...], preferred_element_type=F32) + bd_ref[...]
    h = alpha * x1 + g2 * ff.reshape(gb, rb, d)
    x2_ref[...] = _layer_norm(h, lng_ref[...], lnb_ref[...])


def _mlp(x3, mod3, wu, bu, wd, bd, ln_g, ln_b, gb, rb, alpha, hosted=(None, None, None)):
    g, r, d = x3.shape
    dff = wu.shape[1]
    tile3 = pl.BlockSpec((gb, rb, d), lambda gi, ri: (gi, ri, 0))
    (x2,), dst = _hosting_call(
        functools.partial(_mlp_kernel, alpha=alpha), *hosted,
        out_shape=(jax.ShapeDtypeStruct(x3.shape, F32),),
        grid=(g // gb, r // rb),
        in_specs=[tile3, pl.BlockSpec((gb, 1, mod3.shape[2]), lambda gi, ri: (gi, 0, 0)),
                  _resident(wu.shape), _resident((1, dff)), _resident(wd.shape), _resident((1, d)),
                  _resident((1, d)), _resident((1, d))],
        out_specs=(tile3,),
        args=(x3, mod3, wu, bu.reshape(1, dff), wd, bd.reshape(1, d), ln_g.reshape(1, d), ln_b.reshape(1, d)),
        compiler_params=_cparams("arbitrary", "arbitrary"),
        name="mlp",
    )
    return x2, dst


PROMPT_ROWS = 256
MERGE_ROWS = 512
SAMPLE_SEQS = 32
HGRN_CHUNK = 128


def _prompt_layer(x, mod, caches, t_new, layer, w, alpha):
    b, s, d = x.shape
    mod3 = mod.reshape(b, 1, mod.shape[1])
    gb, rb = 1, min(PROMPT_ROWS, s)
    chunk = min(HGRN_CHUNK, s)
    rp = 2 * HEADS_PER_GROUP
    bd = caches[0].shape[1]
    tile_steps = s // rb * b
    hgrn_steps = s // chunk // math.gcd(HGRN_CHUNKS_PER_STEP, s // chunk) * b

    def plan(gi, b0, nbatch, n_steps):
        rows = caches[gi].shape[2]
        return _plan_shift(layer, b0, nbatch, n_steps, t_new * rp, rows - t_new * rp)

    attn_steps = b * HEADS_PER_GROUP * N_GROUPS
    n_hgrn = bd // 4
    n_attn = (bd - n_hgrn) // 2
    big = None

    def share(b0, nbatch, n_steps):
        return (plan(2, b0, nbatch, n_steps), caches[2], big) if nbatch else (None, None, big)

    (qkv, hq, hk, hv, logf, og, gates, *kv_tails), mid = _in_projection(
        x, mod3, w["w_in"], w["lb_param"], layer, gb, rb, BF16,
        hosted=(plan(1, 0, bd, tile_steps), caches[1], None))
    hg, s_fin, big = _hgrn_prompt(hq, hk, hv, logf, og, w["hg_norm_w"], b, s, chunk,
                                  hosted=share(0, n_hgrn, hgrn_steps))
    attn, big = _attention_prompt(qkv, b, s, hosted=share(n_hgrn, n_attn, attn_steps))
    new_bufs = [kv.reshape(b, kv.shape[1] // rp, 2, HEADS_PER_GROUP, HEAD_DIM) for kv in kv_tails]
    rb_merge = min(MERGE_ROWS, s)
    x1, small = _merge(x, mod3, attn, hg, gates, w["w_branch_a"], w["w_branch_b"], w["w_out"],
                       w["ln1_g"], w["ln1_b"], gb, rb_merge, alpha,
                       hosted=(plan(0, 0, bd, s // rb_merge * b), caches[0], None))
    x2, big = _mlp(x1, mod3, w["w_up"], w["b_up"], w["w_down"], w["b_down"], w["ln2_g"], w["ln2_b"], gb, rb,
                   alpha, hosted=share(n_hgrn + n_attn, bd - n_hgrn - n_attn, tile_steps))
    return x2, new_bufs, s_fin, (small, mid, big)


def _sample_layer(x, mod, caches, shifted, states, layer, w, alpha):
    b, s, d = x.shape
    mod3 = mod.reshape(b, 1, mod.shape[1])
    gb, rb = min(SAMPLE_SEQS, b), s
    rp = 2 * HEADS_PER_GROUP
    (qkv, hq, hk, hv, logf, og, gates, *new_kv), _ = _in_projection(
        x, mod3, w["w_in"], w["lb_param"], layer, gb, rb, F32)
    new_bufs = []
    running = None
    for gi in range(N_GROUPS):
        buf, running = _attention_sample(qkv, new_kv[gi], caches[gi], shifted[gi], layer, running, s, gi)
        new_bufs.append(buf)
    hg, s_fin = _hgrn_sample(hq, hk, hv, logf, og, w["hg_norm_w"], states[layer], s)
    x1, _ = _merge(x, mod3, running, hg, gates, w["w_branch_a"], w["w_branch_b"], w["w_out"],
                   w["ln1_g"], w["ln1_b"], gb, rb, alpha)
    x2, _ = _mlp(x1, mod3, w["w_up"], w["b_up"], w["w_down"], w["b_down"], w["ln2_g"], w["ln2_b"], gb, rb, alpha)
    return x2, new_bufs, s_fin


def _stack_layers(per_layer):
    return per_layer[0][None] if len(per_layer) == 1 else jnp.stack(per_layer)


def kernel(x_prompt, x_sample, c_prompt, c_sample, cache_kv_w128, cache_kv_w512, cache_kv_w2048, state_hgrn,
           w_ada, b_ada, w_in, lb_param, hg_norm_w, w_branch_a, w_branch_b, w_out, ln1_g, ln1_b, w_up, b_up,
           w_down, b_down, ln2_g, ln2_b):
    depth = w_ada.shape[0]
    alpha = (2 * depth) ** 0.25
    caches = (cache_kv_w128, cache_kv_w512, cache_kv_w2048)
    for (window, dil), cache in zip(ATTN_GROUPS, caches):
        assert window // dil == BAND and cache.shape[2] == window
    assert x_prompt.shape[1] % (BAND * ATTN_GROUPS[-1][1]) == 0 and x_sample.shape[1] == SUBLANES
    views = tuple(c.reshape(c.shape[0], c.shape[1], c.shape[2] * c.shape[3] * c.shape[4], c.shape[5])
                  for c in caches)
    nb = c_prompt.shape[0]
    yp, ys = x_prompt, x_sample
    p_bufs, p_states, s_bufs, s_states = [], [], [], []
    for l in range(depth):
        w = dict(w_in=w_in[l].astype(BF16), lb_param=lb_param, hg_norm_w=hg_norm_w[l],
                 w_branch_a=w_branch_a[l].astype(BF16), w_branch_b=w_branch_b[l].astype(BF16),
                 w_out=w_out[l].astype(BF16), ln1_g=ln1_g[l], ln1_b=ln1_b[l],
                 w_up=w_up[l].astype(BF16), b_up=b_up[l], w_down=w_down[l].astype(BF16), b_down=b_down[l],
                 ln2_g=ln2_g[l], ln2_b=ln2_b[l])
        mod = _modulation(jnp.concatenate([c_prompt, c_sample], axis=0), w_ada[l], b_ada[l])
        yp, bufs_p, st_p, shifted = _prompt_layer(yp, mod[:nb], views, x_sample.shape[1], l, w, alpha)
        ys, bufs_s, st_s = _sample_layer(ys, mod[nb:], views, shifted, state_hgrn, l, w, alpha)
        p_bufs.append(bufs_p)
        p_states.append(st_p)
        s_bufs.append([buf.reshape(c.shape[1:]) for buf, c in zip(bufs_s, caches)])
        s_states.append(st_s)
    group = lambda bufs, gi: _stack_layers([bl[gi] for bl in bufs])
    return (yp, ys, group(p_bufs, 0), group(p_bufs, 1), group(p_bufs, 2), _stack_layers(p_states),
            group(s_bufs, 0), group(s_bufs, 1), group(s_bufs, 2), _stack_layers(s_states))
```
